```python
import math
import jax, jax.numpy as jnp
from jax import lax
import numpy as np


D_MODEL = 2048
BATCH = 4
SEQ = 2048
DEPTH = 4
DEC_BATCH = 8
DEC_SEQ = 1
PAST_LEN = 16384
PAGE_SIZE = 128

N_MIX = (DEPTH + 1) // 2
N_ATTN = DEPTH // 2
D_POOL = D_MODEL // 2
POOL_WINDOWS = (2, 4, 8, 16)
POOL_GROUPS = len(POOL_WINDOWS)
POOL_GC = D_POOL // POOL_GROUPS
POOL_BUF = max(POOL_WINDOWS) - 1
MH = 4
D_MLSTM = D_MODEL // 2
DK = D_MLSTM // MH
DV = D_MLSTM // MH
MLSTM_CHUNK = 64
D_IN_MIX = D_POOL + 4 * D_MLSTM + 2 * MH
D_MIX = D_POOL + D_MLSTM
N_HEADS = 16
HEAD_DIM = D_MODEL // N_HEADS
N_KV_HEADS = 4
GROUP = N_HEADS // N_KV_HEADS
D_Q = N_HEADS * HEAD_DIM
D_KV = N_KV_HEADS * HEAD_DIM
MOBA_BLOCK = 256
MOBA_TOPK = 3
Q_BLOCK = 8
D_FF = 4 * D_MODEL
ALPHA = (2 * DEPTH) ** 0.25
BETA = (8 * DEPTH) ** -0.25
LN_EPS = 1e-5
HEAD_NORM_EPS = 1e-6

kernel_name = 'hybrid_pool_mlstm_moba_deepnorm_step'


def layer_norm(x, g, b):
    xf = x.astype(jnp.float32)
    mu = jnp.mean(xf, axis=-1, keepdims=True)
    var = jnp.mean(jnp.square(xf - mu), axis=-1, keepdims=True)
    return ((xf - mu) * lax.rsqrt(var + LN_EPS) * g + b).astype(x.dtype)


def sq_relu_mlp(x, w_up, w_down):
    return jnp.square(jax.nn.relu(x @ w_up)) @ w_down


def pool_mixer(u, buf, pos0, w_pool, pool_scale):
    B, S, _ = u.shape
    ext = jnp.concatenate([buf.astype(u.dtype), u], axis=1).astype(jnp.float32)
    cs = jnp.concatenate([jnp.zeros((B, 1, D_POOL), jnp.float32), jnp.cumsum(ext, axis=1)], axis=1)
    pos = pos0 + jnp.arange(S)
    outs = []
    for g, w in enumerate(POOL_WINDOWS):
        c0, c1 = g * POOL_GC, (g + 1) * POOL_GC
        hi = cs[:, POOL_BUF + 1:POOL_BUF + 1 + S, c0:c1]
        lo = cs[:, POOL_BUF + 1 - w:POOL_BUF + 1 - w + S, c0:c1]
        cnt = jnp.minimum(w, pos + 1).astype(jnp.float32)[None, :, None]
        outs.append((hi - lo) / cnt - ext[:, POOL_BUF:, c0:c1])
    d = jnp.stack(outs, axis=2)
    y = jnp.einsum('bsgc,gcd->bsgd', d, w_pool.astype(jnp.float32)).reshape(B, S, D_POOL)
    y = y * pool_scale
    return y.astype(u.dtype), ext[:, -POOL_BUF:].astype(u.dtype)


def mlstm_chunk(carry, xs):
    C, n, m = carry
    q, k, v, ig, lf = xs
    L = q.shape[2]
    b = jnp.cumsum(lf, axis=-1)
    a = ig - b
    m_t = b + jnp.maximum(m[..., None], lax.cummax(a, axis=2))
    w_inter = jnp.exp(b + m[..., None] - m_t)
    causal = jnp.tril(jnp.ones((L, L), dtype=bool))
    log_d = a[..., None, :] + b[..., :, None] - m_t[..., :, None]
    dmat = jnp.exp(jnp.where(causal, log_d, -jnp.inf))
    s = jnp.einsum('bhtd,bhsd->bhts', q, k) * dmat
    num = w_inter[..., None] * jnp.einsum('bhtd,bhde->bhte', q, C) + jnp.einsum('bhts,bhse->bhte', s, v)
    den = w_inter * jnp.einsum('bhtd,bhd->bht', q, n) + jnp.sum(s, axis=-1)
    h = num / jnp.maximum(jnp.abs(den), jnp.exp(-m_t))[..., None]
    m_new = m_t[..., -1]
    b_last = b[..., -1]
    w_old = jnp.exp(b_last + m - m_new)
    w_s = jnp.exp(a + b_last[..., None] - m_new[..., None])
    C_new = w_old[..., None, None] * C + jnp.einsum('bhs,bhsd,bhse->bhde', w_s, k, v)
    n_new = w_old[..., None] * n + jnp.einsum('bhs,bhsd->bhd', w_s, k)
    return (C_new, n_new, m_new), h


def mlstm(q, k, v, ig, lf, C0, n0, m0):
    B, H, S, _ = q.shape
    L = math.gcd(S, MLSTM_CHUNK)
    nc = S // L

    def chunked(t):
        return jnp.moveaxis(t.reshape(t.shape[:2] + (nc, L) + t.shape[3:]), 2, 0)

    xs = (chunked(q), chunked(k), chunked(v), chunked(ig), chunked(lf))
    (C, n, m), h = lax.scan(mlstm_chunk, (C0, n0, m0), xs)
    h = jnp.moveaxis(h, 0, 2).reshape(B, H, S, DV)
    return h, C, n, m


def mix_layer(x, pos0, buf, C0, n0, m0, w_in, b_if, w_pool, pool_scale, hn_g, w_out):
    B, S, _ = x.shape
    z = x @ w_in
    u = z[..., :D_POOL]
    q = z[..., D_POOL:D_POOL + D_MLSTM]
    k = z[..., D_POOL + D_MLSTM:D_POOL + 2 * D_MLSTM]
    v = z[..., D_POOL + 2 * D_MLSTM:D_POOL + 3 * D_MLSTM]
    o = z[..., D_POOL + 3 * D_MLSTM:D_POOL + 4 * D_MLSTM]
    gates = (z[..., D_POOL + 4 * D_MLSTM:] + b_if).astype(jnp.float32)
    ig = jnp.transpose(gates[..., :MH], (0, 2, 1))
    lf = jnp.transpose(jax.nn.log_sigmoid(gates[..., MH:]), (0, 2, 1))

    def heads(t):
        return t.reshape(B, S, MH, DK).transpose(0, 2, 1, 3).astype(jnp.float32)

    h, C, n, m = mlstm(heads(q), heads(k) * DK ** -0.5, heads(v), ig, lf,
                       C0.astype(jnp.float32), n0.astype(jnp.float32), m0.astype(jnp.float32))
    mu = jnp.mean(h, axis=-1, keepdims=True)
    var = jnp.mean(jnp.square(h - mu), axis=-1, keepdims=True)
    hn = ((h - mu) * lax.rsqrt(var + HEAD_NORM_EPS)).transpose(0, 2, 1, 3).reshape(B, S, D_MLSTM)
    hn = hn * hn_g * jax.nn.sigmoid(o.astype(jnp.float32))
    y_pool, new_buf = pool_mixer(u, buf, pos0, w_pool, pool_scale)
    y = jnp.concatenate([y_pool, hn.astype(x.dtype)], axis=-1) @ w_out
    return y, new_buf, C.astype(x.dtype), n.astype(x.dtype), m.astype(x.dtype)


def gather_blocks(blocks, ix):
    return jax.vmap(jax.vmap(lambda bl, i: bl[i]))(blocks, ix)


def moba_attention(q, k, v, q_pos):
    B, Sq = q.shape[:2]
    T = k.shape[1]
    nb = -(-T // MOBA_BLOCK)
    pad = ((0, 0), (0, nb * MOBA_BLOCK - T), (0, 0), (0, 0))
    kb = jnp.pad(k.astype(jnp.float32), pad).reshape(B, nb, MOBA_BLOCK, N_KV_HEADS, HEAD_DIM).transpose(0, 3, 1, 2, 4)
    vb = jnp.pad(v.astype(jnp.float32), pad).reshape(B, nb, MOBA_BLOCK, N_KV_HEADS, HEAD_DIM).transpose(0, 3, 1, 2, 4)
    k_mean = jnp.mean(kb, axis=3)
    qg = q.astype(jnp.float32).reshape(B, Sq, N_KV_HEADS, GROUP, HEAD_DIM).transpose(0, 2, 3, 1, 4)
    q_blk = q_pos // MOBA_BLOCK
    gate = jnp.einsum('bhgsd,bhnd->bhgsn', qg, k_mean)
    past = jnp.arange(nb)[None, :] < q_blk[:, None]
    gate = jnp.where(past, gate, -jnp.inf)
    ksel = min(MOBA_TOPK, nb)
    _, idx = lax.top_k(gate, ksel)
    qb = math.gcd(Sq, Q_BLOCK)
    nc = Sq // qb
    scale = HEAD_DIM ** -0.5

    def chunked(t):
        return jnp.moveaxis(t.reshape(t.shape[:3] + (nc, qb) + t.shape[4:]), 3, 0)

    def attend(chunk):
        qc, ic, pc = chunk
        ks = gather_blocks(kb, ic)
        vs = gather_blocks(vb, ic)
        jc = pc // MOBA_BLOCK
        ko = kb[:, :, jc]
        vo = vb[:, :, jc]
        valid = jnp.arange(ksel)[None, :] < jc[:, None]
        s_sel = jnp.einsum('bhgtd,bhgtrnd->bhgtrn', qc, ks) * scale
        s_sel = jnp.where(valid[:, :, None], s_sel, -jnp.inf)
        key_pos = jc[:, None] * MOBA_BLOCK + jnp.arange(MOBA_BLOCK)[None, :]
        s_own = jnp.einsum('bhgtd,bhtnd->bhgtn', qc, ko) * scale
        s_own = jnp.where(key_pos <= pc[:, None], s_own, -jnp.inf)
        s = jnp.concatenate([s_sel.reshape(s_sel.shape[:4] + (ksel * MOBA_BLOCK,)), s_own], axis=-1)
        p = jax.nn.softmax(s, axis=-1)
        p_sel = p[..., :ksel * MOBA_BLOCK].reshape(s_sel.shape)
        p_own = p[..., ksel * MOBA_BLOCK:]
        return jnp.einsum('bhgtrn,bhgtrnd->bhgtd', p_sel, vs) + jnp.einsum('bhgtn,bhtnd->bhgtd', p_own, vo)

    out = lax.map(attend, (chunked(qg), chunked(idx), q_pos.reshape(nc, qb)))
    out = jnp.moveaxis(out, 0, 3).reshape(B, N_KV_HEADS, GROUP, Sq, HEAD_DIM)
    return out.transpose(0, 3, 1, 2, 4).reshape(B, Sq, D_Q).astype(q.dtype)


def moba_layer(x, pos0, past_k, past_v, w_qkv, w_o):
    B, S, _ = x.shape
    z = x @ w_qkv
    q = z[..., :D_Q].reshape(B, S, N_HEADS, HEAD_DIM)
    k = z[..., D_Q:D_Q + D_KV].reshape(B, S, N_KV_HEADS, HEAD_DIM)
    v = z[..., D_Q + D_KV:].reshape(B, S, N_KV_HEADS, HEAD_DIM)
    if past_k is None:
        k_all, v_all = k, v
    else:
        k_all = jnp.concatenate([past_k.astype(k.dtype), k], axis=1)
        v_all = jnp.concatenate([past_v.astype(v.dtype), v], axis=1)
    o = moba_attention(q, k_all, v_all, pos0 + jnp.arange(S))
    return o @ w_o, k, v


def trunk(x, pos0, pool_buf, C0, n0, m0, cache_k, cache_v, page_table, params):
    (w_in_mix, b_if, w_pool, pool_scale, mlstm_norm_g, w_out_mix, w_qkv, w_o,
     ln_mix_g, ln_mix_b, ln_ffn_g, ln_ffn_b, w_up, w_down) = params
    ks, vs, bufs, Cs, ns, ms = [], [], [], [], [], []
    for l in range(DEPTH):
        i = l // 2
        if l % 2 == 0:
            y, nbuf, C, n, m = mix_layer(x, pos0, pool_buf[i], C0[i], n0[i], m0[i], w_in_mix[i], b_if[i],
                                         w_pool[i], pool_scale[i], mlstm_norm_g[i], w_out_mix[i])
            bufs.append(nbuf); Cs.append(C); ns.append(n); ms.append(m)
        else:
            if page_table is None:
                past_k, past_v = None, None
            else:
                Bs = page_table.shape[0]
                past_k = cache_k[i][page_table].reshape(Bs, -1, N_KV_HEADS, HEAD_DIM)
                past_v = cache_v[i][page_table].reshape(Bs, -1, N_KV_HEADS, HEAD_DIM)
            y, k, v = moba_layer(x, pos0, past_k, past_v, w_qkv[i], w_o[i])
            ks.append(k); vs.append(v)
        x = layer_norm(ALPHA * x + y, ln_mix_g[l], ln_mix_b[l])
        x = layer_norm(ALPHA * x + sq_relu_mlp(x, w_up[l], w_down[l]), ln_ffn_g[l], ln_ffn_b[l])
    return (x, jnp.stack(ks), jnp.stack(vs), jnp.stack(bufs), jnp.stack(Cs), jnp.stack(ns), jnp.stack(ms))


def setup_inputs(seed: int = 0) -> dict:
    key = jax.random.key(seed)
    ks = jax.random.split(key, 32)
    n_pages = PAST_LEN // PAGE_SIZE
    n_pool = (5 * DEC_BATCH * n_pages) // 4
    nrm = jax.random.normal
    f32 = jnp.float32
    x_prompt = nrm(ks[0], (BATCH, SEQ, D_MODEL), f32)
    x_sample = nrm(ks[1], (DEC_BATCH, DEC_SEQ, D_MODEL), f32)
    cache_k = nrm(ks[2], (N_ATTN, n_pool, PAGE_SIZE, N_KV_HEADS, HEAD_DIM), f32)
    cache_v = nrm(ks[3], (N_ATTN, n_pool, PAGE_SIZE, N_KV_HEADS, HEAD_DIM), f32)
    state_pool = nrm(ks[4], (N_MIX, DEC_BATCH, POOL_BUF, D_POOL), f32)
    state_C = 0.3 * nrm(ks[5], (N_MIX, DEC_BATCH, MH, DK, DV), f32)
    state_n = 0.3 * nrm(ks[6], (N_MIX, DEC_BATCH, MH, DK), f32)
    state_m = 0.5 * nrm(ks[7], (N_MIX, DEC_BATCH, MH), f32)
    page_table = jax.random.permutation(ks[8], n_pool)[:DEC_BATCH * n_pages].reshape(DEC_BATCH, n_pages).astype(jnp.int32)
    w_in_mix = nrm(ks[9], (N_MIX, D_MODEL, D_IN_MIX), f32) * D_MODEL ** -0.5
    b_i = 0.1 * nrm(ks[10], (N_MIX, MH), f32)
    b_f = jnp.linspace(3.0, 6.0, MH, dtype=f32)[None, :] + 0.1 * nrm(ks[11], (N_MIX, MH), f32)
    b_if = jnp.concatenate([b_i, b_f], axis=-1)
    w_pool = nrm(ks[12], (N_MIX, POOL_GROUPS, POOL_GC, POOL_GC), f32) * POOL_GC ** -0.5
    pool_scale = 1.0 + 0.1 * nrm(ks[13], (N_MIX, D_POOL), f32)
    mlstm_norm_g = 1.0 + 0.1 * nrm(ks[14], (N_MIX, D_MLSTM), f32)
    w_out_mix = nrm(ks[15], (N_MIX, D_MIX, D_MODEL), f32) * (D_MIX ** -0.5 * BETA)
    w_qkv = nrm(ks[16], (N_ATTN, D_MODEL, D_Q + 2 * D_KV), f32) * D_MODEL ** -0.5
    w_o = nrm(ks[17], (N_ATTN, D_Q, D_MODEL), f32) * (D_Q ** -0.5 * BETA)
    ln_mix_g = 1.0 + 0.05 * nrm(ks[18], (DEPTH, D_MODEL), f32)
    ln_mix_b = 0.02 * nrm(ks[19], (DEPTH, D_MODEL), f32)
    ln_ffn_g = 1.0 + 0.05 * nrm(ks[20], (DEPTH, D_MODEL), f32)
    ln_ffn_b = 0.02 * nrm(ks[21], (DEPTH, D_MODEL), f32)
    w_up = nrm(ks[22], (DEPTH, D_MODEL, D_FF), f32) * D_MODEL ** -0.5
    w_down = nrm(ks[23], (DEPTH, D_FF, D_MODEL), f32) * (D_FF ** -0.5 * BETA)
    return {'x_prompt': x_prompt, 'x_sample': x_sample, 'cache_k': cache_k, 'cache_v': cache_v,
            'state_pool': state_pool, 'state_C': state_C, 'state_n': state_n, 'state_m': state_m,
            'page_table': page_table, 'w_in_mix': w_in_mix, 'b_if': b_if, 'w_pool': w_pool,
            'pool_scale': pool_scale, 'mlstm_norm_g': mlstm_norm_g, 'w_out_mix': w_out_mix,
            'w_qkv': w_qkv, 'w_o': w_o, 'ln_mix_g': ln_mix_g, 'ln_mix_b': ln_mix_b,
            'ln_ffn_g': ln_ffn_g, 'ln_ffn_b': ln_ffn_b, 'w_up': w_up, 'w_down': w_down}


def reference(x_prompt, x_sample, cache_k, cache_v, state_pool, state_C, state_n, state_m, page_table,
              w_in_mix, b_if, w_pool, pool_scale, mlstm_norm_g, w_out_mix, w_qkv, w_o,
              ln_mix_g, ln_mix_b, ln_ffn_g, ln_ffn_b, w_up, w_down):
    params = (w_in_mix, b_if, w_pool, pool_scale, mlstm_norm_g, w_out_mix, w_qkv, w_o,
              ln_mix_g, ln_mix_b, ln_ffn_g, ln_ffn_b, w_up, w_down)
    dt = x_prompt.dtype
    Bp = x_prompt.shape[0]
    pool0 = jnp.zeros((N_MIX, Bp, POOL_BUF, D_POOL), dt)
    C0 = jnp.zeros((N_MIX, Bp, MH, DK, DV), dt)
    n0 = jnp.zeros((N_MIX, Bp, MH, DK), dt)
    m0 = jnp.zeros((N_MIX, Bp, MH), dt)
    y_prompt, k_prompt, v_prompt, pool_prompt, C_prompt, n_prompt, m_prompt = trunk(
        x_prompt, 0, pool0, C0, n0, m0, None, None, None, params)
    past_len = page_table.shape[1] * PAGE_SIZE
    y_sample, k_sample, v_sample, pool_sample, C_sample, n_sample, m_sample = trunk(
        x_sample, past_len, state_pool, state_C, state_n, state_m, cache_k, cache_v, page_table, params)
    return (y_prompt, y_sample, k_prompt, v_prompt, k_sample, v_sample,
            pool_prompt, C_prompt, n_prompt, m_prompt, pool_sample, C_sample, n_sample, m_sample)
```

```python
import functools
import math

import jax
import jax.numpy as jnp
from jax import lax
from jax.experimental import pallas as pl
from jax.experimental.pallas import tpu as pltpu

F32 = jnp.float32
BF16 = jnp.bfloat16

POOL_WINDOWS = (2, 4, 8, 16)
POOL_BUF = max(POOL_WINDOWS) - 1
POOL_HALO = 16
MH = 4
N_HEADS = 16
N_KV_HEADS = 4
GROUP = N_HEADS // N_KV_HEADS
MOBA_BLOCK = 256
MOBA_TOPK = 3
PAGE_SIZE = 128
LN_EPS = 1e-5
HEAD_NORM_EPS = 1e-6
LANES = 128
SAMPLE_ROWS = 16

VMEM_LIMIT = 56 * 1024 * 1024
MLSTM_CHUNK = 256


def _params(*sem):
    return pltpu.CompilerParams(dimension_semantics=sem, vmem_limit_bytes=VMEM_LIMIT)


def _layer_norm(y, g, b):
    mu = jnp.mean(y, axis=-1, keepdims=True)
    yc = y - mu
    var = jnp.mean(yc * yc, axis=-1, keepdims=True)
    return yc * lax.rsqrt(var + LN_EPS) * g + b


def _log_sigmoid(x):
    return jnp.minimum(x, 0.0) - jnp.log1p(jnp.exp(-jnp.abs(x)))


def _proj_body(x_ref, w_ref, o_ref, xb_ref):
    @pl.when(pl.program_id(1) == 0)
    def _():
        xb_ref[...] = x_ref[...].astype(BF16)

    o_ref[...] = jnp.dot(xb_ref[...], w_ref[...], preferred_element_type=F32)


def _proj_gates_body(x_ref, w_ref, wg_ref, o_ref, g_ref, xb_ref):
    @pl.when(pl.program_id(1) == 0)
    def _():
        xb = x_ref[...].astype(BF16)
        xb_ref[...] = xb
        g_ref[...] = jnp.dot(xb, wg_ref[...], preferred_element_type=F32)

    o_ref[...] = jnp.dot(xb_ref[...], w_ref[...], preferred_element_type=F32)


def _proj(x, w, wg=None, *, tm, tn):
    M, K = x.shape
    N = w.shape[1]
    grid = (M // tm, N // tn)
    x_spec = pl.BlockSpec((tm, K), lambda i, j: (i, 0))
    w_spec = pl.BlockSpec((K, tn), lambda i, j: (0, j))
    o_spec = pl.BlockSpec((tm, tn), lambda i, j: (i, j))
    scratch = [pltpu.VMEM((tm, K), BF16)]
    if wg is None:
        return pl.pallas_call(
            _proj_body, grid=grid, in_specs=[x_spec, w_spec], out_specs=o_spec,
            out_shape=jax.ShapeDtypeStruct((M, N), F32), scratch_shapes=scratch,
            compiler_params=_params("parallel", "arbitrary"), name="proj")(x, w)
    g_spec = pl.BlockSpec((tm, LANES), lambda i, j: (i, 0))
    return pl.pallas_call(
        _proj_gates_body, grid=grid,
        in_specs=[x_spec, w_spec, pl.BlockSpec((K, LANES), lambda i, j: (0, 0))],
        out_specs=[o_spec, g_spec],
        out_shape=[jax.ShapeDtypeStruct((M, N), F32), jax.ShapeDtypeStruct((M, LANES), F32)],
        scratch_shapes=scratch,
        compiler_params=_params("parallel", "arbitrary"), name="proj_gates")(x, w, wg)


def _proj_ln_body(*refs, n_a, alpha):
    a_refs = refs[:n_a]
    w_ref, res_ref, g_ref, b_ref, o_ref = refs[n_a:]
    y = alpha * res_ref[...]
    k0 = 0
    for a_ref in a_refs:
        ka = a_ref.shape[1]
        y = y + jnp.dot(a_ref[...].astype(BF16), w_ref[k0:k0 + ka, :], preferred_element_type=F32)
        k0 += ka
    o_ref[...] = _layer_norm(y, g_ref[...], b_ref[...])


def _proj_ln(a_list, w, res, g, b, *, alpha, tm):
    M, D = res.shape
    K = w.shape[0]
    in_specs = [pl.BlockSpec((tm, a.shape[1]), lambda i: (i, 0)) for a in a_list]
    in_specs += [pl.BlockSpec((K, D), lambda i: (0, 0)),
                 pl.BlockSpec((tm, D), lambda i: (i, 0)),
                 pl.BlockSpec((1, D), lambda i: (0, 0)),
                 pl.BlockSpec((1, D), lambda i: (0, 0))]
    return pl.pallas_call(
        functools.partial(_proj_ln_body, n_a=len(a_list), alpha=alpha),
        grid=(M // tm,), in_specs=in_specs,
        out_specs=pl.BlockSpec((tm, D), lambda i: (i, 0)),
        out_shape=jax.ShapeDtypeStruct((M, D), F32),
        compiler_params=_params("parallel"), name="proj_ln")(*a_list, w, res, g, b)


def _ffn_body(x_ref, wu_ref, wd_ref, g_ref, b_ref, o_ref, xb_ref, acc_ref, *, alpha):
    f = pl.program_id(1)

    @pl.when(f == 0)
    def _():
        xb_ref[...] = x_ref[...].astype(BF16)
        acc_ref[...] = jnp.zeros_like(acc_ref)

    h = jnp.dot(xb_ref[...], wu_ref[...], preferred_element_type=F32)
    h = jnp.square(jnp.maximum(h, 0.0)).astype(BF16)
    acc_ref[...] += jnp.dot(h, wd_ref[...], preferred_element_type=F32)

    @pl.when(f == pl.num_programs(1) - 1)
    def _():
        y = alpha * x_ref[...] + acc_ref[...]
        o_ref[...] = _layer_norm(y, g_ref[...], b_ref[...])


def _ffn(x, wu, wd, g, b, *, alpha, tm, tf):
    M, D = x.shape
    Fd = wu.shape[1]
    return pl.pallas_call(
        functools.partial(_ffn_body, alpha=alpha),
        grid=(M // tm, Fd // tf),
        in_specs=[pl.BlockSpec((tm, D), lambda i, f: (i, 0)),
                  pl.BlockSpec((D, tf), lambda i, f: (0, f)),
                  pl.BlockSpec((tf, D), lambda i, f: (f, 0)),
                  pl.BlockSpec((1, D), lambda i, f: (0, 0)),
                  pl.BlockSpec((1, D), lambda i, f: (0, 0))],
        out_specs=pl.BlockSpec((tm, D), lambda i, f: (i, 0)),
        out_shape=jax.ShapeDtypeStruct((M, D), F32),
        scratch_shapes=[pltpu.VMEM((tm, D), BF16), pltpu.VMEM((tm, D), F32)],
        compiler_params=_params("parallel", "arbitrary"), name="ffn")(x, wu, wd, g, b)


def _pool_prompt_body(u_ref, wp_ref, sc_ref, y_ref, ext_ref, *, T, gc):
    t = pl.program_id(1)

    @pl.when(t == 0)
    def _():
        ext_ref[0:POOL_HALO, :] = jnp.zeros((POOL_HALO, ext_ref.shape[1]), F32)

    ext_ref[POOL_HALO:POOL_HALO + T, :] = u_ref[...]
    pos = t * T + lax.broadcasted_iota(jnp.int32, (T, 1), 0)
    for gi, w in enumerate(POOL_WINDOWS):
        c0, c1 = gi * gc, (gi + 1) * gc
        tok = ext_ref[POOL_HALO:POOL_HALO + T, c0:c1]
        acc = tok
        for k in range(1, w):
            acc = acc + ext_ref[POOL_HALO - k:POOL_HALO - k + T, c0:c1]
        cnt = jnp.minimum(w, pos + 1).astype(F32)
        d = acc / cnt - tok
        y = jnp.dot(d.astype(BF16), wp_ref[gi], preferred_element_type=F32) * sc_ref[:, c0:c1]
        y_ref[:, c0:c1] = y.astype(y_ref.dtype)
    ext_ref[0:POOL_HALO, :] = ext_ref[T:T + POOL_HALO, :]


def _pool_prompt(z, wp, sc, *, B, S, d_pool, T):
    nt = S // T
    gc = d_pool // len(POOL_WINDOWS)
    return pl.pallas_call(
        functools.partial(_pool_prompt_body, T=T, gc=gc),
        grid=(B, nt),
        in_specs=[pl.BlockSpec((T, d_pool), lambda b, t: (b * nt + t, 0)),
                  pl.BlockSpec(wp.shape, lambda b, t: (0, 0, 0)),
                  pl.BlockSpec((1, d_pool), lambda b, t: (0, 0))],
        out_specs=pl.BlockSpec((T, d_pool), lambda b, t: (b * nt + t, 0)),
        out_shape=jax.ShapeDtypeStruct((B * S, d_pool), BF16),
        scratch_shapes=[pltpu.VMEM((POOL_HALO + T, d_pool), F32)],
        compiler_params=_params("parallel", "arbitrary"), name="pool_prompt")(z, wp, sc)


def _pool_step_body(buf_ref, u_ref, wp_ref, sc_ref, y_ref, *, gc, pos0):
    u = u_ref[...]
    for gi, w in enumerate(POOL_WINDOWS):
        c0, c1 = gi * gc, (gi + 1) * gc
        tok = u[:, c0:c1]
        acc = tok
        for k in range(1, w):
            acc = acc + buf_ref[POOL_BUF - k, :, c0:c1]
        d = acc / float(min(w, pos0 + 1)) - tok
        y = jnp.dot(d.astype(BF16), wp_ref[gi], preferred_element_type=F32) * sc_ref[:, c0:c1]
        y_ref[:, c0:c1] = y.astype(y_ref.dtype)


def _pool_step(buf_t, u, wp, sc, *, pos0):
    Bs, d_pool = u.shape
    gc = d_pool // len(POOL_WINDOWS)
    return pl.pallas_call(
        functools.partial(_pool_step_body, gc=gc, pos0=pos0),
        out_shape=jax.ShapeDtypeStruct((Bs, d_pool), BF16),
        compiler_params=pltpu.CompilerParams(vmem_limit_bytes=VMEM_LIMIT),
        name="pool_step")(buf_t, u, wp, sc)


def _head_norm_gate(h, o, g):
    mu = jnp.mean(h, axis=-1, keepdims=True)
    hc = h - mu
    var = jnp.mean(hc * hc, axis=-1, keepdims=True)
    return hc * lax.rsqrt(var + HEAD_NORM_EPS) * g * jax.nn.sigmoid(o)


def _mlstm_prompt_body(q_ref, k_ref, v_ref, o_ref, gt_ref, bias_ref, hg_ref,
                       hn_ref, C_ref, n_ref, m_ref, *, L, dk):
    c = pl.program_id(1)

    @pl.when(c == 0)
    def _():
        C_ref[...] = jnp.zeros_like(C_ref)
        n_ref[...] = jnp.zeros_like(n_ref)
        m_ref[...] = jnp.zeros_like(m_ref)

    gt = gt_ref[...] + bias_ref[...]
    gT = gt.T
    row = lax.broadcasted_iota(jnp.int32, (L, L), 0)
    col = lax.broadcasted_iota(jnp.int32, (L, L), 1)
    causal = col <= row
    neg_inf = jnp.float32(-jnp.inf)
    k_scale = dk ** -0.5
    for h in range(MH):
        c0, c1 = h * dk, (h + 1) * dk
        ig_col = gt[:, h:h + 1]
        lf_col = _log_sigmoid(gt[:, MH + h:MH + h + 1])
        ig_row = gT[h:h + 1, :]
        lf_row = _log_sigmoid(gT[MH + h:MH + h + 1, :])
        b_col = jnp.sum(jnp.where(causal, lf_row, 0.0), axis=1, keepdims=True)
        b_row = jnp.sum(jnp.where(row <= col, lf_col, 0.0), axis=0, keepdims=True)
        a_row = ig_row - b_row
        a_col = ig_col - b_col
        cmax = jnp.max(jnp.where(causal, a_row, neg_inf), axis=1, keepdims=True)
        m_prev = m_ref[0, h, :, 0:1]
        m_t = b_col + jnp.maximum(m_prev, cmax)
        w_inter = jnp.exp(b_col + m_prev - m_t)
        dmat = jnp.exp(jnp.where(causal, a_row + b_col - m_t, neg_inf))
        q = q_ref[:, c0:c1]
        ks = k_ref[:, c0:c1] * k_scale
        qb = q.astype(BF16)
        kb = ks.astype(BF16)
        vb = v_ref[:, c0:c1].astype(BF16)
        s = lax.dot_general(qb, kb, (((1,), (1,)), ((), ())), preferred_element_type=F32) * dmat
        C_prev = C_ref[0, h]
        n_prev = n_ref[0, h]
        num = (w_inter * jnp.dot(qb, C_prev.astype(BF16), preferred_element_type=F32)
               + jnp.dot(s.astype(BF16), vb, preferred_element_type=F32))
        den = (w_inter * jnp.sum(q * n_prev, axis=1, keepdims=True)
               + jnp.sum(s, axis=1, keepdims=True))
        hh = num / jnp.maximum(jnp.abs(den), jnp.exp(-m_t))
        hn_ref[:, c0:c1] = _head_norm_gate(hh, o_ref[:, c0:c1], hg_ref[:, c0:c1]).astype(hn_ref.dtype)

        m_new = m_t[L - 1:L, :]
        b_last = b_col[L - 1:L, :]
        w_old = jnp.exp(b_last + m_prev - m_new)
        kw = ks * jnp.exp(a_col + b_last - m_new)
        C_ref[0, h] = w_old * C_prev + lax.dot_general(
            kw.astype(BF16), vb, (((0,), (0,)), ((), ())), preferred_element_type=F32)
        n_ref[0, h] = w_old * n_prev + jnp.sum(kw, axis=0, keepdims=True)
        m_ref[0, h] = jnp.broadcast_to(m_new, (1, LANES))


def _mlstm_prompt(z, gates, bias, hg, *, B, S, d_pool, d_ml, L):
    nc = S // L
    dk = d_ml // MH
    blk0 = d_pool // d_ml
    assert blk0 * d_ml == d_pool

    def zspec(part):
        return pl.BlockSpec((L, d_ml), lambda b, c: (b * nc + c, blk0 + part))

    return pl.pallas_call(
        functools.partial(_mlstm_prompt_body, L=L, dk=dk),
        grid=(B, nc),
        in_specs=[zspec(0), zspec(1), zspec(2), zspec(3),
                  pl.BlockSpec((L, LANES), lambda b, c: (b * nc + c, 0)),
                  pl.BlockSpec((1, LANES), lambda b, c: (0, 0)),
                  pl.BlockSpec((1, d_ml), lambda b, c: (0, 0))],
        out_specs=[pl.BlockSpec((L, d_ml), lambda b, c: (b * nc + c, 0)),
                   pl.BlockSpec((1, MH, dk, dk), lambda b, c: (b, 0, 0, 0)),
                   pl.BlockSpec((1, MH, 1, dk), lambda b, c: (b, 0, 0, 0)),
                   pl.BlockSpec((1, MH, 1, LANES), lambda b, c: (b, 0, 0, 0))],
        out_shape=[jax.ShapeDtypeStruct((B * S, d_ml), BF16),
                   jax.ShapeDtypeStruct((B, MH, dk, dk), F32),
                   jax.ShapeDtypeStruct((B, MH, 1, dk), F32),
                   jax.ShapeDtypeStruct((B, MH, 1, LANES), F32)],
        compiler_params=_params("parallel", "arbitrary"), name="mlstm_prompt")(
            z, z, z, z, gates, bias, hg)


def _row_to_col(x_row, n):
    r = lax.broadcasted_iota(jnp.int32, (n, n), 0)
    c = lax.broadcasted_iota(jnp.int32, (n, n), 1)
    return jnp.sum(jnp.where(r == c, x_row, 0.0), axis=1, keepdims=True)


def _mlstm_step_body(z_ref, gt_ref, bias_ref, hg_ref, C0_ref, n0_ref, m0_ref,
                     hn_ref, C_ref, n_ref, m_ref, *, d_pool, d_ml, dk):
    gt = gt_ref[0] + bias_ref[...]
    k_scale = dk ** -0.5
    for h in range(MH):
        base = d_pool + h * dk
        q = z_ref[0, :, base:base + dk]
        ks = z_ref[0, :, base + d_ml:base + d_ml + dk] * k_scale
        v = z_ref[0, :, base + 2 * d_ml:base + 2 * d_ml + dk]
        o = z_ref[0, :, base + 3 * d_ml:base + 3 * d_ml + dk]
        ig = gt[:, h:h + 1]
        lf = _log_sigmoid(gt[:, MH + h:MH + h + 1])
        m_prev = m0_ref[0, h]
        m_new = jnp.maximum(m_prev + lf, ig)
        w_old = jnp.exp(lf + m_prev - m_new)
        w_s = jnp.exp(ig - m_new)
        C_new = w_old * C0_ref[0, h] + _row_to_col(ks * w_s, dk) * v
        n_new = w_old * n0_ref[0, h] + w_s * ks
        num = jnp.sum(_row_to_col(q, dk) * C_new, axis=0, keepdims=True)
        den = jnp.sum(q * n_new, axis=1, keepdims=True)
        hh = num / jnp.maximum(jnp.abs(den), jnp.exp(-m_new))
        c0 = h * dk
        hn_ref[0, :, c0:c0 + dk] = _head_norm_gate(hh, o, hg_ref[:, c0:c0 + dk]).astype(hn_ref.dtype)
        C_ref[0, h] = C_new
        n_ref[0, h] = n_new
        m_ref[0, h] = m_new


def _mlstm_step(z3, g3, bias, hg, C0, n0, m0, *, d_pool, d_ml):
    Bs = C0.shape[0]
    dk = d_ml // MH
    dz = z3.shape[-1]
    st = lambda *shape: pl.BlockSpec((1,) + shape, lambda b: (b,) + (0,) * len(shape))
    return pl.pallas_call(
        functools.partial(_mlstm_step_body, d_pool=d_pool, d_ml=d_ml, dk=dk),
        grid=(Bs,),
        in_specs=[st(1, dz), st(1, LANES),
                  pl.BlockSpec((1, LANES), lambda b: (0, 0)),
                  pl.BlockSpec((1, d_ml), lambda b: (0, 0)),
                  st(MH, dk, dk), st(MH, 1, dk), st(MH, 1, 1)],
        out_specs=[st(1, d_ml), st(MH, dk, dk), st(MH, 1, dk), st(MH, 1, 1)],
        out_shape=[jax.ShapeDtypeStruct((Bs, 1, d_ml), BF16),
                   jax.ShapeDtypeStruct((Bs, MH, dk, dk), F32),
                   jax.ShapeDtypeStruct((Bs, MH, 1, dk), F32),
                   jax.ShapeDtypeStruct((Bs, MH, 1, 1), F32)],
        compiler_params=_params("parallel"), name="mlstm_step")(z3, g3, bias, hg, C0, n0, m0)


def _moba_prompt_body(q_ref, k_ref, v_ref, o_ref, m_ref, l_ref, acc_ref, *, nblk, hd):
    j = pl.program_id(2)
    blk = MOBA_BLOCK
    R = GROUP * blk
    scale = hd ** -0.5
    neg_inf = jnp.float32(-jnp.inf)
    qb = jnp.concatenate([q_ref[:, g * hd:(g + 1) * hd] for g in range(GROUP)], axis=0).astype(BF16)

    def scores(kb):
        return lax.dot_general(qb, kb, (((1,), (1,)), ((), ())), preferred_element_type=F32)

    kmean = jnp.concatenate(
        [jnp.sum(k_ref[n * blk:(n + 1) * blk, :], axis=0, keepdims=True) for n in range(nblk)],
        axis=0) * (1.0 / blk)
    gate = scores(kmean.astype(BF16))
    lane = lax.broadcasted_iota(jnp.int32, (R, nblk), 1)
    past = lane < j
    sel = jnp.zeros((R, nblk), F32)
    for n in range(nblk - 1):
        gn = gate[:, n:n + 1]
        beats = ((gate > gn) | ((gate == gn) & (lane < n))) & past
        cnt = jnp.sum(beats.astype(F32), axis=1, keepdims=True)
        sel = jnp.where(lane == n, (cnt < MOBA_TOPK).astype(F32), sel)

    start = pl.multiple_of(j * blk, blk)
    s = scores(k_ref[pl.ds(start, blk), :].astype(BF16)) * scale
    qpos = lax.broadcasted_iota(jnp.int32, (R, blk), 0) & (blk - 1)
    kpos = lax.broadcasted_iota(jnp.int32, (R, blk), 1)
    s = jnp.where(kpos <= qpos, s, neg_inf)
    m0 = jnp.max(s, axis=1, keepdims=True)
    p = jnp.exp(s - m0)
    m_ref[...] = m0
    l_ref[...] = jnp.sum(p, axis=1, keepdims=True)
    acc_ref[...] = jnp.dot(p.astype(BF16), v_ref[pl.ds(start, blk), :].astype(BF16),
                           preferred_element_type=F32)

    for n in range(nblk - 1):
        @pl.when(n < j)
        def _(n=n):
            sn = scores(k_ref[n * blk:(n + 1) * blk, :].astype(BF16)) * scale
            sn = jnp.where(sel[:, n:n + 1] > 0.5, sn, neg_inf)
            m_old = m_ref[...]
            m_new = jnp.maximum(m_old, jnp.max(sn, axis=1, keepdims=True))
            alpha = jnp.exp(m_old - m_new)
            pn = jnp.exp(sn - m_new)
            l_ref[...] = alpha * l_ref[...] + jnp.sum(pn, axis=1, keepdims=True)
            acc_ref[...] = alpha * acc_ref[...] + jnp.dot(
                pn.astype(BF16), v_ref[n * blk:(n + 1) * blk, :].astype(BF16),
                preferred_element_type=F32)
            m_ref[...] = m_new

    out = acc_ref[...] / l_ref[...]
    for g in range(GROUP):
        o_ref[:, g * hd:(g + 1) * hd] = out[g * blk:(g + 1) * blk, :].astype(o_ref.dtype)


def _moba_prompt(z, *, B, S, hd):
    assert S % MOBA_BLOCK == 0
    nq = S // MOBA_BLOCK
    gw = GROUP * hd
    d_q = N_HEADS * hd
    kblk0 = d_q // hd
    vblk0 = kblk0 + N_KV_HEADS
    return pl.pallas_call(
        functools.partial(_moba_prompt_body, nblk=nq, hd=hd),
        grid=(B, N_KV_HEADS, nq),
        in_specs=[pl.BlockSpec((MOBA_BLOCK, gw), lambda b, h, j: (b * nq + j, h)),
                  pl.BlockSpec((S, hd), lambda b, h, j: (b, kblk0 + h)),
                  pl.BlockSpec((S, hd), lambda b, h, j: (b, vblk0 + h))],
        out_specs=pl.BlockSpec((MOBA_BLOCK, gw), lambda b, h, j: (b * nq + j, h)),
        out_shape=jax.ShapeDtypeStruct((B * S, d_q), BF16),
        scratch_shapes=[pltpu.VMEM((GROUP * MOBA_BLOCK, 1), F32),
                        pltpu.VMEM((GROUP * MOBA_BLOCK, 1), F32),
                        pltpu.VMEM((GROUP * MOBA_BLOCK, hd), F32)],
        compiler_params=_params("parallel", "parallel", "arbitrary"), name="moba_prompt")(z, z, z)


PAGES_PER_STEP = 16
PAGES_PER_BLOCK = MOBA_BLOCK // PAGE_SIZE


def _kmean_body(pt_ref, *refs):
    page_refs, o_ref = refs[:PAGES_PER_STEP], refs[PAGES_PER_STEP]
    for blk in range(PAGES_PER_STEP // PAGES_PER_BLOCK):
        acc = None
        for r in range(PAGES_PER_BLOCK):
            part = jnp.sum(page_refs[blk * PAGES_PER_BLOCK + r][0, 0], axis=0, keepdims=True)
            acc = part if acc is None else acc + part
        o_ref[0, 0, blk:blk + 1, :] = acc * (1.0 / MOBA_BLOCK)


def _kmean_paged(cache_k4, pt_flat, *, Bs, n_pages):
    n_attn, _, page, dkv = cache_k4.shape
    ng = n_pages // PAGES_PER_STEP
    bps = PAGES_PER_STEP // PAGES_PER_BLOCK

    def page_spec(r):
        return pl.BlockSpec(
            (1, 1, page, dkv),
            lambda l, b, g, pt: (l, pt[b * n_pages + g * PAGES_PER_STEP + r], 0, 0))

    return pl.pallas_call(
        _kmean_body,
        grid_spec=pltpu.PrefetchScalarGridSpec(
            num_scalar_prefetch=1, grid=(n_attn, Bs, ng),
            in_specs=[page_spec(r) for r in range(PAGES_PER_STEP)],
            out_specs=pl.BlockSpec((1, 1, bps, dkv), lambda l, b, g, pt: (l, b, g, 0))),
        out_shape=jax.ShapeDtypeStruct((n_attn, Bs, ng * bps, dkv), F32),
        compiler_params=_params("parallel", "parallel", "arbitrary"), name="kmean_paged")(
            pt_flat, *([cache_k4] * PAGES_PER_STEP))


def _topk_body(q_ref, km_ref, idx_ref, *, hd, nblk):
    qb = q_ref[0].astype(BF16)
    head = lax.broadcasted_iota(jnp.int32, (N_HEADS, nblk), 0)
    lane = lax.broadcasted_iota(jnp.int32, (N_HEADS, nblk), 1)
    gate = jnp.zeros((N_HEADS, nblk), F32)
    for h in range(N_KV_HEADS):
        km = km_ref[0, :, h * hd:(h + 1) * hd].astype(BF16)
        gh = lax.dot_general(qb, km, (((1,), (1,)), ((), ())), preferred_element_type=F32)
        gate = jnp.where(head // GROUP == h, gh, gate)
    out_lane = lax.broadcasted_iota(jnp.int32, (N_HEADS, LANES), 1)
    out = jnp.zeros((N_HEADS, LANES), jnp.int32)
    for r in range(MOBA_TOPK):
        mx = jnp.max(gate, axis=1, keepdims=True)
        idx = jnp.min(jnp.where(gate == mx, lane, nblk), axis=1, keepdims=True)
        out = jnp.where(out_lane == r, idx, out)
        gate = jnp.where(lane == idx, -jnp.inf, gate)
    idx_ref[0] = out


def _topk_blocks(q3, kmean_l):
    Bs, _, hd = q3.shape
    nblk = kmean_l.shape[1]
    assert nblk >= MOBA_TOPK
    return pl.pallas_call(
        functools.partial(_topk_body, hd=hd, nblk=nblk),
        grid=(Bs,),
        in_specs=[pl.BlockSpec((1, N_HEADS, hd), lambda b: (b, 0, 0)),
                  pl.BlockSpec((1,) + kmean_l.shape[1:], lambda b: (b, 0, 0))],
        out_specs=pl.BlockSpec((1, N_HEADS, LANES), lambda b: (b, 0, 0)),
        out_shape=jax.ShapeDtypeStruct((Bs, N_HEADS, LANES), jnp.int32),
        compiler_params=_params("parallel"), name="topk_blocks")(q3, kmean_l)


N_SEL_PAGES = MOBA_TOPK * PAGES_PER_BLOCK


def _moba_step_body(idx_ref, pt_ref, q_ref, kn_ref, vn_ref, *refs, hd):
    k_refs = refs[:N_SEL_PAGES]
    v_refs = refs[N_SEL_PAGES:2 * N_SEL_PAGES]
    o_ref = refs[2 * N_SEL_PAGES]
    scale = hd ** -0.5
    rows = 8
    q = q_ref[0, 0]
    qb = jnp.broadcast_to(q, (rows, hd)).astype(BF16)
    s_pages = [lax.dot_general(qb, kr[0, 0].astype(BF16), (((1,), (1,)), ((), ())),
                               preferred_element_type=F32) * scale for kr in k_refs]
    s_own = jnp.sum(qb.astype(F32) * kn_ref[0, 0].astype(BF16).astype(F32), axis=1, keepdims=True) * scale
    m = s_own
    for s in s_pages:
        m = jnp.maximum(m, jnp.max(s, axis=1, keepdims=True))
    p_own = jnp.exp(s_own - m)
    l = p_own
    acc = p_own * vn_ref[0, 0].astype(BF16).astype(F32)
    for s, vr in zip(s_pages, v_refs):
        p = jnp.exp(s - m)
        l = l + jnp.sum(p, axis=1, keepdims=True)
        acc = acc + jnp.dot(p.astype(BF16), vr[0, 0].astype(BF16), preferred_element_type=F32)
    o_ref[0, 0] = (acc / l)[0:1, :].astype(o_ref.dtype)


def _moba_step(q4, kn4, vn4, cache_k4, cache_v4, idx_flat, pt_flat, *, layer, n_pages):
    Bs, _, _, hd = q4.shape
    page = cache_k4.shape[2]

    def page_spec(r):
        sel, half = divmod(r, PAGES_PER_BLOCK)

        def index_map(b, hq, idx, pt):
            blk = idx[(b * N_HEADS + hq) * MOBA_TOPK + sel]
            return (layer, pt[b * n_pages + blk * PAGES_PER_BLOCK + half], 0, hq // GROUP)

        return pl.BlockSpec((1, 1, page, hd), index_map)

    head_spec = pl.BlockSpec((1, 1, 1, hd), lambda b, hq, idx, pt: (b, hq, 0, 0))
    kv_spec = pl.BlockSpec((1, 1, 1, hd), lambda b, hq, idx, pt: (b, hq // GROUP, 0, 0))
    return pl.pallas_call(
        functools.partial(_moba_step_body, hd=hd),
        grid_spec=pltpu.PrefetchScalarGridSpec(
            num_scalar_prefetch=2, grid=(Bs, N_HEADS),
            in_specs=[head_spec, kv_spec, kv_spec]
            + [page_spec(r) for r in range(N_SEL_PAGES)] * 2,
            out_specs=head_spec),
        out_shape=jax.ShapeDtypeStruct((Bs, N_HEADS, 1, hd), BF16),
        compiler_params=_params("parallel", "arbitrary"), name="moba_step")(
            idx_flat, pt_flat, q4, kn4, vn4,
            *([cache_k4] * N_SEL_PAGES), *([cache_v4] * N_SEL_PAGES))


def _pad_rows(a, rows):
    return jnp.pad(a, ((0, rows - a.shape[0]),) + ((0, 0),) * (a.ndim - 1))


def kernel(x_prompt, x_sample, cache_k, cache_v, state_pool, state_C, state_n, state_m, page_table,
           w_in_mix, b_if, w_pool, pool_scale, mlstm_norm_g, w_out_mix, w_qkv, w_o,
           ln_mix_g, ln_mix_b, ln_ffn_g, ln_ffn_b, w_up, w_down):
    B, S, D = x_prompt.shape
    Bs = x_sample.shape[0]
    assert x_sample.shape[1] == 1 and Bs <= SAMPLE_ROWS
    depth = w_up.shape[0]
    n_mix, n_attn = w_in_mix.shape[0], w_qkv.shape[0]
    d_pool = state_pool.shape[-1]
    dk = state_C.shape[-1]
    d_ml = MH * dk
    d_main = d_pool + 4 * d_ml
    assert w_in_mix.shape[2] == d_main + 2 * MH
    hd = cache_k.shape[-1]
    d_q, d_kv = N_HEADS * hd, N_KV_HEADS * hd
    n_pages = page_table.shape[1]
    past_len = n_pages * PAGE_SIZE
    assert cache_k.shape[2] == PAGE_SIZE and past_len % MOBA_BLOCK == 0
    alpha = (2 * depth) ** 0.25
    M = B * S
    L = math.gcd(S, MLSTM_CHUNK)

    w_in_b = w_in_mix[:, :, :d_main].astype(BF16)
    w_g_b = jnp.pad(w_in_mix[:, :, d_main:], ((0, 0), (0, 0), (0, LANES - 2 * MH))).astype(BF16)
    bias_g = jnp.pad(b_if, ((0, 0), (0, LANES - 2 * MH)))[:, None, :]
    w_pool_b = w_pool.astype(BF16)
    w_out_b = w_out_mix.astype(BF16)
    w_qkv_b = w_qkv.astype(BF16)
    w_o_b = w_o.astype(BF16)
    w_up_b = w_up.astype(BF16)
    w_down_b = w_down.astype(BF16)

    cache_k4 = cache_k.reshape(cache_k.shape[:3] + (d_kv,))
    cache_v4 = cache_v.reshape(cache_v.shape[:3] + (d_kv,))
    pt_flat = page_table.reshape(-1)
    kmean_past = _kmean_paged(cache_k4, pt_flat, Bs=Bs, n_pages=n_pages)

    xp = x_prompt.reshape(M, D)
    xs = _pad_rows(x_sample.reshape(Bs, D), SAMPLE_ROWS)

    kp, vp, ks_, vs_ = [], [], [], []
    pool_p, C_p, n_p, m_p = [], [], [], []
    pool_s, C_s, n_s, m_s = [], [], [], []
    for l in range(depth):
        i = l // 2
        ln_g, ln_b = ln_mix_g[l][None, :], ln_mix_b[l][None, :]
        if l % 2 == 0:
            sc = pool_scale[i][None, :]
            hg = mlstm_norm_g[i][None, :]
            z, gates = _proj(xp, w_in_b[i], w_g_b[i], tm=1024, tn=1024)
            y_pool = _pool_prompt(z, w_pool_b[i], sc, B=B, S=S, d_pool=d_pool, T=256)
            hn, C, n, m = _mlstm_prompt(z, gates, bias_g[i], hg, B=B, S=S, d_pool=d_pool, d_ml=d_ml, L=L)
            xp = _proj_ln([y_pool, hn], w_out_b[i], xp, ln_g, ln_b, alpha=alpha, tm=512)
            pool_p.append(z.reshape(B, S, d_main)[:, S - POOL_BUF:, :d_pool])
            C_p.append(C)
            n_p.append(n.reshape(B, MH, dk))
            m_p.append(m[:, :, 0, 0])
            zs, gs = _proj(xs, w_in_b[i], w_g_b[i], tm=SAMPLE_ROWS, tn=1024)
            us = zs[:Bs, :d_pool]
            ys_pool = _pool_step(jnp.transpose(state_pool[i], (1, 0, 2)), us, w_pool_b[i], sc, pos0=past_len)
            hns, C, n, m = _mlstm_step(
                zs[:Bs, None, :], gs[:Bs, None, :], bias_g[i], hg, state_C[i],
                state_n[i][:, :, None, :], state_m[i][:, :, None, None], d_pool=d_pool, d_ml=d_ml)
            xs = _proj_ln([_pad_rows(ys_pool, SAMPLE_ROWS), _pad_rows(hns[:, 0, :], SAMPLE_ROWS)],
                          w_out_b[i], xs, ln_g, ln_b, alpha=alpha, tm=SAMPLE_ROWS)
            pool_s.append(jnp.concatenate([state_pool[i][:, 1:, :], us[:, None, :]], axis=1))
            C_s.append(C)
            n_s.append(n[:, :, 0, :])
            m_s.append(m[:, :, 0, 0])
        else:
            z = _proj(xp, w_qkv_b[i], tm=1024, tn=1024)
            o = _moba_prompt(z, B=B, S=S, hd=hd)
            xp = _proj_ln([o], w_o_b[i], xp, ln_g, ln_b, alpha=alpha, tm=512)
            kp.append(z[:, d_q:d_q + d_kv].reshape(B, S, N_KV_HEADS, hd))
            vp.append(z[:, d_q + d_kv:].reshape(B, S, N_KV_HEADS, hd))
            zs = _proj(xs, w_qkv_b[i], tm=SAMPLE_ROWS, tn=1024)
            q_s = zs[:Bs, :d_q].reshape(Bs, N_HEADS, hd)
            k_new = zs[:Bs, d_q:d_q + d_kv].reshape(Bs, N_KV_HEADS, 1, hd)
            v_new = zs[:Bs, d_q + d_kv:].reshape(Bs, N_KV_HEADS, 1, hd)
            idx = _topk_blocks(q_s, kmean_past[i])[:, :, :MOBA_TOPK].reshape(-1)
            os_ = _moba_step(q_s[:, :, None, :], k_new, v_new, cache_k4, cache_v4, idx, pt_flat,
                             layer=i, n_pages=n_pages)
            xs = _proj_ln([_pad_rows(os_.reshape(Bs, d_q), SAMPLE_ROWS)], w_o_b[i], xs, ln_g, ln_b,
                          alpha=alpha, tm=SAMPLE_ROWS)
            ks_.append(k_new.reshape(Bs, 1, N_KV_HEADS, hd))
            vs_.append(v_new.reshape(Bs, 1, N_KV_HEADS, hd))
        fg, fb = ln_ffn_g[l][None, :], ln_ffn_b[l][None, :]
        xp = _ffn(xp, w_up_b[l], w_down_b[l], fg, fb, alpha=alpha, tm=512, tf=1024)
        xs = _ffn(xs, w_up_b[l], w_down_b[l], fg, fb, alpha=alpha, tm=SAMPLE_ROWS, tf=1024)

    return (xp.reshape(B, S, D), xs[:Bs].reshape(Bs, 1, D),
            jnp.stack(kp), jnp.stack(vp), jnp.stack(ks_), jnp.stack(vs_),
            jnp.stack(pool_p), jnp.stack(C_p), jnp.stack(n_p), jnp.stack(m_p),
            jnp.stack(pool_s), jnp.stack(C_s), jnp.stack(n_s), jnp.stack(m_s))
```

```python
import functools
import math

import jax
import jax.numpy as jnp
from jax import lax
from jax.experimental import pallas as pl
from jax.experimental.pallas import tpu as pltpu

F32 = jnp.float32
BF16 = jnp.bfloat16

POOL_WINDOWS = (2, 4, 8, 16)
POOL_BUF = max(POOL_WINDOWS) - 1
POOL_HALO = 16
MH = 4
N_HEADS = 16
N_KV_HEADS = 4
GROUP = N_HEADS // N_KV_HEADS
MOBA_BLOCK = 256
MOBA_TOPK = 3
PAGE_SIZE = 128
LN_EPS = 1e-5
HEAD_NORM_EPS = 1e-6
LANES = 128
SUBLANES = 8
SAMPLE_ROWS = 16
LOG2E = 1.4426950408889634

VMEM_LIMIT = 56 * 1024 * 1024
MLSTM_CHUNK = 256


def _params(*sem):
    return pltpu.CompilerParams(dimension_semantics=sem, vmem_limit_bytes=VMEM_LIMIT)


def _layer_norm(y, g, b):
    mu = jnp.mean(y, axis=-1, keepdims=True)
    yc = y - mu
    var = jnp.mean(yc * yc, axis=-1, keepdims=True)
    return yc * lax.rsqrt(var + LN_EPS) * g + b


def _log_sigmoid(x):
    return jnp.minimum(x, 0.0) - jnp.log1p(jnp.exp(-jnp.abs(x)))


def _proj_gates_body(x_ref, w_ref, wg_ref, o_ref, g_ref, xb_ref):
    @pl.when(pl.program_id(1) == 0)
    def _():
        xb = x_ref[...].astype(BF16)
        xb_ref[...] = xb
        g_ref[...] = jnp.dot(xb, wg_ref[...], preferred_element_type=F32)

    o_ref[...] = jnp.dot(xb_ref[...], w_ref[...], preferred_element_type=F32)


def _proj_gates(x, w_all, wg_all, layer, *, n_main, tm, tn):
    M, K = x.shape
    return pl.pallas_call(
        _proj_gates_body, grid=(M // tm, n_main // tn),
        in_specs=[pl.BlockSpec((tm, K), lambda i, j: (i, 0)),
                  pl.BlockSpec((None, K, tn), lambda i, j: (layer, 0, j)),
                  pl.BlockSpec((None, K, LANES), lambda i, j: (layer, 0, 0))],
        out_specs=[pl.BlockSpec((tm, tn), lambda i, j: (i, j)),
                   pl.BlockSpec((tm, LANES), lambda i, j: (i, 0))],
        out_shape=[jax.ShapeDtypeStruct((M, n_main), F32), jax.ShapeDtypeStruct((M, LANES), F32)],
        scratch_shapes=[pltpu.VMEM((tm, K), BF16)],
        compiler_params=_params("parallel", "arbitrary"), name="proj_gates")(x, w_all, wg_all)


def _proj_qkv_body(x_ref, w_ref, q_ref, k_ref, v_ref, xb_ref, *, nq_tiles):
    j = pl.program_id(1)

    @pl.when(j == 0)
    def _():
        xb_ref[...] = x_ref[...].astype(BF16)

    y = jnp.dot(xb_ref[...], w_ref[...], preferred_element_type=F32)

    @pl.when(j < nq_tiles)
    def _():
        q_ref[...] = y.astype(q_ref.dtype)

    @pl.when(j == nq_tiles)
    def _():
        k_ref[...] = y

    @pl.when(j == nq_tiles + 1)
    def _():
        v_ref[...] = y


def _proj_qkv(x, w_all, layer, *, d_q, d_kv, tm):
    M, K = x.shape
    tn = d_kv
    nq_tiles = d_q // tn
    return pl.pallas_call(
        functools.partial(_proj_qkv_body, nq_tiles=nq_tiles),
        grid=(M // tm, nq_tiles + 2),
        in_specs=[pl.BlockSpec((tm, K), lambda i, j: (i, 0)),
                  pl.BlockSpec((None, K, tn), lambda i, j: (layer, 0, j))],
        out_specs=[pl.BlockSpec((tm, tn), lambda i, j: (i, jnp.minimum(j, nq_tiles - 1))),
                   pl.BlockSpec((tm, tn), lambda i, j: (i, 0)),
                   pl.BlockSpec((tm, tn), lambda i, j: (i, 0))],
        out_shape=[jax.ShapeDtypeStruct((M, d_q), BF16),
                   jax.ShapeDtypeStruct((M, d_kv), F32),
                   jax.ShapeDtypeStruct((M, d_kv), F32)],
        scratch_shapes=[pltpu.VMEM((tm, K), BF16)],
        compiler_params=_params("parallel", "arbitrary"), name="proj_qkv")(x, w_all)


def _proj_ln_body(*refs, n_a, alpha):
    a_refs = refs[:n_a]
    w_ref, res_ref, g_ref, b_ref, o_ref = refs[n_a:]
    y = alpha * res_ref[...]
    k0 = 0
    for a_ref in a_refs:
        ka = a_ref.shape[1]
        y = y + jnp.dot(a_ref[...], w_ref[k0:k0 + ka, :], preferred_element_type=F32)
        k0 += ka
    o_ref[...] = _layer_norm(y, g_ref[...], b_ref[...])


def _proj_ln(a_list, w_all, layer, res, g, b, *, alpha, tm):
    M, D = res.shape
    K = w_all.shape[1]
    in_specs = [pl.BlockSpec((tm, a.shape[1]), lambda i: (i, 0)) for a in a_list]
    in_specs += [pl.BlockSpec((None, K, D), lambda i: (layer, 0, 0)),
                 pl.BlockSpec((tm, D), lambda i: (i, 0)),
                 pl.BlockSpec((1, D), lambda i: (0, 0)),
                 pl.BlockSpec((1, D), lambda i: (0, 0))]
    return pl.pallas_call(
        functools.partial(_proj_ln_body, n_a=len(a_list), alpha=alpha),
        grid=(M // tm,), in_specs=in_specs,
        out_specs=pl.BlockSpec((tm, D), lambda i: (i, 0)),
        out_shape=jax.ShapeDtypeStruct((M, D), F32),
        compiler_params=_params("parallel"), name="proj_ln")(*a_list, w_all, res, g, b)


def _ffn_body(x_ref, wu_ref, wd_ref, g_ref, b_ref, o_ref, xb_ref, acc_ref, *, alpha):
    f = pl.program_id(1)

    @pl.when(f == 0)
    def _():
        xb_ref[...] = x_ref[...].astype(BF16)
        acc_ref[...] = jnp.zeros_like(acc_ref)

    h = jnp.dot(xb_ref[...], wu_ref[...], preferred_element_type=F32)
    h = jnp.square(jnp.maximum(h, 0.0)).astype(BF16)
    acc_ref[...] += jnp.dot(h, wd_ref[...], preferred_element_type=F32)

    @pl.when(f == pl.num_programs(1) - 1)
    def _():
        y = alpha * x_ref[...] + acc_ref[...]
        o_ref[...] = _layer_norm(y, g_ref[...], b_ref[...])


def _ffn(x, wu_all, wd_all, layer, g, b, *, alpha, tm, tf):
    M, D = x.shape
    Fd = wu_all.shape[2]
    return pl.pallas_call(
        functools.partial(_ffn_body, alpha=alpha),
        grid=(M // tm, Fd // tf),
        in_specs=[pl.BlockSpec((tm, D), lambda i, f: (i, 0)),
                  pl.BlockSpec((None, D, tf), lambda i, f: (layer, 0, f)),
                  pl.BlockSpec((None, tf, D), lambda i, f: (layer, f, 0)),
                  pl.BlockSpec((1, D), lambda i, f: (0, 0)),
                  pl.BlockSpec((1, D), lambda i, f: (0, 0))],
        out_specs=pl.BlockSpec((tm, D), lambda i, f: (i, 0)),
        out_shape=jax.ShapeDtypeStruct((M, D), F32),
        scratch_shapes=[pltpu.VMEM((tm, D), BF16), pltpu.VMEM((tm, D), F32)],
        compiler_params=_params("parallel", "arbitrary"), name="ffn")(x, wu_all, wd_all, g, b)


def _pool_prompt_body(u_ref, wp_ref, sc_ref, y_ref, ext_ref, *, T, gc):
    t = pl.program_id(1)

    @pl.when(t == 0)
    def _():
        ext_ref[0:POOL_HALO, :] = jnp.zeros((POOL_HALO, ext_ref.shape[1]), F32)

    ext_ref[POOL_HALO:POOL_HALO + T, :] = u_ref[...]
    pos = t * T + lax.broadcasted_iota(jnp.int32, (T, 1), 0)
    for gi, w in enumerate(POOL_WINDOWS):
        c0, c1 = gi * gc, (gi + 1) * gc
        tok = ext_ref[POOL_HALO:POOL_HALO + T, c0:c1]
        acc = tok
        for k in range(1, w):
            acc = acc + ext_ref[POOL_HALO - k:POOL_HALO - k + T, c0:c1]
        cnt = jnp.minimum(w, pos + 1).astype(F32)
        d = acc / cnt - tok
        y = jnp.dot(d.astype(BF16), wp_ref[gi], preferred_element_type=F32) * sc_ref[:, c0:c1]
        y_ref[:, c0:c1] = y.astype(y_ref.dtype)
    ext_ref[0:POOL_HALO, :] = ext_ref[T:T + POOL_HALO, :]


def _pool_prompt(z, wp_all, layer, sc, *, B, S, d_pool, T):
    nt = S // T
    gc = d_pool // len(POOL_WINDOWS)
    return pl.pallas_call(
        functools.partial(_pool_prompt_body, T=T, gc=gc),
        grid=(B, nt),
        in_specs=[pl.BlockSpec((T, d_pool), lambda b, t: (b * nt + t, 0)),
                  pl.BlockSpec((None,) + wp_all.shape[1:], lambda b, t: (layer, 0, 0, 0)),
                  pl.BlockSpec((1, d_pool), lambda b, t: (0, 0))],
        out_specs=pl.BlockSpec((T, d_pool), lambda b, t: (b * nt + t, 0)),
        out_shape=jax.ShapeDtypeStruct((B * S, d_pool), BF16),
        scratch_shapes=[pltpu.VMEM((POOL_HALO + T, d_pool), F32)],
        compiler_params=_params("parallel", "arbitrary"), name="pool_prompt")(z, wp_all, sc)


def _pool_step_body(buf_ref, u_ref, wp_ref, sc_ref, y_ref, *, gc, pos0):
    u = u_ref[...]
    for gi, w in enumerate(POOL_WINDOWS):
        c0, c1 = gi * gc, (gi + 1) * gc
        tok = u[:, c0:c1]
        acc = tok
        for k in range(1, w):
            acc = acc + buf_ref[POOL_BUF - k, :, c0:c1]
        d = acc / float(min(w, pos0 + 1)) - tok
        y = jnp.dot(d.astype(BF16), wp_ref[gi], preferred_element_type=F32) * sc_ref[:, c0:c1]
        y_ref[:, c0:c1] = y.astype(y_ref.dtype)


def _pool_step(buf_t, u, wp, sc, *, pos0):
    Bs, d_pool = u.shape
    gc = d_pool // len(POOL_WINDOWS)
    return pl.pallas_call(
        functools.partial(_pool_step_body, gc=gc, pos0=pos0),
        out_shape=jax.ShapeDtypeStruct((Bs, d_pool), BF16),
        compiler_params=pltpu.CompilerParams(vmem_limit_bytes=VMEM_LIMIT),
        name="pool_step")(buf_t, u, wp, sc)


def _head_norm_gate(h, o, g):
    mu = jnp.mean(h, axis=-1, keepdims=True)
    hc = h - mu
    var = jnp.mean(hc * hc, axis=-1, keepdims=True)
    return hc * lax.rsqrt(var + HEAD_NORM_EPS) * g * jax.nn.sigmoid(o)


def _mlstm_prompt_body(q_ref, k_ref, v_ref, o_ref, gt_ref, bias_ref, hg_ref,
                       hn_ref, C_ref, n_ref, m_ref, *, L, dk):
    c = pl.program_id(1)

    @pl.when(c == 0)
    def _():
        C_ref[...] = jnp.zeros_like(C_ref)
        n_ref[...] = jnp.zeros_like(n_ref)
        m_ref[...] = jnp.zeros_like(m_ref)

    gt = gt_ref[...] + bias_ref[...]
    gT = gt.T
    row = lax.broadcasted_iota(jnp.int32, (L, L), 0)
    col = lax.broadcasted_iota(jnp.int32, (L, L), 1)
    causal = col <= row
    neg_inf = jnp.float32(-jnp.inf)
    k_scale = dk ** -0.5
    for h in range(MH):
        c0, c1 = h * dk, (h + 1) * dk
        ig_col = gt[:, h:h + 1]
        lf_col = _log_sigmoid(gt[:, MH + h:MH + h + 1])
        ig_row = gT[h:h + 1, :]
        lf_row = _log_sigmoid(gT[MH + h:MH + h + 1, :])
        b_col = jnp.sum(jnp.where(causal, lf_row, 0.0), axis=1, keepdims=True)
        b_row = jnp.sum(jnp.where(row <= col, lf_col, 0.0), axis=0, keepdims=True)
        a_row = ig_row - b_row
        a_col = ig_col - b_col
        cmax = jnp.max(jnp.where(causal, a_row, neg_inf), axis=1, keepdims=True)
        m_prev = m_ref[0, h, :, 0:1]
        m_t = b_col + jnp.maximum(m_prev, cmax)
        w_inter = jnp.exp(b_col + m_prev - m_t)
        dmat = jnp.exp(jnp.where(causal, a_row + b_col - m_t, neg_inf))
        q = q_ref[:, c0:c1]
        ks = k_ref[:, c0:c1] * k_scale
        qb = q.astype(BF16)
        kb = ks.astype(BF16)
        vb = v_ref[:, c0:c1].astype(BF16)
        s = lax.dot_general(qb, kb, (((1,), (1,)), ((), ())), preferred_element_type=F32) * dmat
        C_prev = C_ref[0, h]
        n_prev = n_ref[0, h]
        num = (w_inter * jnp.dot(qb, C_prev.astype(BF16), preferred_element_type=F32)
               + jnp.dot(s.astype(BF16), vb, preferred_element_type=F32))
        den = (w_inter * jnp.sum(q * n_prev, axis=1, keepdims=True)
               + jnp.sum(s, axis=1, keepdims=True))
        hh = num / jnp.maximum(jnp.abs(den), jnp.exp(-m_t))
        hn_ref[:, c0:c1] = _head_norm_gate(hh, o_ref[:, c0:c1], hg_ref[:, c0:c1]).astype(hn_ref.dtype)

        m_new = m_t[L - 1:L, :]
        b_last = b_col[L - 1:L, :]
        w_old = jnp.exp(b_last + m_prev - m_new)
        kw = ks * jnp.exp(a_col + b_last - m_new)
        C_ref[0, h] = w_old * C_prev + lax.dot_general(
            kw.astype(BF16), vb, (((0,), (0,)), ((), ())), preferred_element_type=F32)
        n_ref[0, h] = w_old * n_prev + jnp.sum(kw, axis=0, keepdims=True)
        m_ref[0, h] = jnp.broadcast_to(m_new, (1, LANES))


def _mlstm_prompt(z, gates, bias, hg, *, B, S, d_pool, d_ml, L):
    nc = S // L
    dk = d_ml // MH
    blk0 = d_pool // d_ml
    assert blk0 * d_ml == d_pool

    def zspec(part):
        return pl.BlockSpec((L, d_ml), lambda b, c: (b * nc + c, blk0 + part))

    return pl.pallas_call(
        functools.partial(_mlstm_prompt_body, L=L, dk=dk),
        grid=(B, nc),
        in_specs=[zspec(0), zspec(1), zspec(2), zspec(3),
                  pl.BlockSpec((L, LANES), lambda b, c: (b * nc + c, 0)),
                  pl.BlockSpec((1, LANES), lambda b, c: (0, 0)),
                  pl.BlockSpec((1, d_ml), lambda b, c: (0, 0))],
        out_specs=[pl.BlockSpec((L, d_ml), lambda b, c: (b * nc + c, 0)),
                   pl.BlockSpec((1, MH, dk, dk), lambda b, c: (b, 0, 0, 0)),
                   pl.BlockSpec((1, MH, 1, dk), lambda b, c: (b, 0, 0, 0)),
                   pl.BlockSpec((1, MH, 1, LANES), lambda b, c: (b, 0, 0, 0))],
        out_shape=[jax.ShapeDtypeStruct((B * S, d_ml), BF16),
                   jax.ShapeDtypeStruct((B, MH, dk, dk), F32),
                   jax.ShapeDtypeStruct((B, MH, 1, dk), F32),
                   jax.ShapeDtypeStruct((B, MH, 1, LANES), F32)],
        compiler_params=_params("parallel", "arbitrary"), name="mlstm_prompt")(
            z, z, z, z, gates, bias, hg)


def _row_to_col(x_row, n):
    r = lax.broadcasted_iota(jnp.int32, (n, n), 0)
    c = lax.broadcasted_iota(jnp.int32, (n, n), 1)
    return jnp.sum(jnp.where(r == c, x_row, 0.0), axis=1, keepdims=True)


def _mlstm_step_body(z_ref, gt_ref, bias_ref, hg_ref, C0_ref, n0_ref, m0_ref,
                     hn_ref, C_ref, n_ref, m_ref, *, d_pool, d_ml, dk):
    gt = gt_ref[0] + bias_ref[...]
    k_scale = dk ** -0.5
    for h in range(MH):
        base = d_pool + h * dk
        q = z_ref[0, :, base:base + dk]
        ks = z_ref[0, :, base + d_ml:base + d_ml + dk] * k_scale
        v = z_ref[0, :, base + 2 * d_ml:base + 2 * d_ml + dk]
        o = z_ref[0, :, base + 3 * d_ml:base + 3 * d_ml + dk]
        ig = gt[:, h:h + 1]
        lf = _log_sigmoid(gt[:, MH + h:MH + h + 1])
        m_prev = m0_ref[0, h]
        m_new = jnp.maximum(m_prev + lf, ig)
        w_old = jnp.exp(lf + m_prev - m_new)
        w_s = jnp.exp(ig - m_new)
        C_new = w_old * C0_ref[0, h] + _row_to_col(ks * w_s, dk) * v
        n_new = w_old * n0_ref[0, h] + w_s * ks
        num = jnp.sum(_row_to_col(q, dk) * C_new, axis=0, keepdims=True)
        den = jnp.sum(q * n_new, axis=1, keepdims=True)
        hh = num / jnp.maximum(jnp.abs(den), jnp.exp(-m_new))
        c0 = h * dk
        hn_ref[0, :, c0:c0 + dk] = _head_norm_gate(hh, o, hg_ref[:, c0:c0 + dk]).astype(hn_ref.dtype)
        C_ref[0, h] = C_new
        n_ref[0, h] = n_new
        m_ref[0, h] = m_new


def _mlstm_step(z3, g3, bias, hg, C0, n0, m0, *, d_pool, d_ml):
    Bs = C0.shape[0]
    dk = d_ml // MH
    dz = z3.shape[-1]
    st = lambda *shape: pl.BlockSpec((1,) + shape, lambda b: (b,) + (0,) * len(shape))
    return pl.pallas_call(
        functools.partial(_mlstm_step_body, d_pool=d_pool, d_ml=d_ml, dk=dk),
        grid=(Bs,),
        in_specs=[st(1, dz), st(1, LANES),
                  pl.BlockSpec((1, LANES), lambda b: (0, 0)),
                  pl.BlockSpec((1, d_ml), lambda b: (0, 0)),
                  st(MH, dk, dk), st(MH, 1, dk), st(MH, 1, 1)],
        out_specs=[st(1, d_ml), st(MH, dk, dk), st(MH, 1, dk), st(MH, 1, 1)],
        out_shape=[jax.ShapeDtypeStruct((Bs, 1, d_ml), BF16),
                   jax.ShapeDtypeStruct((Bs, MH, dk, dk), F32),
                   jax.ShapeDtypeStruct((Bs, MH, 1, dk), F32),
                   jax.ShapeDtypeStruct((Bs, MH, 1, 1), F32)],
        compiler_params=_params("parallel"), name="mlstm_step")(z3, g3, bias, hg, C0, n0, m0)


KM_ROWS = 16


def _moba_prompt_body(q_ref, k_ref, v_ref, o_ref, kb_ref, vT_ref, km_ref, qT_ref, pT_ref,
                      m_ref, l_ref, acc_ref, *, nblk, hd):
    j = pl.program_id(2)
    blk = MOBA_BLOCK
    R = GROUP * blk
    c_exp = hd ** -0.5 * LOG2E
    neg_inf = jnp.float32(-jnp.inf)

    @pl.when(j == 0)
    def _():
        kb_ref[...] = k_ref[...].astype(BF16)
        vT_ref[...] = v_ref[...].T.astype(BF16)
        km = [jnp.sum(k_ref[n * blk:(n + 1) * blk, :], axis=0, keepdims=True) * (1.0 / blk)
              for n in range(nblk)]
        km.append(jnp.zeros((KM_ROWS - nblk, hd), F32))
        km_ref[...] = jnp.concatenate(km, axis=0).astype(BF16)

    for g in range(GROUP):
        qT_ref[:, g * blk:(g + 1) * blk] = q_ref[:, g * hd:(g + 1) * hd].astype(F32).T.astype(BF16)

    gate = jnp.dot(km_ref[...], qT_ref[...], preferred_element_type=F32)[0:nblk, :]
    sub = lax.broadcasted_iota(jnp.int32, (nblk, R), 0)
    past = sub < j
    sel = []
    for n in range(nblk - 1):
        gn = gate[n:n + 1, :]
        beats = ((gate > gn) | ((gate == gn) & (sub < n))) & past
        cnt = jnp.sum(beats.astype(F32), axis=0, keepdims=True)
        sel.append((cnt < MOBA_TOPK).astype(F32))

    def attend(start, sel_n):
        kb = kb_ref[pl.ds(start, blk), :]
        for c in range(R // LANES):
            lanes = slice(c * LANES, (c + 1) * LANES)
            s = jnp.dot(kb, qT_ref[:, lanes], preferred_element_type=F32)
            if sel_n is None:
                key = lax.broadcasted_iota(jnp.int32, (blk, LANES), 0)
                qry = lax.broadcasted_iota(jnp.int32, (blk, LANES), 1) + (c * LANES) % blk
                s = jnp.where(key <= qry, s, neg_inf)
                m_new = jnp.max(s, axis=0, keepdims=True)
                p = jnp.exp2((s - m_new) * c_exp)
                l_new = jnp.sum(p, axis=0, keepdims=True)
            else:
                s = jnp.where(sel_n[:, lanes] > 0.5, s, neg_inf)
                m_old = m_ref[:, lanes]
                m_new = jnp.maximum(m_old, jnp.max(s, axis=0, keepdims=True))
                alpha = jnp.exp2((m_old - m_new) * c_exp)
                p = jnp.exp2((s - m_new) * c_exp)
                l_new = alpha * l_ref[:, lanes] + jnp.sum(p, axis=0, keepdims=True)
                acc_ref[:, lanes] = acc_ref[:, lanes] * alpha
            m_ref[:, lanes] = m_new
            l_ref[:, lanes] = l_new
            pT_ref[:, lanes] = p.astype(BF16)
        pv = jnp.dot(vT_ref[:, pl.ds(start, blk)], pT_ref[...], preferred_element_type=F32)
        if sel_n is None:
            acc_ref[...] = pv
        else:
            acc_ref[...] += pv

    attend(pl.multiple_of(j * blk, blk), None)
    for n in range(nblk - 1):
        @pl.when(n < j)
        def _(n=n):
            attend(n * blk, sel[n])

    inv_l = 1.0 / l_ref[...]
    for g in range(GROUP):
        cols = slice(g * blk, (g + 1) * blk)
        o_ref[:, g * hd:(g + 1) * hd] = (acc_ref[:, cols] * inv_l[:, cols]).T.astype(o_ref.dtype)


def _moba_prompt(q, k, v, *, B, S, hd):
    assert S % MOBA_BLOCK == 0 and S // MOBA_BLOCK <= KM_ROWS
    nq = S // MOBA_BLOCK
    gw = GROUP * hd
    R = GROUP * MOBA_BLOCK
    return pl.pallas_call(
        functools.partial(_moba_prompt_body, nblk=nq, hd=hd),
        grid=(B, N_KV_HEADS, nq),
        in_specs=[pl.BlockSpec((MOBA_BLOCK, gw), lambda b, h, j: (b * nq + j, h)),
                  pl.BlockSpec((S, hd), lambda b, h, j: (b, h)),
                  pl.BlockSpec((S, hd), lambda b, h, j: (b, h))],
        out_specs=pl.BlockSpec((MOBA_BLOCK, gw), lambda b, h, j: (b * nq + j, h)),
        out_shape=jax.ShapeDtypeStruct((B * S, N_HEADS * hd), BF16),
        scratch_shapes=[pltpu.VMEM((S, hd), BF16),
                        pltpu.VMEM((hd, S), BF16),
                        pltpu.VMEM((KM_ROWS, hd), BF16),
                        pltpu.VMEM((hd, R), BF16),
                        pltpu.VMEM((MOBA_BLOCK, R), BF16),
                        pltpu.VMEM((1, R), F32),
                        pltpu.VMEM((1, R), F32),
                        pltpu.VMEM((hd, R), F32)],
        compiler_params=_params("parallel", "parallel", "arbitrary"), name="moba_prompt")(q, k, v)


PAGES_PER_STEP = 16
PAGES_PER_BLOCK = MOBA_BLOCK // PAGE_SIZE
TOKENS_PER_TILE = SUBLANES // N_KV_HEADS
KV_SHIFT = N_KV_HEADS.bit_length() - 1
GROUP_SHIFT = GROUP.bit_length() - 1
assert 1 << KV_SHIFT == N_KV_HEADS and 1 << GROUP_SHIFT == GROUP and TOKENS_PER_TILE * N_KV_HEADS == SUBLANES


def _kmean_body(pt_ref, *refs):
    page_refs, o_ref = refs[:PAGES_PER_STEP], refs[PAGES_PER_STEP]
    rows, hd = page_refs[0].shape[2:]
    for blk in range(PAGES_PER_STEP // PAGES_PER_BLOCK):
        acc = None
        for r in range(PAGES_PER_BLOCK):
            pg = page_refs[blk * PAGES_PER_BLOCK + r][0, 0]
            part = jnp.sum(pg.reshape(rows // SUBLANES, SUBLANES, hd), axis=0)
            acc = part if acc is None else acc + part
        per_head = acc[0:N_KV_HEADS]
        for t in range(1, TOKENS_PER_TILE):
            per_head = per_head + acc[t * N_KV_HEADS:(t + 1) * N_KV_HEADS]
        o_ref[0, 0, blk * N_KV_HEADS:(blk + 1) * N_KV_HEADS, :] = per_head * (1.0 / MOBA_BLOCK)


def _kmean_paged(cache_k4, pt_flat, *, Bs, n_pages):
    n_attn, _, rows, hd = cache_k4.shape
    ng = n_pages // PAGES_PER_STEP
    out_rows = PAGES_PER_STEP // PAGES_PER_BLOCK * N_KV_HEADS

    def page_spec(r):
        return pl.BlockSpec(
            (1, 1, rows, hd),
            lambda l, b, g, pt: (l, pt[b * n_pages + g * PAGES_PER_STEP + r], 0, 0))

    return pl.pallas_call(
        _kmean_body,
        grid_spec=pltpu.PrefetchScalarGridSpec(
            num_scalar_prefetch=1, grid=(n_attn, Bs, ng),
            in_specs=[page_spec(r) for r in range(PAGES_PER_STEP)],
            out_specs=pl.BlockSpec((1, 1, out_rows, hd), lambda l, b, g, pt: (l, b, g, 0))),
        out_shape=jax.ShapeDtypeStruct((n_attn, Bs, ng * out_rows, hd), F32),
        compiler_params=_params("parallel", "parallel", "arbitrary"), name="kmean_paged")(
            pt_flat, *([cache_k4] * PAGES_PER_STEP))


def _topk_body(q_ref, km_ref, idx_ref, *, ncol):
    qb = q_ref[0].astype(BF16)
    km = km_ref[0, 0].astype(BF16)
    gate = lax.dot_general(qb, km, (((1,), (1,)), ((), ())), preferred_element_type=F32)
    head = lax.broadcasted_iota(jnp.int32, (N_HEADS, ncol), 0)
    col = lax.broadcasted_iota(jnp.int32, (N_HEADS, ncol), 1)
    gate = jnp.where((col & (N_KV_HEADS - 1)) == (head >> GROUP_SHIFT), gate, -jnp.inf)
    out_lane = lax.broadcasted_iota(jnp.int32, (N_HEADS, LANES), 1)
    out = jnp.zeros((N_HEADS, LANES), jnp.int32)
    for r in range(MOBA_TOPK):
        mx = jnp.max(gate, axis=1, keepdims=True)
        idx = jnp.min(jnp.where(gate == mx, col, ncol), axis=1, keepdims=True)
        out = jnp.where(out_lane == r, idx >> KV_SHIFT, out)
        gate = jnp.where(col == idx, -jnp.inf, gate)
    idx_ref[0] = out


def _topk_blocks(q3, kmean, layer):
    Bs, _, hd = q3.shape
    ncol = kmean.shape[2]
    assert ncol >= MOBA_TOPK * N_KV_HEADS
    return pl.pallas_call(
        functools.partial(_topk_body, ncol=ncol),
        grid=(Bs,),
        in_specs=[pl.BlockSpec((1, N_HEADS, hd), lambda b: (b, 0, 0)),
                  pl.BlockSpec((1, 1, ncol, hd), lambda b: (layer, b, 0, 0))],
        out_specs=pl.BlockSpec((1, N_HEADS, LANES), lambda b: (b, 0, 0)),
        out_shape=jax.ShapeDtypeStruct((Bs, N_HEADS, LANES), jnp.int32),
        compiler_params=_params("parallel"), name="topk_blocks")(q3, kmean)


N_SEL_PAGES = MOBA_TOPK * PAGES_PER_BLOCK


def _moba_step_body(idx_ref, pt_ref, q_ref, kn_ref, vn_ref, *refs, hd):
    k_refs = refs[:N_SEL_PAGES]
    v_refs = refs[N_SEL_PAGES:2 * N_SEL_PAGES]
    o_ref = refs[2 * N_SEL_PAGES]
    kv_head = pl.program_id(1) // GROUP
    scale = hd ** -0.5
    rows = SAMPLE_ROWS
    ncol = k_refs[0].shape[2]
    mine = (lax.broadcasted_iota(jnp.int32, (rows, ncol), 1) & (N_KV_HEADS - 1)) == kv_head
    qb = jnp.broadcast_to(q_ref[0, 0], (rows, hd)).astype(BF16)
    s_pages = [jnp.where(mine,
                         lax.dot_general(qb, kr[0, 0].astype(BF16), (((1,), (1,)), ((), ())),
                                         preferred_element_type=F32) * scale,
                         -jnp.inf) for kr in k_refs]
    s_own = jnp.sum(qb.astype(F32) * kn_ref[0, 0].astype(BF16).astype(F32), axis=1, keepdims=True) * scale
    m = s_own
    for s in s_pages:
        m = jnp.maximum(m, jnp.max(s, axis=1, keepdims=True))
    p_own = jnp.exp(s_own - m)
    l = p_own
    acc = p_own * vn_ref[0, 0].astype(BF16).astype(F32)
    for s, vr in zip(s_pages, v_refs):
        p = jnp.exp(s - m)
        l = l + jnp.sum(p, axis=1, keepdims=True)
        acc = acc + jnp.dot(p.astype(BF16), vr[0, 0].astype(BF16), preferred_element_type=F32)
    o_ref[0, 0] = (acc / l)[0:1, :].astype(o_ref.dtype)


def _moba_step(q4, kn4, vn4, cache_k4, cache_v4, idx_flat, pt_flat, *, layer, n_pages):
    Bs, _, _, hd = q4.shape
    rows = cache_k4.shape[2]

    def page_spec(r):
        sel, half = divmod(r, PAGES_PER_BLOCK)

        def index_map(b, hq, idx, pt):
            blk = idx[(b * N_HEADS + hq) * MOBA_TOPK + sel]
            return (layer, pt[b * n_pages + blk * PAGES_PER_BLOCK + half], 0, 0)

        return pl.BlockSpec((1, 1, rows, hd), index_map)

    head_spec = pl.BlockSpec((1, 1, 1, hd), lambda b, hq, idx, pt: (b, hq, 0, 0))
    kv_spec = pl.BlockSpec((1, 1, 1, hd), lambda b, hq, idx, pt: (b, hq // GROUP, 0, 0))
    return pl.pallas_call(
        functools.partial(_moba_step_body, hd=hd),
        grid_spec=pltpu.PrefetchScalarGridSpec(
            num_scalar_prefetch=2, grid=(Bs, N_HEADS),
            in_specs=[head_spec, kv_spec, kv_spec]
            + [page_spec(r) for r in range(N_SEL_PAGES)] * 2,
            out_specs=head_spec),
        out_shape=jax.ShapeDtypeStruct((Bs, N_HEADS, 1, hd), BF16),
        compiler_params=_params("parallel", "arbitrary"), name="moba_step")(
            idx_flat, pt_flat, q4, kn4, vn4,
            *([cache_k4] * N_SEL_PAGES), *([cache_v4] * N_SEL_PAGES))


def _pad_rows(a, rows):
    return jnp.pad(a, ((0, rows - a.shape[0]),) + ((0, 0),) * (a.ndim - 1))


def kernel(x_prompt, x_sample, cache_k, cache_v, state_pool, state_C, state_n, state_m, page_table,
           w_in_mix, b_if, w_pool, pool_scale, mlstm_norm_g, w_out_mix, w_qkv, w_o,
           ln_mix_g, ln_mix_b, ln_ffn_g, ln_ffn_b, w_up, w_down):
    B, S, D = x_prompt.shape
    Bs = x_sample.shape[0]
    assert x_sample.shape[1] == 1 and Bs <= SAMPLE_ROWS
    depth = w_up.shape[0]
    d_pool = state_pool.shape[-1]
    dk = state_C.shape[-1]
    d_ml = MH * dk
    d_main = d_pool + 4 * d_ml
    assert w_in_mix.shape[2] == d_main + 2 * MH
    hd = cache_k.shape[-1]
    d_q, d_kv = N_HEADS * hd, N_KV_HEADS * hd
    n_pages = page_table.shape[1]
    past_len = n_pages * PAGE_SIZE
    assert cache_k.shape[2] == PAGE_SIZE and past_len % MOBA_BLOCK == 0
    alpha = (2 * depth) ** 0.25
    M = B * S
    L = math.gcd(S, MLSTM_CHUNK)

    w_in_b = w_in_mix.astype(BF16)
    w_g_b = jnp.pad(w_in_mix[:, :, d_main:], ((0, 0), (0, 0), (0, LANES - 2 * MH))).astype(BF16)
    bias_g = jnp.pad(b_if, ((0, 0), (0, LANES - 2 * MH)))[:, None, :]
    w_pool_b = w_pool.astype(BF16)
    w_out_b = w_out_mix.astype(BF16)
    w_qkv_b = w_qkv.astype(BF16)
    w_o_b = w_o.astype(BF16)
    w_up_b = w_up.astype(BF16)
    w_down_b = w_down.astype(BF16)

    cache_k4 = cache_k.reshape(cache_k.shape[:2] + (PAGE_SIZE * N_KV_HEADS, hd))
    cache_v4 = cache_v.reshape(cache_v.shape[:2] + (PAGE_SIZE * N_KV_HEADS, hd))
    pt_flat = page_table.reshape(-1)
    kmean_past = _kmean_paged(cache_k4, pt_flat, Bs=Bs, n_pages=n_pages)

    xp = x_prompt.reshape(M, D)
    xs = _pad_rows(x_sample.reshape(Bs, D), SAMPLE_ROWS)

    kp, vp, ks_, vs_ = [], [], [], []
    pool_p, C_p, n_p, m_p = [], [], [], []
    pool_s, C_s, n_s, m_s = [], [], [], []
    for l in range(depth):
        i = l // 2
        ln_g, ln_b = ln_mix_g[l][None, :], ln_mix_b[l][None, :]
        if l % 2 == 0:
            sc = pool_scale[i][None, :]
            hg = mlstm_norm_g[i][None, :]
            z, gates = _proj_gates(xp, w_in_b, w_g_b, i, n_main=d_main, tm=1024, tn=1024)
            y_pool = _pool_prompt(z, w_pool_b, i, sc, B=B, S=S, d_pool=d_pool, T=256)
            hn, C, n, m = _mlstm_prompt(z, gates, bias_g[i], hg, B=B, S=S, d_pool=d_pool, d_ml=d_ml, L=L)
            xp = _proj_ln([y_pool, hn], w_out_b, i, xp, ln_g, ln_b, alpha=alpha, tm=512)
            pool_p.append(z.reshape(B, S, d_main)[:, S - POOL_BUF:, :d_pool])
            C_p.append(C)
            n_p.append(n.reshape(B, MH, dk))
            m_p.append(m[:, :, 0, 0])
            zs, gs = _proj_gates(xs, w_in_b, w_g_b, i, n_main=d_main, tm=SAMPLE_ROWS, tn=1024)
            us = zs[:Bs, :d_pool]
            ys_pool = _pool_step(jnp.transpose(state_pool[i], (1, 0, 2)), us, w_pool_b[i], sc, pos0=past_len)
            hns, C, n, m = _mlstm_step(
                zs[:Bs, None, :], gs[:Bs, None, :], bias_g[i], hg, state_C[i],
                state_n[i][:, :, None, :], state_m[i][:, :, None, None], d_pool=d_pool, d_ml=d_ml)
            xs = _proj_ln([_pad_rows(ys_pool, SAMPLE_ROWS), _pad_rows(hns[:, 0, :], SAMPLE_ROWS)],
                          w_out_b, i, xs, ln_g, ln_b, alpha=alpha, tm=SAMPLE_ROWS)
            pool_s.append(jnp.concatenate([state_pool[i][:, 1:, :], us[:, None, :]], axis=1))
            C_s.append(C)
            n_s.append(n[:, :, 0, :])
            m_s.append(m[:, :, 0, 0])
        else:
            q, k, v = _proj_qkv(xp, w_qkv_b, i, d_q=d_q, d_kv=d_kv, tm=1024)
            o = _moba_prompt(q, k, v, B=B, S=S, hd=hd)
            xp = _proj_ln([o], w_o_b, i, xp, ln_g, ln_b, alpha=alpha, tm=512)
            kp.append(k.reshape(B, S, N_KV_HEADS, hd))
            vp.append(v.reshape(B, S, N_KV_HEADS, hd))
            qs, k_new, v_new = _proj_qkv(xs, w_qkv_b, i, d_q=d_q, d_kv=d_kv, tm=SAMPLE_ROWS)
            q_s = qs[:Bs].astype(F32).reshape(Bs, N_HEADS, hd)
            k_new = k_new[:Bs].reshape(Bs, N_KV_HEADS, 1, hd)
            v_new = v_new[:Bs].reshape(Bs, N_KV_HEADS, 1, hd)
            idx = _topk_blocks(q_s, kmean_past, i)[:, :, :MOBA_TOPK].reshape(-1)
            os_ = _moba_step(q_s[:, :, None, :], k_new, v_new, cache_k4, cache_v4, idx, pt_flat,
                             layer=i, n_pages=n_pages)
            xs = _proj_ln([_pad_rows(os_.reshape(Bs, d_q), SAMPLE_ROWS)], w_o_b, i, xs, ln_g, ln_b,
                          alpha=alpha, tm=SAMPLE_ROWS)
            ks_.append(k_new.reshape(Bs, 1, N_KV_HEADS, hd))
            vs_.append(v_new.reshape(Bs, 1, N_KV_HEADS, hd))
        fg, fb = ln_ffn_g[l][None, :], ln_ffn_b[l][None, :]
        xp = _ffn(xp, w_up_b, w_down_b, l, fg, fb, alpha=alpha, tm=512, tf=1024)
        xs = _ffn(xs, w_up_b, w_down_b, l, fg, fb, alpha=alpha, tm=SAMPLE_ROWS, tf=1024)

    return (xp.reshape(B, S, D), xs[:Bs].reshape(Bs, 1, D),
            jnp.stack(kp), jnp.stack(vp), jnp.stack(ks_), jnp.stack(vs_),
            jnp.stack(pool_p), jnp.stack(C_p), jnp.stack(n_p), jnp.stack(m_p),
            jnp.stack(pool_s), jnp.stack(C_s), jnp.stack(n_s), jnp.stack(m_s))
```

```python
import functools
import math

import jax
import jax.numpy as jnp
from jax import lax
from jax.experimental import pallas as pl
from jax.experimental.pallas import tpu as pltpu

F32 = jnp.float32
BF16 = jnp.bfloat16

POOL_WINDOWS = (2, 4, 8, 16)
POOL_BUF = max(POOL_WINDOWS) - 1
POOL_HALO = 16
MH = 4
N_HEADS = 16
N_KV_HEADS = 4
GROUP = N_HEADS // N_KV_HEADS
MOBA_BLOCK = 256
MOBA_TOPK = 3
PAGE_SIZE = 128
LN_EPS = 1e-5
HEAD_NORM_EPS = 1e-6
LANES = 128
SUBLANES = 8
SAMPLE_ROWS = 16
LOG2E = 1.4426950408889634

VMEM_LIMIT = 56 * 1024 * 1024
MLSTM_CHUNK = 256


def _params(*sem):
    return pltpu.CompilerParams(dimension_semantics=sem, vmem_limit_bytes=VMEM_LIMIT)


def _layer_norm(y, g, b):
    mu = jnp.mean(y, axis=-1, keepdims=True)
    yc = y - mu
    var = jnp.mean(yc * yc, axis=-1, keepdims=True)
    return yc * lax.rsqrt(var + LN_EPS) * g + b


def _log_sigmoid(x):
    return jnp.minimum(x, 0.0) - jnp.log1p(jnp.exp(-jnp.abs(x)))


def _proj_gates_body(x_ref, w_ref, wg_ref, o_ref, g_ref, xb_ref):
    @pl.when(pl.program_id(1) == 0)
    def _():
        xb = x_ref[...].astype(BF16)
        xb_ref[...] = xb
        g_ref[...] = jnp.dot(xb, wg_ref[...], preferred_element_type=F32)

    o_ref[...] = jnp.dot(xb_ref[...], w_ref[...], preferred_element_type=F32)


def _proj_gates(x, w_all, wg_all, layer, *, n_main, tm, tn):
    M, K = x.shape
    return pl.pallas_call(
        _proj_gates_body, grid=(M // tm, n_main // tn),
        in_specs=[pl.BlockSpec((tm, K), lambda i, j: (i, 0)),
                  pl.BlockSpec((None, K, tn), lambda i, j: (layer, 0, j)),
                  pl.BlockSpec((None, K, LANES), lambda i, j: (layer, 0, 0))],
        out_specs=[pl.BlockSpec((tm, tn), lambda i, j: (i, j)),
                   pl.BlockSpec((tm, LANES), lambda i, j: (i, 0))],
        out_shape=[jax.ShapeDtypeStruct((M, n_main), F32), jax.ShapeDtypeStruct((M, LANES), F32)],
        scratch_shapes=[pltpu.VMEM((tm, K), BF16)],
        compiler_params=_params("parallel", "arbitrary"), name="proj_gates")(x, w_all, wg_all)


def _proj_qkv_body(x_ref, w_ref, q_ref, k_ref, v_ref, xb_ref, *, nq_tiles):
    j = pl.program_id(1)

    @pl.when(j == 0)
    def _():
        xb_ref[...] = x_ref[...].astype(BF16)

    y = jnp.dot(xb_ref[...], w_ref[...], preferred_element_type=F32)

    @pl.when(j < nq_tiles)
    def _():
        q_ref[...] = y.astype(q_ref.dtype)

    @pl.when(j == nq_tiles)
    def _():
        d_kv = k_ref.shape[1]
        k_ref[...] = y[:, :d_kv]
        v_ref[...] = y[:, d_kv:]


def _proj_qkv(x, w_all, layer, *, d_q, d_kv, tm):
    M, K = x.shape
    tn = 2 * d_kv
    nq_tiles = d_q // tn
    assert nq_tiles * tn == d_q
    return pl.pallas_call(
        functools.partial(_proj_qkv_body, nq_tiles=nq_tiles),
        grid=(M // tm, nq_tiles + 1),
        in_specs=[pl.BlockSpec((tm, K), lambda i, j: (i, 0)),
                  pl.BlockSpec((None, K, tn), lambda i, j: (layer, 0, j))],
        out_specs=[pl.BlockSpec((tm, tn), lambda i, j: (i, jnp.minimum(j, nq_tiles - 1))),
                   pl.BlockSpec((tm, d_kv), lambda i, j: (i, 0)),
                   pl.BlockSpec((tm, d_kv), lambda i, j: (i, 0))],
        out_shape=[jax.ShapeDtypeStruct((M, d_q), BF16),
                   jax.ShapeDtypeStruct((M, d_kv), F32),
                   jax.ShapeDtypeStruct((M, d_kv), F32)],
        scratch_shapes=[pltpu.VMEM((tm, K), BF16)],
        compiler_params=_params("parallel", "arbitrary"), name="proj_qkv")(x, w_all)


def _proj_ln_body(*refs, n_a, alpha):
    a_refs = refs[:n_a]
    w_ref, res_ref, g_ref, b_ref, o_ref = refs[n_a:]
    y = alpha * res_ref[...]
    k0 = 0
    for a_ref in a_refs:
        ka = a_ref.shape[1]
        y = y + jnp.dot(a_ref[...], w_ref[k0:k0 + ka, :], preferred_element_type=F32)
        k0 += ka
    o_ref[...] = _layer_norm(y, g_ref[...], b_ref[...])


def _proj_ln(a_list, w_all, layer, res, g, b, *, alpha, tm):
    M, D = res.shape
    K = w_all.shape[1]
    in_specs = [pl.BlockSpec((tm, a.shape[1]), lambda i: (i, 0)) for a in a_list]
    in_specs += [pl.BlockSpec((None, K, D), lambda i: (layer, 0, 0)),
                 pl.BlockSpec((tm, D), lambda i: (i, 0)),
                 pl.BlockSpec((1, D), lambda i: (0, 0)),
                 pl.BlockSpec((1, D), lambda i: (0, 0))]
    return pl.pallas_call(
        functools.partial(_proj_ln_body, n_a=len(a_list), alpha=alpha),
        grid=(M // tm,), in_specs=in_specs,
        out_specs=pl.BlockSpec((tm, D), lambda i: (i, 0)),
        out_shape=jax.ShapeDtypeStruct((M, D), F32),
        compiler_params=_params("parallel"), name="proj_ln")(*a_list, w_all, res, g, b)


def _ffn_body(x_ref, wu_ref, wd_ref, g_ref, b_ref, o_ref, xb_ref, acc_ref, *, alpha):
    f = pl.program_id(1)

    @pl.when(f == 0)
    def _():
        xb_ref[...] = x_ref[...].astype(BF16)
        acc_ref[...] = jnp.zeros_like(acc_ref)

    h = jnp.dot(xb_ref[...], wu_ref[...], preferred_element_type=F32)
    h = jnp.square(jnp.maximum(h, 0.0)).astype(BF16)
    acc_ref[...] += jnp.dot(h, wd_ref[...], preferred_element_type=F32)

    @pl.when(f == pl.num_programs(1) - 1)
    def _():
        y = alpha * x_ref[...] + acc_ref[...]
        o_ref[...] = _layer_norm(y, g_ref[...], b_ref[...])


def _ffn(x, wu_all, wd_all, layer, g, b, *, alpha, tm, tf):
    M, D = x.shape
    Fd = wu_all.shape[2]
    return pl.pallas_call(
        functools.partial(_ffn_body, alpha=alpha),
        grid=(M // tm, Fd // tf),
        in_specs=[pl.BlockSpec((tm, D), lambda i, f: (i, 0)),
                  pl.BlockSpec((None, D, tf), lambda i, f: (layer, 0, f)),
                  pl.BlockSpec((None, tf, D), lambda i, f: (layer, f, 0)),
                  pl.BlockSpec((1, D), lambda i, f: (0, 0)),
                  pl.BlockSpec((1, D), lambda i, f: (0, 0))],
        out_specs=pl.BlockSpec((tm, D), lambda i, f: (i, 0)),
        out_shape=jax.ShapeDtypeStruct((M, D), F32),
        scratch_shapes=[pltpu.VMEM((tm, D), BF16), pltpu.VMEM((tm, D), F32)],
        compiler_params=_params("parallel", "arbitrary"), name="ffn")(x, wu_all, wd_all, g, b)


def _pool_prompt_body(u_ref, wp_ref, sc_ref, y_ref, ext_ref, *, T, gc):
    t = pl.program_id(1)

    @pl.when(t == 0)
    def _():
        ext_ref[0:POOL_HALO, :] = jnp.zeros((POOL_HALO, ext_ref.shape[1]), F32)

    ext_ref[POOL_HALO:POOL_HALO + T, :] = u_ref[...]
    pos = t * T + lax.broadcasted_iota(jnp.int32, (T, 1), 0)
    for gi, w in enumerate(POOL_WINDOWS):
        c0, c1 = gi * gc, (gi + 1) * gc
        tok = ext_ref[POOL_HALO:POOL_HALO + T, c0:c1]
        acc = tok
        for k in range(1, w):
            acc = acc + ext_ref[POOL_HALO - k:POOL_HALO - k + T, c0:c1]
        cnt = jnp.minimum(w, pos + 1).astype(F32)
        d = acc / cnt - tok
        y = jnp.dot(d.astype(BF16), wp_ref[gi], preferred_element_type=F32) * sc_ref[:, c0:c1]
        y_ref[:, c0:c1] = y.astype(y_ref.dtype)
    ext_ref[0:POOL_HALO, :] = ext_ref[T:T + POOL_HALO, :]


def _pool_prompt(z, wp_all, layer, sc, *, B, S, d_pool, T):
    nt = S // T
    gc = d_pool // len(POOL_WINDOWS)
    return pl.pallas_call(
        functools.partial(_pool_prompt_body, T=T, gc=gc),
        grid=(B, nt),
        in_specs=[pl.BlockSpec((T, d_pool), lambda b, t: (b * nt + t, 0)),
                  pl.BlockSpec((None,) + wp_all.shape[1:], lambda b, t: (layer, 0, 0, 0)),
                  pl.BlockSpec((1, d_pool), lambda b, t: (0, 0))],
        out_specs=pl.BlockSpec((T, d_pool), lambda b, t: (b * nt + t, 0)),
        out_shape=jax.ShapeDtypeStruct((B * S, d_pool), BF16),
        scratch_shapes=[pltpu.VMEM((POOL_HALO + T, d_pool), F32)],
        compiler_params=_params("parallel", "arbitrary"), name="pool_prompt")(z, wp_all, sc)


def _pool_step_body(buf_ref, u_ref, wp_ref, sc_ref, y_ref, *, gc, pos0):
    u = u_ref[...]
    for gi, w in enumerate(POOL_WINDOWS):
        c0, c1 = gi * gc, (gi + 1) * gc
        tok = u[:, c0:c1]
        acc = tok
        for k in range(1, w):
            acc = acc + buf_ref[POOL_BUF - k, :, c0:c1]
        d = acc / float(min(w, pos0 + 1)) - tok
        y = jnp.dot(d.astype(BF16), wp_ref[gi], preferred_element_type=F32) * sc_ref[:, c0:c1]
        y_ref[:, c0:c1] = y.astype(y_ref.dtype)


def _pool_step(buf_t, u, wp, sc, *, pos0):
    Bs, d_pool = u.shape
    gc = d_pool // len(POOL_WINDOWS)
    return pl.pallas_call(
        functools.partial(_pool_step_body, gc=gc, pos0=pos0),
        out_shape=jax.ShapeDtypeStruct((Bs, d_pool), BF16),
        compiler_params=pltpu.CompilerParams(vmem_limit_bytes=VMEM_LIMIT),
        name="pool_step")(buf_t, u, wp, sc)


def _head_norm_gate(h, o, g):
    mu = jnp.mean(h, axis=-1, keepdims=True)
    hc = h - mu
    var = jnp.mean(hc * hc, axis=-1, keepdims=True)
    return hc * lax.rsqrt(var + HEAD_NORM_EPS) * g * jax.nn.sigmoid(o)


def _mlstm_prompt_body(q_ref, k_ref, v_ref, o_ref, gt_ref, bias_ref, hg_ref,
                       hn_ref, C_ref, n_ref, m_ref, *, L, dk):
    c = pl.program_id(1)

    @pl.when(c == 0)
    def _():
        C_ref[...] = jnp.zeros_like(C_ref)
        n_ref[...] = jnp.zeros_like(n_ref)
        m_ref[...] = jnp.zeros_like(m_ref)

    gt = gt_ref[...] + bias_ref[...]
    gT = gt.T
    row = lax.broadcasted_iota(jnp.int32, (L, L), 0)
    col = lax.broadcasted_iota(jnp.int32, (L, L), 1)
    causal = col <= row
    neg_inf = jnp.float32(-jnp.inf)
    k_scale = dk ** -0.5
    for h in range(MH):
        c0, c1 = h * dk, (h + 1) * dk
        ig_col = gt[:, h:h + 1]
        lf_col = _log_sigmoid(gt[:, MH + h:MH + h + 1])
        ig_row = gT[h:h + 1, :]
        lf_row = _log_sigmoid(gT[MH + h:MH + h + 1, :])
        b_col = jnp.sum(jnp.where(causal, lf_row, 0.0), axis=1, keepdims=True)
        b_row = jnp.sum(jnp.where(row <= col, lf_col, 0.0), axis=0, keepdims=True)
        a_row = ig_row - b_row
        a_col = ig_col - b_col
        cmax = jnp.max(jnp.where(causal, a_row, neg_inf), axis=1, keepdims=True)
        m_prev = m_ref[0, h, :, 0:1]
        m_t = b_col + jnp.maximum(m_prev, cmax)
        w_inter = jnp.exp(b_col + m_prev - m_t)
        dmat = jnp.exp(jnp.where(causal, a_row + b_col - m_t, neg_inf))
        q = q_ref[:, c0:c1]
        ks = k_ref[:, c0:c1] * k_scale
        qb = q.astype(BF16)
        kb = ks.astype(BF16)
        vb = v_ref[:, c0:c1].astype(BF16)
        s = lax.dot_general(qb, kb, (((1,), (1,)), ((), ())), preferred_element_type=F32) * dmat
        C_prev = C_ref[0, h]
        n_prev = n_ref[0, h]
        num = (w_inter * jnp.dot(qb, C_prev.astype(BF16), preferred_element_type=F32)
               + jnp.dot(s.astype(BF16), vb, preferred_element_type=F32))
        den = (w_inter * jnp.sum(q * n_prev, axis=1, keepdims=True)
               + jnp.sum(s, axis=1, keepdims=True))
        hh = num * (1.0 / jnp.maximum(jnp.abs(den), jnp.exp(-m_t)))
        hn_ref[:, c0:c1] = _head_norm_gate(hh, o_ref[:, c0:c1], hg_ref[:, c0:c1]).astype(hn_ref.dtype)

        m_new = m_t[L - 1:L, :]
        b_last = b_col[L - 1:L, :]
        w_old = jnp.exp(b_last + m_prev - m_new)
        kw = ks * jnp.exp(a_col + b_last - m_new)
        C_ref[0, h] = w_old * C_prev + lax.dot_general(
            kw.astype(BF16), vb, (((0,), (0,)), ((), ())), preferred_element_type=F32)
        n_ref[0, h] = w_old * n_prev + jnp.sum(kw, axis=0, keepdims=True)
        m_ref[0, h] = jnp.broadcast_to(m_new, (1, LANES))


def _mlstm_prompt(z, gates, bias, hg, *, B, S, d_pool, d_ml, L):
    nc = S // L
    dk = d_ml // MH
    blk0 = d_pool // d_ml
    assert blk0 * d_ml == d_pool

    def zspec(part):
        return pl.BlockSpec((L, d_ml), lambda b, c: (b * nc + c, blk0 + part))

    return pl.pallas_call(
        functools.partial(_mlstm_prompt_body, L=L, dk=dk),
        grid=(B, nc),
        in_specs=[zspec(0), zspec(1), zspec(2), zspec(3),
                  pl.BlockSpec((L, LANES), lambda b, c: (b * nc + c, 0)),
                  pl.BlockSpec((1, LANES), lambda b, c: (0, 0)),
                  pl.BlockSpec((1, d_ml), lambda b, c: (0, 0))],
        out_specs=[pl.BlockSpec((L, d_ml), lambda b, c: (b * nc + c, 0)),
                   pl.BlockSpec((1, MH, dk, dk), lambda b, c: (b, 0, 0, 0)),
                   pl.BlockSpec((1, MH, 1, dk), lambda b, c: (b, 0, 0, 0)),
                   pl.BlockSpec((1, MH, 1, LANES), lambda b, c: (b, 0, 0, 0))],
        out_shape=[jax.ShapeDtypeStruct((B * S, d_ml), BF16),
                   jax.ShapeDtypeStruct((B, MH, dk, dk), F32),
                   jax.ShapeDtypeStruct((B, MH, 1, dk), F32),
                   jax.ShapeDtypeStruct((B, MH, 1, LANES), F32)],
        compiler_params=_params("parallel", "arbitrary"), name="mlstm_prompt")(
            z, z, z, z, gates, bias, hg)


def _row_to_col(x_row, n):
    r = lax.broadcasted_iota(jnp.int32, (n, n), 0)
    c = lax.broadcasted_iota(jnp.int32, (n, n), 1)
    return jnp.sum(jnp.where(r == c, x_row, 0.0), axis=1, keepdims=True)


def _mlstm_step_body(z_ref, gt_ref, bias_ref, hg_ref, C0_ref, n0_ref, m0_ref,
                     hn_ref, C_ref, n_ref, m_ref, *, d_pool, d_ml, dk):
    gt = gt_ref[0] + bias_ref[...]
    k_scale = dk ** -0.5
    for h in range(MH):
        base = d_pool + h * dk
        q = z_ref[0, :, base:base + dk]
        ks = z_ref[0, :, base + d_ml:base + d_ml + dk] * k_scale
        v = z_ref[0, :, base + 2 * d_ml:base + 2 * d_ml + dk]
        o = z_ref[0, :, base + 3 * d_ml:base + 3 * d_ml + dk]
        ig = gt[:, h:h + 1]
        lf = _log_sigmoid(gt[:, MH + h:MH + h + 1])
        m_prev = m0_ref[0, h]
        m_new = jnp.maximum(m_prev + lf, ig)
        w_old = jnp.exp(lf + m_prev - m_new)
        w_s = jnp.exp(ig - m_new)
        C_new = w_old * C0_ref[0, h] + _row_to_col(ks * w_s, dk) * v
        n_new = w_old * n0_ref[0, h] + w_s * ks
        num = jnp.sum(_row_to_col(q, dk) * C_new, axis=0, keepdims=True)
        den = jnp.sum(q * n_new, axis=1, keepdims=True)
        hh = num / jnp.maximum(jnp.abs(den), jnp.exp(-m_new))
        c0 = h * dk
        hn_ref[0, :, c0:c0 + dk] = _head_norm_gate(hh, o, hg_ref[:, c0:c0 + dk]).astype(hn_ref.dtype)
        C_ref[0, h] = C_new
        n_ref[0, h] = n_new
        m_ref[0, h] = m_new


def _mlstm_step(z3, g3, bias, hg, C0, n0, m0, *, d_pool, d_ml):
    Bs = C0.shape[0]
    dk = d_ml // MH
    dz = z3.shape[-1]
    st = lambda *shape: pl.BlockSpec((1,) + shape, lambda b: (b,) + (0,) * len(shape))
    return pl.pallas_call(
        functools.partial(_mlstm_step_body, d_pool=d_pool, d_ml=d_ml, dk=dk),
        grid=(Bs,),
        in_specs=[st(1, dz), st(1, LANES),
                  pl.BlockSpec((1, LANES), lambda b: (0, 0)),
                  pl.BlockSpec((1, d_ml), lambda b: (0, 0)),
                  st(MH, dk, dk), st(MH, 1, dk), st(MH, 1, 1)],
        out_specs=[st(1, d_ml), st(MH, dk, dk), st(MH, 1, dk), st(MH, 1, 1)],
        out_shape=[jax.ShapeDtypeStruct((Bs, 1, d_ml), BF16),
                   jax.ShapeDtypeStruct((Bs, MH, dk, dk), F32),
                   jax.ShapeDtypeStruct((Bs, MH, 1, dk), F32),
                   jax.ShapeDtypeStruct((Bs, MH, 1, 1), F32)],
        compiler_params=_params("parallel"), name="mlstm_step")(z3, g3, bias, hg, C0, n0, m0)


KM_ROWS = 16


def _moba_prompt_body(q_ref, k_ref, v_ref, o_ref, kb_ref, vT_ref, km_ref, qT_ref, s_ref, pT_ref,
                      al_ref, m_ref, l_ref, acc_ref, *, nblk, hd):
    j = pl.program_id(2)
    blk = MOBA_BLOCK
    R = GROUP * blk
    c_exp = hd ** -0.5 * LOG2E
    neg_inf = jnp.float32(-jnp.inf)

    @pl.when(j == 0)
    def _():
        kb_ref[...] = k_ref[...].astype(BF16)
        vT_ref[...] = v_ref[...].T.astype(BF16)
        km = [jnp.sum(k_ref[n * blk:(n + 1) * blk, :], axis=0, keepdims=True) * (1.0 / blk)
              for n in range(nblk)]
        km.append(jnp.zeros((KM_ROWS - nblk, hd), F32))
        km_ref[...] = jnp.concatenate(km, axis=0).astype(BF16)

    for g in range(GROUP):
        qT_ref[:, g * blk:(g + 1) * blk] = q_ref[:, g * hd:(g + 1) * hd].astype(F32).T.astype(BF16)

    gate = jnp.dot(km_ref[...], qT_ref[...], preferred_element_type=F32)[0:nblk, :]
    sub = lax.broadcasted_iota(jnp.int32, (nblk, R), 0)
    past = sub < j
    sel = []
    for n in range(nblk - 1):
        gn = gate[n:n + 1, :]
        beats = ((gate > gn) | ((gate == gn) & (sub < n))) & past
        cnt = jnp.sum(beats.astype(F32), axis=0, keepdims=True)
        sel.append((cnt < MOBA_TOPK).astype(F32))

    def issue_scores(start, buf):
        kb = kb_ref[pl.ds(start, blk), :]
        for g in range(GROUP):
            cols = slice(g * blk, (g + 1) * blk)
            s_ref[buf, :, cols] = jnp.dot(kb, qT_ref[:, cols], preferred_element_type=F32)

    def flush_values(start):
        pv = jnp.dot(vT_ref[:, pl.ds(start, blk)], pT_ref[...], preferred_element_type=F32)
        acc_ref[...] = acc_ref[...] * al_ref[...] + pv

    own_start = pl.multiple_of(j * blk, blk)

    acc_ref[...] = jnp.zeros_like(acc_ref)
    al_ref[...] = jnp.ones_like(al_ref)
    for c in range(R // LANES):
        lanes = slice(c * LANES, (c + 1) * LANES)
        q0 = (c * LANES) % blk
        nk = q0 + LANES
        s = jnp.dot(kb_ref[pl.ds(own_start, nk), :], qT_ref[:, lanes], preferred_element_type=F32)
        key = lax.broadcasted_iota(jnp.int32, (nk, LANES), 0)
        qry = lax.broadcasted_iota(jnp.int32, (nk, LANES), 1) + q0
        s = jnp.where(key <= qry, s, neg_inf)
        m_new = jnp.max(s, axis=0, keepdims=True)
        p = jnp.exp2((s - m_new) * c_exp)
        m_ref[:, lanes] = m_new
        l_ref[:, lanes] = jnp.sum(p, axis=0, keepdims=True)
        pT_ref[0:nk, lanes] = p.astype(BF16)
        if nk < blk:
            pT_ref[nk:blk, lanes] = jnp.zeros((blk - nk, LANES), BF16)
    if nblk > 1:
        issue_scores(0, 0)

    for n in range(nblk - 1):
        @pl.when(n < j)
        def _(n=n):
            flush_values(own_start if n == 0 else (n - 1) * blk)
            if n + 1 < nblk - 1:
                issue_scores((n + 1) * blk, (n + 1) % 2)
            for c in range(R // LANES):
                lanes = slice(c * LANES, (c + 1) * LANES)
                s = jnp.where(sel[n][:, lanes] > 0.5, s_ref[n % 2, :, lanes], neg_inf)
                m_old = m_ref[:, lanes]
                m_new = jnp.maximum(m_old, jnp.max(s, axis=0, keepdims=True))
                alpha = jnp.exp2((m_old - m_new) * c_exp)
                p = jnp.exp2((s - m_new) * c_exp)
                m_ref[:, lanes] = m_new
                l_ref[:, lanes] = alpha * l_ref[:, lanes] + jnp.sum(p, axis=0, keepdims=True)
                al_ref[:, lanes] = alpha
                pT_ref[:, lanes] = p.astype(BF16)

    flush_values(pl.multiple_of(jnp.maximum(j - 1, 0) * blk, blk))
    inv_l = 1.0 / l_ref[...]
    for g in range(GROUP):
        cols = slice(g * blk, (g + 1) * blk)
        o_ref[:, g * hd:(g + 1) * hd] = (acc_ref[:, cols] * inv_l[:, cols]).T.astype(o_ref.dtype)


def _moba_prompt(q, k, v, *, B, S, hd):
    assert S % MOBA_BLOCK == 0 and S // MOBA_BLOCK <= KM_ROWS
    nq = S // MOBA_BLOCK
    gw = GROUP * hd
    R = GROUP * MOBA_BLOCK
    return pl.pallas_call(
        functools.partial(_moba_prompt_body, nblk=nq, hd=hd),
        grid=(B, N_KV_HEADS, nq),
        in_specs=[pl.BlockSpec((MOBA_BLOCK, gw), lambda b, h, j: (b * nq + j, h)),
                  pl.BlockSpec((S, hd), lambda b, h, j: (b, h)),
                  pl.BlockSpec((S, hd), lambda b, h, j: (b, h))],
        out_specs=pl.BlockSpec((MOBA_BLOCK, gw), lambda b, h, j: (b * nq + j, h)),
        out_shape=jax.ShapeDtypeStruct((B * S, N_HEADS * hd), BF16),
        scratch_shapes=[pltpu.VMEM((S, hd), BF16),
                        pltpu.VMEM((hd, S), BF16),
                        pltpu.VMEM((KM_ROWS, hd), BF16),
                        pltpu.VMEM((hd, R), BF16),
                        pltpu.VMEM((2, MOBA_BLOCK, R), F32),
                        pltpu.VMEM((MOBA_BLOCK, R), BF16),
                        pltpu.VMEM((1, R), F32),
                        pltpu.VMEM((1, R), F32),
                        pltpu.VMEM((1, R), F32),
                        pltpu.VMEM((hd, R), F32)],
        compiler_params=_params("parallel", "parallel", "arbitrary"), name="moba_prompt")(q, k, v)


PAGES_PER_STEP = 16
PAGES_PER_BLOCK = MOBA_BLOCK // PAGE_SIZE
TOKENS_PER_TILE = SUBLANES // N_KV_HEADS
KV_SHIFT = N_KV_HEADS.bit_length() - 1
GROUP_SHIFT = GROUP.bit_length() - 1
assert 1 << KV_SHIFT == N_KV_HEADS and 1 << GROUP_SHIFT == GROUP and TOKENS_PER_TILE * N_KV_HEADS == SUBLANES


def _kmean_body(pt_ref, *refs):
    page_refs, o_ref = refs[:PAGES_PER_STEP], refs[PAGES_PER_STEP]
    rows, hd = page_refs[0].shape[2:]
    for blk in range(PAGES_PER_STEP // PAGES_PER_BLOCK):
        acc = None
        for r in range(PAGES_PER_BLOCK):
            pg = page_refs[blk * PAGES_PER_BLOCK + r][0, 0]
            part = jnp.sum(pg.reshape(rows // SUBLANES, SUBLANES, hd), axis=0)
            acc = part if acc is None else acc + part
        per_head = acc[0:N_KV_HEADS]
        for t in range(1, TOKENS_PER_TILE):
            per_head = per_head + acc[t * N_KV_HEADS:(t + 1) * N_KV_HEADS]
        o_ref[0, 0, blk * N_KV_HEADS:(blk + 1) * N_KV_HEADS, :] = per_head * (1.0 / MOBA_BLOCK)


def _kmean_paged(cache_k4, pt_flat, *, Bs, n_pages):
    n_attn, _, rows, hd = cache_k4.shape
    ng = n_pages // PAGES_PER_STEP
    out_rows = PAGES_PER_STEP // PAGES_PER_BLOCK * N_KV_HEADS

    def page_spec(r):
        return pl.BlockSpec(
            (1, 1, rows, hd),
            lambda l, b, g, pt: (l, pt[b * n_pages + g * PAGES_PER_STEP + r], 0, 0))

    return pl.pallas_call(
        _kmean_body,
        grid_spec=pltpu.PrefetchScalarGridSpec(
            num_scalar_prefetch=1, grid=(n_attn, Bs, ng),
            in_specs=[page_spec(r) for r in range(PAGES_PER_STEP)],
            out_specs=pl.BlockSpec((1, 1, out_rows, hd), lambda l, b, g, pt: (l, b, g, 0))),
        out_shape=jax.ShapeDtypeStruct((n_attn, Bs, ng * out_rows, hd), F32),
        compiler_params=_params("parallel", "parallel", "arbitrary"), name="kmean_paged")(
            pt_flat, *([cache_k4] * PAGES_PER_STEP))


def _topk_body(q_ref, km_ref, idx_ref, *, ncol):
    qb = q_ref[0].astype(BF16)
    km = km_ref[0, 0].astype(BF16)
    gate = lax.dot_general(qb, km, (((1,), (1,)), ((), ())), preferred_element_type=F32)
    head = lax.broadcasted_iota(jnp.int32, (N_HEADS, ncol), 0)
    col = lax.broadcasted_iota(jnp.int32, (N_HEADS, ncol), 1)
    gate = jnp.where((col & (N_KV_HEADS - 1)) == (head >> GROUP_SHIFT), gate, -jnp.inf)
    out_lane = lax.broadcasted_iota(jnp.int32, (N_HEADS, LANES), 1)
    out = jnp.zeros((N_HEADS, LANES), jnp.int32)
    for r in range(MOBA_TOPK):
        mx = jnp.max(gate, axis=1, keepdims=True)
        idx = jnp.min(jnp.where(gate == mx, col, ncol), axis=1, keepdims=True)
        out = jnp.where(out_lane == r, idx >> KV_SHIFT, out)
        gate = jnp.where(col == idx, -jnp.inf, gate)
    idx_ref[0] = out


def _topk_blocks(q3, kmean, layer):
    Bs, _, hd = q3.shape
    ncol = kmean.shape[2]
    assert ncol >= MOBA_TOPK * N_KV_HEADS
    return pl.pallas_call(
        functools.partial(_topk_body, ncol=ncol),
        grid=(Bs,),
        in_specs=[pl.BlockSpec((1, N_HEADS, hd), lambda b: (b, 0, 0)),
                  pl.BlockSpec((1, 1, ncol, hd), lambda b: (layer, b, 0, 0))],
        out_specs=pl.BlockSpec((1, N_HEADS, LANES), lambda b: (b, 0, 0)),
        out_shape=jax.ShapeDtypeStruct((Bs, N_HEADS, LANES), jnp.int32),
        compiler_params=_params("parallel"), name="topk_blocks")(q3, kmean)


N_SEL_PAGES = MOBA_TOPK * PAGES_PER_BLOCK


def _moba_step_body(idx_ref, pt_ref, q_ref, kn_ref, vn_ref, *refs, hd):
    n_in = GROUP * N_SEL_PAGES
    k_refs, v_refs, o_ref = refs[:n_in], refs[n_in:2 * n_in], refs[2 * n_in]
    kv_head = pl.program_id(1)
    scale = hd ** -0.5
    rows = SAMPLE_ROWS
    ncol = k_refs[0].shape[2]
    mine = (lax.broadcasted_iota(jnp.int32, (rows, ncol), 1) & (N_KV_HEADS - 1)) == kv_head
    k_new = kn_ref[0, 0].astype(BF16).astype(F32)
    v_new = vn_ref[0, 0].astype(BF16).astype(F32)
    for g in range(GROUP):
        qb = jnp.broadcast_to(q_ref[0, g], (rows, hd)).astype(BF16)
        pages = slice(g * N_SEL_PAGES, (g + 1) * N_SEL_PAGES)
        s_pages = [jnp.where(mine,
                             lax.dot_general(qb, kr[0, 0].astype(BF16), (((1,), (1,)), ((), ())),
                                             preferred_element_type=F32) * scale,
                             -jnp.inf) for kr in k_refs[pages]]
        s_own = jnp.sum(qb.astype(F32) * k_new, axis=1, keepdims=True) * scale
        m = s_own
        for s in s_pages:
            m = jnp.maximum(m, jnp.max(s, axis=1, keepdims=True))
        p_own = jnp.exp(s_own - m)
        l = p_own
        acc = p_own * v_new
        for s, vr in zip(s_pages, v_refs[pages]):
            p = jnp.exp(s - m)
            l = l + jnp.sum(p, axis=1, keepdims=True)
            acc = acc + jnp.dot(p.astype(BF16), vr[0, 0].astype(BF16), preferred_element_type=F32)
        o_ref[0, g] = (acc / l)[0:1, :].astype(o_ref.dtype)


def _moba_step(q4, kn4, vn4, cache_k4, cache_v4, idx_flat, pt_flat, *, layer, n_pages):
    Bs, _, _, hd = q4.shape
    rows = cache_k4.shape[2]

    def page_spec(r):
        g, r = divmod(r, N_SEL_PAGES)
        sel, half = divmod(r, PAGES_PER_BLOCK)

        def index_map(b, h, idx, pt):
            blk = idx[(b * N_HEADS + h * GROUP + g) * MOBA_TOPK + sel]
            return (layer, pt[b * n_pages + blk * PAGES_PER_BLOCK + half], 0, 0)

        return pl.BlockSpec((1, 1, rows, hd), index_map)

    group_spec = pl.BlockSpec((1, GROUP, 1, hd), lambda b, h, idx, pt: (b, h, 0, 0))
    kv_spec = pl.BlockSpec((1, 1, 1, hd), lambda b, h, idx, pt: (b, h, 0, 0))
    n_in = GROUP * N_SEL_PAGES
    return pl.pallas_call(
        functools.partial(_moba_step_body, hd=hd),
        grid_spec=pltpu.PrefetchScalarGridSpec(
            num_scalar_prefetch=2, grid=(Bs, N_KV_HEADS),
            in_specs=[group_spec, kv_spec, kv_spec] + [page_spec(r) for r in range(n_in)] * 2,
            out_specs=group_spec),
        out_shape=jax.ShapeDtypeStruct((Bs, N_HEADS, 1, hd), BF16),
        compiler_params=_params("parallel", "arbitrary"), name="moba_step")(
            idx_flat, pt_flat, q4, kn4, vn4, *([cache_k4] * n_in), *([cache_v4] * n_in))


def _pad_rows(a, rows):
    return jnp.pad(a, ((0, rows - a.shape[0]),) + ((0, 0),) * (a.ndim - 1))


def kernel(x_prompt, x_sample, cache_k, cache_v, state_pool, state_C, state_n, state_m, page_table,
           w_in_mix, b_if, w_pool, pool_scale, mlstm_norm_g, w_out_mix, w_qkv, w_o,
           ln_mix_g, ln_mix_b, ln_ffn_g, ln_ffn_b, w_up, w_down):
    B, S, D = x_prompt.shape
    Bs = x_sample.shape[0]
    assert x_sample.shape[1] == 1 and Bs <= SAMPLE_ROWS
    depth = w_up.shape[0]
    d_pool = state_pool.shape[-1]
    dk = state_C.shape[-1]
    d_ml = MH * dk
    d_main = d_pool + 4 * d_ml
    assert w_in_mix.shape[2] == d_main + 2 * MH
    hd = cache_k.shape[-1]
    d_q, d_kv = N_HEADS * hd, N_KV_HEADS * hd
    n_pages = page_table.shape[1]
    past_len = n_pages * PAGE_SIZE
    assert cache_k.shape[2] == PAGE_SIZE and past_len % MOBA_BLOCK == 0
    alpha = (2 * depth) ** 0.25
    M = B * S
    L = math.gcd(S, MLSTM_CHUNK)

    w_in_b = w_in_mix.astype(BF16)
    w_g_b = jnp.pad(w_in_mix[:, :, d_main:], ((0, 0), (0, 0), (0, LANES - 2 * MH))).astype(BF16)
    bias_g = jnp.pad(b_if, ((0, 0), (0, LANES - 2 * MH)))[:, None, :]
    w_pool_b = w_pool.astype(BF16)
    w_out_b = w_out_mix.astype(BF16)
    w_qkv_b = w_qkv.astype(BF16)
    w_o_b = w_o.astype(BF16)
    w_up_b = w_up.astype(BF16)
    w_down_b = w_down.astype(BF16)

    cache_k4 = cache_k.reshape(cache_k.shape[:2] + (PAGE_SIZE * N_KV_HEADS, hd))
    cache_v4 = cache_v.reshape(cache_v.shape[:2] + (PAGE_SIZE * N_KV_HEADS, hd))
    pt_flat = page_table.reshape(-1)
    kmean_past = _kmean_paged(cache_k4, pt_flat, Bs=Bs, n_pages=n_pages)

    xp = x_prompt.reshape(M, D)
    xs = _pad_rows(x_sample.reshape(Bs, D), SAMPLE_ROWS)

    kp, vp, ks_, vs_ = [], [], [], []
    pool_p, C_p, n_p, m_p = [], [], [], []
    pool_s, C_s, n_s, m_s = [], [], [], []
    for l in range(depth):
        i = l // 2
        ln_g, ln_b = ln_mix_g[l][None, :], ln_mix_b[l][None, :]
        if l % 2 == 0:
            sc = pool_scale[i][None, :]
            hg = mlstm_norm_g[i][None, :]
            z, gates = _proj_gates(xp, w_in_b, w_g_b, i, n_main=d_main, tm=1024, tn=1024)
            y_pool = _pool_prompt(z, w_pool_b, i, sc, B=B, S=S, d_pool=d_pool, T=256)
            hn, C, n, m = _mlstm_prompt(z, gates, bias_g[i], hg, B=B, S=S, d_pool=d_pool, d_ml=d_ml, L=L)
            xp = _proj_ln([y_pool, hn], w_out_b, i, xp, ln_g, ln_b, alpha=alpha, tm=512)
            pool_p.append(z.reshape(B, S, d_main)[:, S - POOL_BUF:, :d_pool])
            C_p.append(C)
            n_p.append(n.reshape(B, MH, dk))
            m_p.append(m[:, :, 0, 0])
            zs, gs = _proj_gates(xs, w_in_b, w_g_b, i, n_main=d_main, tm=SAMPLE_ROWS, tn=1024)
            us = zs[:Bs, :d_pool]
            ys_pool = _pool_step(jnp.transpose(state_pool[i], (1, 0, 2)), us, w_pool_b[i], sc, pos0=past_len)
            hns, C, n, m = _mlstm_step(
                zs[:Bs, None, :], gs[:Bs, None, :], bias_g[i], hg, state_C[i],
                state_n[i][:, :, None, :], state_m[i][:, :, None, None], d_pool=d_pool, d_ml=d_ml)
            xs = _proj_ln([_pad_rows(ys_pool, SAMPLE_ROWS), _pad_rows(hns[:, 0, :], SAMPLE_ROWS)],
                          w_out_b, i, xs, ln_g, ln_b, alpha=alpha, tm=SAMPLE_ROWS)
            pool_s.append(jnp.concatenate([state_pool[i][:, 1:, :], us[:, None, :]], axis=1))
            C_s.append(C)
            n_s.append(n[:, :, 0, :])
            m_s.append(m[:, :, 0, 0])
        else:
            q, k, v = _proj_qkv(xp, w_qkv_b, i, d_q=d_q, d_kv=d_kv, tm=1024)
            o = _moba_prompt(q, k, v, B=B, S=S, hd=hd)
            xp = _proj_ln([o], w_o_b, i, xp, ln_g, ln_b, alpha=alpha, tm=512)
            kp.append(k.reshape(B, S, N_KV_HEADS, hd))
            vp.append(v.reshape(B, S, N_KV_HEADS, hd))
            qs, k_new, v_new = _proj_qkv(xs, w_qkv_b, i, d_q=d_q, d_kv=d_kv, tm=SAMPLE_ROWS)
            q_s = qs[:Bs].astype(F32).reshape(Bs, N_HEADS, hd)
            k_new = k_new[:Bs].reshape(Bs, N_KV_HEADS, 1, hd)
            v_new = v_new[:Bs].reshape(Bs, N_KV_HEADS, 1, hd)
            idx = _topk_blocks(q_s, kmean_past, i)[:, :, :MOBA_TOPK].reshape(-1)
            os_ = _moba_step(q_s[:, :, None, :], k_new, v_new, cache_k4, cache_v4, idx, pt_flat,
                             layer=i, n_pages=n_pages)
            xs = _proj_ln([_pad_rows(os_.reshape(Bs, d_q), SAMPLE_ROWS)], w_o_b, i, xs, ln_g, ln_b,
                          alpha=alpha, tm=SAMPLE_ROWS)
            ks_.append(k_new.reshape(Bs, 1, N_KV_HEADS, hd))
            vs_.append(v_new.reshape(Bs, 1, N_KV_HEADS, hd))
        fg, fb = ln_ffn_g[l][None, :], ln_ffn_b[l][None, :]
        xp = _ffn(xp, w_up_b, w_down_b, l, fg, fb, alpha=alpha, tm=512, tf=1024)
        xs = _ffn(xs, w_up_b, w_down_b, l, fg, fb, alpha=alpha, tm=SAMPLE_ROWS, tf=1024)

    return (xp.reshape(B, S, D), xs[:Bs].reshape(Bs, 1, D),
            jnp.stack(kp), jnp.stack(vp), jnp.stack(ks_), jnp.stack(vs_),
            jnp.stack(pool_p), jnp.stack(C_p), jnp.stack(n_p), jnp.stack(m_p),
            jnp.stack(pool_s), jnp.stack(C_s), jnp.stack(n_s), jnp.stack(m_s))
```

```python
import functools
import math

import jax
import jax.numpy as jnp
from jax import lax
from jax.experimental import pallas as pl
from jax.experimental.pallas import tpu as pltpu

F32 = jnp.float32
BF16 = jnp.bfloat16

POOL_WINDOWS = (2, 4, 8, 16)
POOL_BUF = max(POOL_WINDOWS) - 1
POOL_HALO = 16
MH = 4
N_HEADS = 16
N_KV_HEADS = 4
GROUP = N_HEADS // N_KV_HEADS
MOBA_BLOCK = 256
MOBA_TOPK = 3
PAGE_SIZE = 128
LN_EPS = 1e-5
HEAD_NORM_EPS = 1e-6
LANES = 128
SUBLANES = 8
SAMPLE_ROWS = 16
LOG2E = 1.4426950408889634

VMEM_LIMIT = 56 * 1024 * 1024
MLSTM_CHUNK = 256


def _params(*sem):
    return pltpu.CompilerParams(dimension_semantics=sem, vmem_limit_bytes=VMEM_LIMIT)


def _layer_norm(y, g, b):
    mu = jnp.mean(y, axis=-1, keepdims=True)
    yc = y - mu
    var = jnp.mean(yc * yc, axis=-1, keepdims=True)
    return yc * lax.rsqrt(var + LN_EPS) * g + b


def _log_sigmoid(x):
    return jnp.minimum(x, 0.0) - jnp.log1p(jnp.exp(-jnp.abs(x)))


def _proj_gates_body(x_ref, w_ref, wg_ref, o_ref, g_ref, xb_ref):
    @pl.when(pl.program_id(1) == 0)
    def _():
        xb = x_ref[...].astype(BF16)
        xb_ref[...] = xb
        g_ref[...] = jnp.dot(xb, wg_ref[...], preferred_element_type=F32)

    o_ref[...] = jnp.dot(xb_ref[...], w_ref[...], preferred_element_type=F32)


def _proj_gates(x, w_all, wg_all, layer, *, n_main, tm, tn):
    M, K = x.shape
    return pl.pallas_call(
        _proj_gates_body, grid=(M // tm, n_main // tn),
        in_specs=[pl.BlockSpec((tm, K), lambda i, j: (i, 0)),
                  pl.BlockSpec((None, K, tn), lambda i, j: (layer, 0, j)),
                  pl.BlockSpec((None, K, LANES), lambda i, j: (layer, 0, 0))],
        out_specs=[pl.BlockSpec((tm, tn), lambda i, j: (i, j)),
                   pl.BlockSpec((tm, LANES), lambda i, j: (i, 0))],
        out_shape=[jax.ShapeDtypeStruct((M, n_main), F32), jax.ShapeDtypeStruct((M, LANES), F32)],
        scratch_shapes=[pltpu.VMEM((tm, K), BF16)],
        compiler_params=_params("parallel", "arbitrary"), name="proj_gates")(x, w_all, wg_all)


def _proj_qkv_body(x_ref, w_ref, q_ref, k_ref, v_ref, xb_ref, *, nq_tiles):
    j = pl.program_id(1)

    @pl.when(j == 0)
    def _():
        xb_ref[...] = x_ref[...].astype(BF16)

    y = jnp.dot(xb_ref[...], w_ref[...], preferred_element_type=F32)

    @pl.when(j < nq_tiles)
    def _():
        q_ref[...] = y.astype(q_ref.dtype)

    @pl.when(j == nq_tiles)
    def _():
        d_kv = k_ref.shape[1]
        k_ref[...] = y[:, :d_kv]
        v_ref[...] = y[:, d_kv:]


def _proj_qkv(x, w_all, layer, *, d_q, d_kv, tm):
    M, K = x.shape
    tn = 2 * d_kv
    nq_tiles = d_q // tn
    assert nq_tiles * tn == d_q
    return pl.pallas_call(
        functools.partial(_proj_qkv_body, nq_tiles=nq_tiles),
        grid=(M // tm, nq_tiles + 1),
        in_specs=[pl.BlockSpec((tm, K), lambda i, j: (i, 0)),
                  pl.BlockSpec((None, K, tn), lambda i, j: (layer, 0, j))],
        out_specs=[pl.BlockSpec((tm, tn), lambda i, j: (i, jnp.minimum(j, nq_tiles - 1))),
                   pl.BlockSpec((tm, d_kv), lambda i, j: (i, 0)),
                   pl.BlockSpec((tm, d_kv), lambda i, j: (i, 0))],
        out_shape=[jax.ShapeDtypeStruct((M, d_q), BF16),
                   jax.ShapeDtypeStruct((M, d_kv), F32),
                   jax.ShapeDtypeStruct((M, d_kv), F32)],
        scratch_shapes=[pltpu.VMEM((tm, K), BF16)],
        compiler_params=_params("parallel", "arbitrary"), name="proj_qkv")(x, w_all)


def _proj_ln_body(*refs, n_a, alpha):
    a_refs = refs[:n_a]
    w_ref, res_ref, g_ref, b_ref, o_ref = refs[n_a:]
    y = alpha * res_ref[...]
    k0 = 0
    for a_ref in a_refs:
        ka = a_ref.shape[1]
        y = y + jnp.dot(a_ref[...], w_ref[k0:k0 + ka, :], preferred_element_type=F32)
        k0 += ka
    o_ref[...] = _layer_norm(y, g_ref[...], b_ref[...])


def _proj_ln(a_list, w_all, layer, res, g, b, *, alpha, tm):
    M, D = res.shape
    K = w_all.shape[1]
    in_specs = [pl.BlockSpec((tm, a.shape[1]), lambda i: (i, 0)) for a in a_list]
    in_specs += [pl.BlockSpec((None, K, D), lambda i: (layer, 0, 0)),
                 pl.BlockSpec((tm, D), lambda i: (i, 0)),
                 pl.BlockSpec((1, D), lambda i: (0, 0)),
                 pl.BlockSpec((1, D), lambda i: (0, 0))]
    return pl.pallas_call(
        functools.partial(_proj_ln_body, n_a=len(a_list), alpha=alpha),
        grid=(M // tm,), in_specs=in_specs,
        out_specs=pl.BlockSpec((tm, D), lambda i: (i, 0)),
        out_shape=jax.ShapeDtypeStruct((M, D), F32),
        compiler_params=_params("parallel"), name="proj_ln")(*a_list, w_all, res, g, b)


def _ffn_body(x_ref, wu_ref, wd_ref, g_ref, b_ref, o_ref, xb_ref, acc_ref, *, alpha):
    f = pl.program_id(1)

    @pl.when(f == 0)
    def _():
        xb_ref[...] = x_ref[...].astype(BF16)
        acc_ref[...] = jnp.zeros_like(acc_ref)

    h = jnp.dot(xb_ref[...], wu_ref[...], preferred_element_type=F32)
    h = jnp.square(jnp.maximum(h, 0.0)).astype(BF16)
    acc_ref[...] += jnp.dot(h, wd_ref[...], preferred_element_type=F32)

    @pl.when(f == pl.num_programs(1) - 1)
    def _():
        y = alpha * x_ref[...] + acc_ref[...]
        o_ref[...] = _layer_norm(y, g_ref[...], b_ref[...])


def _ffn(x, wu_all, wd_all, layer, g, b, *, alpha, tm, tf):
    M, D = x.shape
    Fd = wu_all.shape[2]
    return pl.pallas_call(
        functools.partial(_ffn_body, alpha=alpha),
        grid=(M // tm, Fd // tf),
        in_specs=[pl.BlockSpec((tm, D), lambda i, f: (i, 0)),
                  pl.BlockSpec((None, D, tf), lambda i, f: (layer, 0, f)),
                  pl.BlockSpec((None, tf, D), lambda i, f: (layer, f, 0)),
                  pl.BlockSpec((1, D), lambda i, f: (0, 0)),
                  pl.BlockSpec((1, D), lambda i, f: (0, 0))],
        out_specs=pl.BlockSpec((tm, D), lambda i, f: (i, 0)),
        out_shape=jax.ShapeDtypeStruct((M, D), F32),
        scratch_shapes=[pltpu.VMEM((tm, D), BF16), pltpu.VMEM((tm, D), F32)],
        compiler_params=_params("parallel", "arbitrary"), name="ffn")(x, wu_all, wd_all, g, b)


def _pool_prompt_body(u_ref, wp_ref, sc_ref, y_ref, ext_ref, *, T, gc):
    t = pl.program_id(1)

    @pl.when(t == 0)
    def _():
        ext_ref[0:POOL_HALO, :] = jnp.zeros((POOL_HALO, ext_ref.shape[1]), F32)

    ext_ref[POOL_HALO:POOL_HALO + T, :] = u_ref[...]
    pos = t * T + lax.broadcasted_iota(jnp.int32, (T, 1), 0)
    for gi, w in enumerate(POOL_WINDOWS):
        c0, c1 = gi * gc, (gi + 1) * gc
        tok = ext_ref[POOL_HALO:POOL_HALO + T, c0:c1]
        acc = tok
        for k in range(1, w):
            acc = acc + ext_ref[POOL_HALO - k:POOL_HALO - k + T, c0:c1]
        cnt = jnp.minimum(w, pos + 1).astype(F32)
        d = acc / cnt - tok
        y = jnp.dot(d.astype(BF16), wp_ref[gi], preferred_element_type=F32) * sc_ref[:, c0:c1]
        y_ref[:, c0:c1] = y.astype(y_ref.dtype)
    ext_ref[0:POOL_HALO, :] = ext_ref[T:T + POOL_HALO, :]


def _pool_prompt(z, wp_all, layer, sc, *, B, S, d_pool, T):
    nt = S // T
    gc = d_pool // len(POOL_WINDOWS)
    return pl.pallas_call(
        functools.partial(_pool_prompt_body, T=T, gc=gc),
        grid=(B, nt),
        in_specs=[pl.BlockSpec((T, d_pool), lambda b, t: (b * nt + t, 0)),
                  pl.BlockSpec((None,) + wp_all.shape[1:], lambda b, t: (layer, 0, 0, 0)),
                  pl.BlockSpec((1, d_pool), lambda b, t: (0, 0))],
        out_specs=pl.BlockSpec((T, d_pool), lambda b, t: (b * nt + t, 0)),
        out_shape=jax.ShapeDtypeStruct((B * S, d_pool), BF16),
        scratch_shapes=[pltpu.VMEM((POOL_HALO + T, d_pool), F32)],
        compiler_params=_params("parallel", "arbitrary"), name="pool_prompt")(z, wp_all, sc)


def _pool_step_body(buf_ref, u_ref, wp_ref, sc_ref, y_ref, *, gc, pos0):
    u = u_ref[...]
    for gi, w in enumerate(POOL_WINDOWS):
        c0, c1 = gi * gc, (gi + 1) * gc
        tok = u[:, c0:c1]
        acc = tok
        for k in range(1, w):
            acc = acc + buf_ref[POOL_BUF - k, :, c0:c1]
        d = acc / float(min(w, pos0 + 1)) - tok
        y = jnp.dot(d.astype(BF16), wp_ref[gi], preferred_element_type=F32) * sc_ref[:, c0:c1]
        y_ref[:, c0:c1] = y.astype(y_ref.dtype)


def _pool_step(buf_t, u, wp, sc, *, pos0):
    Bs, d_pool = u.shape
    gc = d_pool // len(POOL_WINDOWS)
    return pl.pallas_call(
        functools.partial(_pool_step_body, gc=gc, pos0=pos0),
        out_shape=jax.ShapeDtypeStruct((Bs, d_pool), BF16),
        compiler_params=pltpu.CompilerParams(vmem_limit_bytes=VMEM_LIMIT),
        name="pool_step")(buf_t, u, wp, sc)


def _head_norm_gate(h, o, g):
    mu = jnp.mean(h, axis=-1, keepdims=True)
    hc = h - mu
    var = jnp.mean(hc * hc, axis=-1, keepdims=True)
    return hc * lax.rsqrt(var + HEAD_NORM_EPS) * g * jax.nn.sigmoid(o)


def _mlstm_prompt_body(q_ref, k_ref, v_ref, o_ref, gt_ref, bias_ref, hg_ref,
                       hn_ref, C_ref, n_ref, m_ref, *, L, dk):
    c = pl.program_id(1)

    @pl.when(c == 0)
    def _():
        C_ref[...] = jnp.zeros_like(C_ref)
        n_ref[...] = jnp.zeros_like(n_ref)
        m_ref[...] = jnp.zeros_like(m_ref)

    gt = gt_ref[...] + bias_ref[...]
    gT = gt.T
    row = lax.broadcasted_iota(jnp.int32, (L, L), 0)
    col = lax.broadcasted_iota(jnp.int32, (L, L), 1)
    causal = col <= row
    neg_inf = jnp.float32(-jnp.inf)
    k_scale = dk ** -0.5
    for h in range(MH):
        c0, c1 = h * dk, (h + 1) * dk
        ig_col = gt[:, h:h + 1]
        lf_col = _log_sigmoid(gt[:, MH + h:MH + h + 1])
        ig_row = gT[h:h + 1, :]
        lf_row = _log_sigmoid(gT[MH + h:MH + h + 1, :])
        b_col = jnp.sum(jnp.where(causal, lf_row, 0.0), axis=1, keepdims=True)
        b_row = jnp.sum(jnp.where(row <= col, lf_col, 0.0), axis=0, keepdims=True)
        a_row = ig_row - b_row
        a_col = ig_col - b_col
        cmax = jnp.max(jnp.where(causal, a_row, neg_inf), axis=1, keepdims=True)
        m_prev = m_ref[0, h, :, 0:1]
        m_t = b_col + jnp.maximum(m_prev, cmax)
        w_inter = jnp.exp(b_col + m_prev - m_t)
        dmat = jnp.exp(jnp.where(causal, a_row + b_col - m_t, neg_inf))
        q = q_ref[:, c0:c1]
        ks = k_ref[:, c0:c1] * k_scale
        qb = q.astype(BF16)
        kb = ks.astype(BF16)
        vb = v_ref[:, c0:c1].astype(BF16)
        s = lax.dot_general(qb, kb, (((1,), (1,)), ((), ())), preferred_element_type=F32) * dmat
        C_prev = C_ref[0, h]
        n_prev = n_ref[0, h]
        num = (w_inter * jnp.dot(qb, C_prev.astype(BF16), preferred_element_type=F32)
               + jnp.dot(s.astype(BF16), vb, preferred_element_type=F32))
        den = (w_inter * jnp.sum(q * n_prev, axis=1, keepdims=True)
               + jnp.sum(s, axis=1, keepdims=True))
        hh = num * (1.0 / jnp.maximum(jnp.abs(den), jnp.exp(-m_t)))
        hn_ref[:, c0:c1] = _head_norm_gate(hh, o_ref[:, c0:c1], hg_ref[:, c0:c1]).astype(hn_ref.dtype)

        m_new = m_t[L - 1:L, :]
        b_last = b_col[L - 1:L, :]
        w_old = jnp.exp(b_last + m_prev - m_new)
        kw = ks * jnp.exp(a_col + b_last - m_new)
        C_ref[0, h] = w_old * C_prev + lax.dot_general(
            kw.astype(BF16), vb, (((0,), (0,)), ((), ())), preferred_element_type=F32)
        n_ref[0, h] = w_old * n_prev + jnp.sum(kw, axis=0, keepdims=True)
        m_ref[0, h] = jnp.broadcast_to(m_new, (1, LANES))


def _mlstm_prompt(z, gates, bias, hg, *, B, S, d_pool, d_ml, L):
    nc = S // L
    dk = d_ml // MH
    blk0 = d_pool // d_ml
    assert blk0 * d_ml == d_pool

    def zspec(part):
        return pl.BlockSpec((L, d_ml), lambda b, c: (b * nc + c, blk0 + part))

    return pl.pallas_call(
        functools.partial(_mlstm_prompt_body, L=L, dk=dk),
        grid=(B, nc),
        in_specs=[zspec(0), zspec(1), zspec(2), zspec(3),
                  pl.BlockSpec((L, LANES), lambda b, c: (b * nc + c, 0)),
                  pl.BlockSpec((1, LANES), lambda b, c: (0, 0)),
                  pl.BlockSpec((1, d_ml), lambda b, c: (0, 0))],
        out_specs=[pl.BlockSpec((L, d_ml), lambda b, c: (b * nc + c, 0)),
                   pl.BlockSpec((1, MH, dk, dk), lambda b, c: (b, 0, 0, 0)),
                   pl.BlockSpec((1, MH, 1, dk), lambda b, c: (b, 0, 0, 0)),
                   pl.BlockSpec((1, MH, 1, LANES), lambda b, c: (b, 0, 0, 0))],
        out_shape=[jax.ShapeDtypeStruct((B * S, d_ml), BF16),
                   jax.ShapeDtypeStruct((B, MH, dk, dk), F32),
                   jax.ShapeDtypeStruct((B, MH, 1, dk), F32),
                   jax.ShapeDtypeStruct((B, MH, 1, LANES), F32)],
        compiler_params=_params("parallel", "arbitrary"), name="mlstm_prompt")(
            z, z, z, z, gates, bias, hg)


def _row_to_col(x_row, n):
    r = lax.broadcasted_iota(jnp.int32, (n, n), 0)
    c = lax.broadcasted_iota(jnp.int32, (n, n), 1)
    return jnp.sum(jnp.where(r == c, x_row, 0.0), axis=1, keepdims=True)


def _mlstm_step_body(z_ref, gt_ref, bias_ref, hg_ref, C0_ref, n0_ref, m0_ref,
                     hn_ref, C_ref, n_ref, m_ref, *, d_pool, d_ml, dk):
    gt = gt_ref[0] + bias_ref[...]
    k_scale = dk ** -0.5
    for h in range(MH):
        base = d_pool + h * dk
        q = z_ref[0, :, base:base + dk]
        ks = z_ref[0, :, base + d_ml:base + d_ml + dk] * k_scale
        v = z_ref[0, :, base + 2 * d_ml:base + 2 * d_ml + dk]
        o = z_ref[0, :, base + 3 * d_ml:base + 3 * d_ml + dk]
        ig = gt[:, h:h + 1]
        lf = _log_sigmoid(gt[:, MH + h:MH + h + 1])
        m_prev = m0_ref[0, h]
        m_new = jnp.maximum(m_prev + lf, ig)
        w_old = jnp.exp(lf + m_prev - m_new)
        w_s = jnp.exp(ig - m_new)
        C_new = w_old * C0_ref[0, h] + _row_to_col(ks * w_s, dk) * v
        n_new = w_old * n0_ref[0, h] + w_s * ks
        num = jnp.sum(_row_to_col(q, dk) * C_new, axis=0, keepdims=True)
        den = jnp.sum(q * n_new, axis=1, keepdims=True)
        hh = num / jnp.maximum(jnp.abs(den), jnp.exp(-m_new))
        c0 = h * dk
        hn_ref[0, :, c0:c0 + dk] = _head_norm_gate(hh, o, hg_ref[:, c0:c0 + dk]).astype(hn_ref.dtype)
        C_ref[0, h] = C_new
        n_ref[0, h] = n_new
        m_ref[0, h] = m_new


def _mlstm_step(z3, g3, bias, hg, C0, n0, m0, *, d_pool, d_ml):
    Bs = C0.shape[0]
    dk = d_ml // MH
    dz = z3.shape[-1]
    st = lambda *shape: pl.BlockSpec((1,) + shape, lambda b: (b,) + (0,) * len(shape))
    return pl.pallas_call(
        functools.partial(_mlstm_step_body, d_pool=d_pool, d_ml=d_ml, dk=dk),
        grid=(Bs,),
        in_specs=[st(1, dz), st(1, LANES),
                  pl.BlockSpec((1, LANES), lambda b: (0, 0)),
                  pl.BlockSpec((1, d_ml), lambda b: (0, 0)),
                  st(MH, dk, dk), st(MH, 1, dk), st(MH, 1, 1)],
        out_specs=[st(1, d_ml), st(MH, dk, dk), st(MH, 1, dk), st(MH, 1, 1)],
        out_shape=[jax.ShapeDtypeStruct((Bs, 1, d_ml), BF16),
                   jax.ShapeDtypeStruct((Bs, MH, dk, dk), F32),
                   jax.ShapeDtypeStruct((Bs, MH, 1, dk), F32),
                   jax.ShapeDtypeStruct((Bs, MH, 1, 1), F32)],
        compiler_params=_params("parallel"), name="mlstm_step")(z3, g3, bias, hg, C0, n0, m0)


KM_ROWS = 16


F32_BIG = 1.0e30


def _moba_seq_body(q_ref, k_ref, v_ref, o_ref, kb_ref, vT_ref, km_ref, qT_ref, pT_ref,
                   m_ref, l_ref, acc_ref, *, nblk, hd):
    blk = MOBA_BLOCK
    R = GROUP * blk
    nchunk = R // LANES
    c_exp = hd ** -0.5 * LOG2E
    neg_inf = jnp.float32(-jnp.inf)

    kb_ref[...] = k_ref[...].astype(BF16)
    vT_ref[...] = v_ref[...].T.astype(BF16)
    km = [jnp.sum(k_ref[n * blk:(n + 1) * blk, :], axis=0, keepdims=True) * (1.0 / blk) for n in range(nblk)]
    km.append(jnp.zeros((KM_ROWS - nblk, hd), F32))
    km_ref[...] = jnp.concatenate(km, axis=0).astype(BF16)
    sub = lax.broadcasted_iota(jnp.int32, (nblk, R), 0)

    def own_block(j, buf, p_rows_ref):
        ms, ls = [], []
        for c in range(nchunk):
            lanes = slice(c * LANES, (c + 1) * LANES)
            q0 = (c * LANES) % blk
            nk = q0 + LANES
            s = jnp.dot(kb_ref[j * blk:j * blk + nk, :], qT_ref[buf, :, lanes], preferred_element_type=F32)
            key = lax.broadcasted_iota(jnp.int32, (nk, LANES), 0)
            qry = lax.broadcasted_iota(jnp.int32, (nk, LANES), 1) + q0
            s = jnp.where(key <= qry, s, neg_inf)
            m = jnp.max(s, axis=0, keepdims=True)
            p = jnp.exp2((s - m) * c_exp)
            ms.append(m)
            ls.append(jnp.sum(p, axis=0, keepdims=True))
            p_rows_ref[0:nk, lanes] = p.astype(BF16)
            if nk < blk:
                p_rows_ref[nk:blk, lanes] = jnp.zeros((blk - nk, LANES), BF16)
        return ms, ls

    def write_out(j, acc, l_all):
        inv_l = 1.0 / l_all
        for g in range(GROUP):
            cols = slice(g * blk, (g + 1) * blk)
            o_ref[j * blk:(j + 1) * blk, g * hd:(g + 1) * hd] = (acc[:, cols] * inv_l[:, cols]).T.astype(o_ref.dtype)

    for j in range(nblk):
        buf = j % 2
        for g in range(GROUP):
            qT_ref[buf, :, g * blk:(g + 1) * blk] = (
                q_ref[j * blk:(j + 1) * blk, g * hd:(g + 1) * hd].astype(F32).T.astype(BF16))

        sel = None
        if j > MOBA_TOPK:
            gate = jnp.dot(km_ref[...], qT_ref[buf], preferred_element_type=F32)[0:nblk, :]
            rows = []
            for n in range(j):
                gn = gate[n:n + 1, :]
                beats = ((gate > gn) | ((gate == gn) & (sub < n))) & (sub < j)
                rows.append((jnp.sum(beats.astype(F32), axis=0, keepdims=True) < MOBA_TOPK).astype(F32))
            sel = rows

        ms, ls = own_block(j, buf, pT_ref.at[buf, j * blk:(j + 1) * blk])
        for n in range(j):
            kb = kb_ref[n * blk:(n + 1) * blk, :]
            for g in range(GROUP):
                s2 = jnp.dot(kb, qT_ref[buf, :, g * blk:(g + 1) * blk], preferred_element_type=F32)
                for half in range(blk // LANES):
                    c = g * (blk // LANES) + half
                    lanes = slice(c * LANES, (c + 1) * LANES)
                    s = s2[:, half * LANES:(half + 1) * LANES]
                    if sel is not None:
                        s = jnp.where(sel[n][:, lanes] > 0.5, s, neg_inf)
                    p = jnp.exp2((s - ms[c]) * c_exp)
                    ls[c] = ls[c] + jnp.sum(p, axis=0, keepdims=True)
                    pT_ref[buf, n * blk:(n + 1) * blk, lanes] = p.astype(BF16)
        nkeys = (j + 1) * blk
        acc = jnp.dot(vT_ref[:, 0:nkeys], pT_ref[buf, 0:nkeys, :], preferred_element_type=F32)
        l_all = jnp.concatenate(ls, axis=1)
        write_out(j, acc, l_all)

        if j > 0:
            @pl.when(jnp.logical_not(jnp.max(l_all) < F32_BIG))
            def _(j=j, buf=buf, sel=sel):
                ms, ls = own_block(j, buf, pT_ref.at[buf, 0:blk])
                m_ref[...] = jnp.concatenate(ms, axis=1)
                l_ref[...] = jnp.concatenate(ls, axis=1)
                acc_ref[...] = jnp.dot(vT_ref[:, j * blk:(j + 1) * blk], pT_ref[buf, 0:blk, :],
                                       preferred_element_type=F32)
                selmat = None if sel is None else jnp.concatenate(
                    sel + [jnp.zeros((nblk - j, R), F32)], axis=0)

                def body(n, carry):
                    start = pl.multiple_of(n * blk, blk)
                    kb = kb_ref[pl.ds(start, blk), :]
                    vtb = vT_ref[:, pl.ds(start, blk)]
                    if selmat is not None:
                        sel_n = jnp.sum(jnp.where(sub == n, selmat, 0.0), axis=0, keepdims=True)
                    for c in range(nchunk):
                        lanes = slice(c * LANES, (c + 1) * LANES)
                        s = jnp.dot(kb, qT_ref[buf, :, lanes], preferred_element_type=F32)
                        if selmat is not None:
                            s = jnp.where(sel_n[:, lanes] > 0.5, s, neg_inf)
                        m_old = m_ref[:, lanes]
                        m_new = jnp.maximum(m_old, jnp.max(s, axis=0, keepdims=True))
                        alpha = jnp.exp2((m_old - m_new) * c_exp)
                        p = jnp.exp2((s - m_new) * c_exp)
                        m_ref[:, lanes] = m_new
                        l_ref[:, lanes] = alpha * l_ref[:, lanes] + jnp.sum(p, axis=0, keepdims=True)
                        acc_ref[:, lanes] = acc_ref[:, lanes] * alpha + jnp.dot(
                            vtb, p.astype(BF16), preferred_element_type=F32)
                    return carry

                lax.fori_loop(0, j, body, 0)
                write_out(j, acc_ref[...], l_ref[...])


def _moba_seq(q, k, v, *, B, S, hd):
    assert S % MOBA_BLOCK == 0 and S // MOBA_BLOCK <= KM_ROWS
    nq = S // MOBA_BLOCK
    gw = GROUP * hd
    R = GROUP * MOBA_BLOCK
    return pl.pallas_call(
        functools.partial(_moba_seq_body, nblk=nq, hd=hd),
        grid=(B, N_KV_HEADS),
        in_specs=[pl.BlockSpec((S, gw), lambda b, h: (b, h)),
                  pl.BlockSpec((S, hd), lambda b, h: (b, h)),
                  pl.BlockSpec((S, hd), lambda b, h: (b, h))],
        out_specs=pl.BlockSpec((S, gw), lambda b, h: (b, h)),
        out_shape=jax.ShapeDtypeStruct((B * S, N_HEADS * hd), BF16),
        scratch_shapes=[pltpu.VMEM((S, hd), BF16),
                        pltpu.VMEM((hd, S), BF16),
                        pltpu.VMEM((KM_ROWS, hd), BF16),
                        pltpu.VMEM((2, hd, R), BF16),
                        pltpu.VMEM((2, S, R), BF16),
                        pltpu.VMEM((1, R), F32),
                        pltpu.VMEM((1, R), F32),
                        pltpu.VMEM((hd, R), F32)],
        compiler_params=_params("parallel", "parallel"), name="moba_seq")(q, k, v)


PAGES_PER_STEP = 16
PAGES_PER_BLOCK = MOBA_BLOCK // PAGE_SIZE
TOKENS_PER_TILE = SUBLANES // N_KV_HEADS
KV_SHIFT = N_KV_HEADS.bit_length() - 1
GROUP_SHIFT = GROUP.bit_length() - 1
assert 1 << KV_SHIFT == N_KV_HEADS and 1 << GROUP_SHIFT == GROUP and TOKENS_PER_TILE * N_KV_HEADS == SUBLANES


def _kmean_body(pt_ref, *refs):
    page_refs, o_ref = refs[:PAGES_PER_STEP], refs[PAGES_PER_STEP]
    rows, hd = page_refs[0].shape[2:]
    for blk in range(PAGES_PER_STEP // PAGES_PER_BLOCK):
        acc = None
        for r in range(PAGES_PER_BLOCK):
            pg = page_refs[blk * PAGES_PER_BLOCK + r][0, 0]
            part = jnp.sum(pg.reshape(rows // SUBLANES, SUBLANES, hd), axis=0)
            acc = part if acc is None else acc + part
        per_head = acc[0:N_KV_HEADS]
        for t in range(1, TOKENS_PER_TILE):
            per_head = per_head + acc[t * N_KV_HEADS:(t + 1) * N_KV_HEADS]
        o_ref[0, 0, blk * N_KV_HEADS:(blk + 1) * N_KV_HEADS, :] = per_head * (1.0 / MOBA_BLOCK)


def _kmean_paged(cache_k4, pt_flat, *, Bs, n_pages):
    n_attn, _, rows, hd = cache_k4.shape
    ng = n_pages // PAGES_PER_STEP
    out_rows = PAGES_PER_STEP // PAGES_PER_BLOCK * N_KV_HEADS

    def page_spec(r):
        return pl.BlockSpec(
            (1, 1, rows, hd),
            lambda l, b, g, pt: (l, pt[b * n_pages + g * PAGES_PER_STEP + r], 0, 0))

    return pl.pallas_call(
        _kmean_body,
        grid_spec=pltpu.PrefetchScalarGridSpec(
            num_scalar_prefetch=1, grid=(n_attn, Bs, ng),
            in_specs=[page_spec(r) for r in range(PAGES_PER_STEP)],
            out_specs=pl.BlockSpec((1, 1, out_rows, hd), lambda l, b, g, pt: (l, b, g, 0))),
        out_shape=jax.ShapeDtypeStruct((n_attn, Bs, ng * out_rows, hd), F32),
        compiler_params=_params("parallel", "parallel", "arbitrary"), name="kmean_paged")(
            pt_flat, *([cache_k4] * PAGES_PER_STEP))


def _topk_body(q_ref, km_ref, idx_ref, *, ncol):
    qb = q_ref[0].astype(BF16)
    km = km_ref[0, 0].astype(BF16)
    gate = lax.dot_general(qb, km, (((1,), (1,)), ((), ())), preferred_element_type=F32)
    head = lax.broadcasted_iota(jnp.int32, (N_HEADS, ncol), 0)
    col = lax.broadcasted_iota(jnp.int32, (N_HEADS, ncol), 1)
    gate = jnp.where((col & (N_KV_HEADS - 1)) == (head >> GROUP_SHIFT), gate, -jnp.inf)
    out_lane = lax.broadcasted_iota(jnp.int32, (N_HEADS, LANES), 1)
    out = jnp.zeros((N_HEADS, LANES), jnp.int32)
    for r in range(MOBA_TOPK):
        mx = jnp.max(gate, axis=1, keepdims=True)
        idx = jnp.min(jnp.where(gate == mx, col, ncol), axis=1, keepdims=True)
        out = jnp.where(out_lane == r, idx >> KV_SHIFT, out)
        gate = jnp.where(col == idx, -jnp.inf, gate)
    idx_ref[0] = out


def _topk_blocks(q3, kmean, layer):
    Bs, _, hd = q3.shape
    ncol = kmean.shape[2]
    assert ncol >= MOBA_TOPK * N_KV_HEADS
    return pl.pallas_call(
        functools.partial(_topk_body, ncol=ncol),
        grid=(Bs,),
        in_specs=[pl.BlockSpec((1, N_HEADS, hd), lambda b: (b, 0, 0)),
                  pl.BlockSpec((1, 1, ncol, hd), lambda b: (layer, b, 0, 0))],
        out_specs=pl.BlockSpec((1, N_HEADS, LANES), lambda b: (b, 0, 0)),
        out_shape=jax.ShapeDtypeStruct((Bs, N_HEADS, LANES), jnp.int32),
        compiler_params=_params("parallel"), name="topk_blocks")(q3, kmean)


N_SEL_PAGES = MOBA_TOPK * PAGES_PER_BLOCK


def _moba_step_body(idx_ref, pt_ref, q_ref, kn_ref, vn_ref, *refs, hd):
    n_in = GROUP * N_SEL_PAGES
    k_refs, v_refs, o_ref = refs[:n_in], refs[n_in:2 * n_in], refs[2 * n_in]
    kv_head = pl.program_id(1)
    scale = hd ** -0.5
    rows = SAMPLE_ROWS
    ncol = k_refs[0].shape[2]
    mine = (lax.broadcasted_iota(jnp.int32, (rows, ncol), 1) & (N_KV_HEADS - 1)) == kv_head
    k_new = kn_ref[0, 0].astype(BF16).astype(F32)
    v_new = vn_ref[0, 0].astype(BF16).astype(F32)
    for g in range(GROUP):
        qb = jnp.broadcast_to(q_ref[0, g], (rows, hd)).astype(BF16)
        pages = slice(g * N_SEL_PAGES, (g + 1) * N_SEL_PAGES)
        s_pages = [jnp.where(mine,
                             lax.dot_general(qb, kr[0, 0].astype(BF16), (((1,), (1,)), ((), ())),
                                             preferred_element_type=F32) * scale,
                             -jnp.inf) for kr in k_refs[pages]]
        s_own = jnp.sum(qb.astype(F32) * k_new, axis=1, keepdims=True) * scale
        m = s_own
        for s in s_pages:
            m = jnp.maximum(m, jnp.max(s, axis=1, keepdims=True))
        p_own = jnp.exp(s_own - m)
        l = p_own
        acc = p_own * v_new
        for s, vr in zip(s_pages, v_refs[pages]):
            p = jnp.exp(s - m)
            l = l + jnp.sum(p, axis=1, keepdims=True)
            acc = acc + jnp.dot(p.astype(BF16), vr[0, 0].astype(BF16), preferred_element_type=F32)
        o_ref[0, g] = (acc / l)[0:1, :].astype(o_ref.dtype)


def _moba_step(q4, kn4, vn4, cache_k4, cache_v4, idx_flat, pt_flat, *, layer, n_pages):
    Bs, _, _, hd = q4.shape
    rows = cache_k4.shape[2]

    def page_spec(r):
        g, r = divmod(r, N_SEL_PAGES)
        sel, half = divmod(r, PAGES_PER_BLOCK)

        def index_map(b, h, idx, pt):
            blk = idx[(b * N_HEADS + h * GROUP + g) * MOBA_TOPK + sel]
            return (layer, pt[b * n_pages + blk * PAGES_PER_BLOCK + half], 0, 0)

        return pl.BlockSpec((1, 1, rows, hd), index_map)

    group_spec = pl.BlockSpec((1, GROUP, 1, hd), lambda b, h, idx, pt: (b, h, 0, 0))
    kv_spec = pl.BlockSpec((1, 1, 1, hd), lambda b, h, idx, pt: (b, h, 0, 0))
    n_in = GROUP * N_SEL_PAGES
    return pl.pallas_call(
        functools.partial(_moba_step_body, hd=hd),
        grid_spec=pltpu.PrefetchScalarGridSpec(
            num_scalar_prefetch=2, grid=(Bs, N_KV_HEADS),
            in_specs=[group_spec, kv_spec, kv_spec] + [page_spec(r) for r in range(n_in)] * 2,
            out_specs=group_spec),
        out_shape=jax.ShapeDtypeStruct((Bs, N_HEADS, 1, hd), BF16),
        compiler_params=_params("parallel", "arbitrary"), name="moba_step")(
            idx_flat, pt_flat, q4, kn4, vn4, *([cache_k4] * n_in), *([cache_v4] * n_in))


def _pad_rows(a, rows):
    return jnp.pad(a, ((0, rows - a.shape[0]),) + ((0, 0),) * (a.ndim - 1))


def kernel(x_prompt, x_sample, cache_k, cache_v, state_pool, state_C, state_n, state_m, page_table,
           w_in_mix, b_if, w_pool, pool_scale, mlstm_norm_g, w_out_mix, w_qkv, w_o,
           ln_mix_g, ln_mix_b, ln_ffn_g, ln_ffn_b, w_up, w_down):
    B, S, D = x_prompt.shape
    Bs = x_sample.shape[0]
    assert x_sample.shape[1] == 1 and Bs <= SAMPLE_ROWS
    depth = w_up.shape[0]
    d_pool = state_pool.shape[-1]
    dk = state_C.shape[-1]
    d_ml = MH * dk
    d_main = d_pool + 4 * d_ml
    assert w_in_mix.shape[2] == d_main + 2 * MH
    hd = cache_k.shape[-1]
    d_q, d_kv = N_HEADS * hd, N_KV_HEADS * hd
    n_pages = page_table.shape[1]
    past_len = n_pages * PAGE_SIZE
    assert cache_k.shape[2] == PAGE_SIZE and past_len % MOBA_BLOCK == 0
    alpha = (2 * depth) ** 0.25
    M = B * S
    L = math.gcd(S, MLSTM_CHUNK)

    w_in_b = w_in_mix.astype(BF16)
    w_g_b = jnp.pad(w_in_mix[:, :, d_main:], ((0, 0), (0, 0), (0, LANES - 2 * MH))).astype(BF16)
    bias_g = jnp.pad(b_if, ((0, 0), (0, LANES - 2 * MH)))[:, None, :]
    w_pool_b = w_pool.astype(BF16)
    w_out_b = w_out_mix.astype(BF16)
    w_qkv_b = w_qkv.astype(BF16)
    w_o_b = w_o.astype(BF16)
    w_up_b = w_up.astype(BF16)
    w_down_b = w_down.astype(BF16)

    cache_k4 = cache_k.reshape(cache_k.shape[:2] + (PAGE_SIZE * N_KV_HEADS, hd))
    cache_v4 = cache_v.reshape(cache_v.shape[:2] + (PAGE_SIZE * N_KV_HEADS, hd))
    pt_flat = page_table.reshape(-1)
    kmean_past = _kmean_paged(cache_k4, pt_flat, Bs=Bs, n_pages=n_pages)

    xp = x_prompt.reshape(M, D)
    xs = _pad_rows(x_sample.reshape(Bs, D), SAMPLE_ROWS)

    kp, vp, ks_, vs_ = [], [], [], []
    pool_p, C_p, n_p, m_p = [], [], [], []
    pool_s, C_s, n_s, m_s = [], [], [], []
    for l in range(depth):
        i = l // 2
        ln_g, ln_b = ln_mix_g[l][None, :], ln_mix_b[l][None, :]
        if l % 2 == 0:
            sc = pool_scale[i][None, :]
            hg = mlstm_norm_g[i][None, :]
            z, gates = _proj_gates(xp, w_in_b, w_g_b, i, n_main=d_main, tm=1024, tn=1024)
            y_pool = _pool_prompt(z, w_pool_b, i, sc, B=B, S=S, d_pool=d_pool, T=256)
            hn, C, n, m = _mlstm_prompt(z, gates, bias_g[i], hg, B=B, S=S, d_pool=d_pool, d_ml=d_ml, L=L)
            xp = _proj_ln([y_pool, hn], w_out_b, i, xp, ln_g, ln_b, alpha=alpha, tm=512)
            pool_p.append(z.reshape(B, S, d_main)[:, S - POOL_BUF:, :d_pool])
            C_p.append(C)
            n_p.append(n.reshape(B, MH, dk))
            m_p.append(m[:, :, 0, 0])
            zs, gs = _proj_gates(xs, w_in_b, w_g_b, i, n_main=d_main, tm=SAMPLE_ROWS, tn=1024)
            us = zs[:Bs, :d_pool]
            ys_pool = _pool_step(jnp.transpose(state_pool[i], (1, 0, 2)), us, w_pool_b[i], sc, pos0=past_len)
            hns, C, n, m = _mlstm_step(
                zs[:Bs, None, :], gs[:Bs, None, :], bias_g[i], hg, state_C[i],
                state_n[i][:, :, None, :], state_m[i][:, :, None, None], d_pool=d_pool, d_ml=d_ml)
            xs = _proj_ln([_pad_rows(ys_pool, SAMPLE_ROWS), _pad_rows(hns[:, 0, :], SAMPLE_ROWS)],
                          w_out_b, i, xs, ln_g, ln_b, alpha=alpha, tm=SAMPLE_ROWS)
            pool_s.append(jnp.concatenate([state_pool[i][:, 1:, :], us[:, None, :]], axis=1))
            C_s.append(C)
            n_s.append(n[:, :, 0, :])
            m_s.append(m[:, :, 0, 0])
        else:
            q, k, v = _proj_qkv(xp, w_qkv_b, i, d_q=d_q, d_kv=d_kv, tm=1024)
            o = _moba_seq(q, k, v, B=B, S=S, hd=hd)
            xp = _proj_ln([o], w_o_b, i, xp, ln_g, ln_b, alpha=alpha, tm=512)
            kp.append(k.reshape(B, S, N_KV_HEADS, hd))
            vp.append(v.reshape(B, S, N_KV_HEADS, hd))
            qs, k_new, v_new = _proj_qkv(xs, w_qkv_b, i, d_q=d_q, d_kv=d_kv, tm=SAMPLE_ROWS)
            q_s = qs[:Bs].astype(F32).reshape(Bs, N_HEADS, hd)
            k_new = k_new[:Bs].reshape(Bs, N_KV_HEADS, 1, hd)
            v_new = v_new[:Bs].reshape(Bs, N_KV_HEADS, 1, hd)
            idx = _topk_blocks(q_s, kmean_past, i)[:, :, :MOBA_TOPK].reshape(-1)
            os_ = _moba_step(q_s[:, :, None, :], k_new, v_new, cache_k4, cache_v4, idx, pt_flat,
                             layer=i, n_pages=n_pages)
            xs = _proj_ln([_pad_rows(os_.reshape(Bs, d_q), SAMPLE_ROWS)], w_o_b, i, xs, ln_g, ln_b,
                          alpha=alpha, tm=SAMPLE_ROWS)
            ks_.append(k_new.reshape(Bs, 1, N_KV_HEADS, hd))
            vs_.append(v_new.reshape(Bs, 1, N_KV_HEADS, hd))
        fg, fb = ln_ffn_g[l][None, :], ln_ffn_b[l][None, :]
        xp = _ffn(xp, w_up_b, w_down_b, l, fg, fb, alpha=alpha, tm=512, tf=1024)
        xs = _ffn(xs, w_up_b, w_down_b, l, fg, fb, alpha=alpha, tm=SAMPLE_ROWS, tf=1024)

    return (xp.reshape(B, S, D), xs[:Bs].reshape(Bs, 1, D),
            jnp.stack(kp), jnp.stack(vp), jnp.stack(ks_), jnp.stack(vs_),
            jnp.stack(pool_p), jnp.stack(C_p), jnp.stack(n_p), jnp.stack(m_p),
            jnp.stack(pool_s), jnp.stack(C_s), jnp.stack(n_s), jnp.stack(m_s))
```

```python
import functools
import math

import jax
import jax.numpy as jnp
from jax import lax
from jax.experimental import pallas as pl
from jax.experimental.pallas import tpu as pltpu

F32 = jnp.float32
BF16 = jnp.bfloat16

POOL_WINDOWS = (2, 4, 8, 16)
POOL_BUF = max(POOL_WINDOWS) - 1
POOL_HALO = 16
MH = 4
N_HEADS = 16
N_KV_HEADS = 4
GROUP = N_HEADS // N_KV_HEADS
MOBA_BLOCK = 256
MOBA_TOPK = 3
PAGE_SIZE = 128
LN_EPS = 1e-5
HEAD_NORM_EPS = 1e-6
LANES = 128
SUBLANES = 8
SAMPLE_ROWS = 16
LOG2E = 1.4426950408889634

VMEM_LIMIT = 56 * 1024 * 1024
MLSTM_CHUNK = 256
LN_SUB_ROWS = 256


def _params(*sem):
    return pltpu.CompilerParams(dimension_semantics=sem, vmem_limit_bytes=VMEM_LIMIT)


def _layer_norm(y, g, b):
    mu = jnp.mean(y, axis=-1, keepdims=True)
    yc = y - mu
    var = jnp.mean(yc * yc, axis=-1, keepdims=True)
    return yc * lax.rsqrt(var + LN_EPS) * g + b


def _log_sigmoid(x):
    return jnp.minimum(x, 0.0) - jnp.log1p(jnp.exp(-jnp.abs(x)))


def _proj_gates_body(x_ref, w_ref, wg_ref, o_ref, g_ref, xb_ref):
    @pl.when(pl.program_id(1) == 0)
    def _():
        xb = x_ref[...].astype(BF16)
        xb_ref[...] = xb
        g_ref[...] = jnp.dot(xb, wg_ref[...], preferred_element_type=F32)

    o_ref[...] = jnp.dot(xb_ref[...], w_ref[...], preferred_element_type=F32)


def _proj_gates(x, w_all, layer, wg_all, g_layer, *, n_main, tm, tn):
    M, K = x.shape
    return pl.pallas_call(
        _proj_gates_body, grid=(M // tm, n_main // tn),
        in_specs=[pl.BlockSpec((tm, K), lambda i, j: (i, 0)),
                  pl.BlockSpec((None, K, tn), lambda i, j: (layer, 0, j)),
                  pl.BlockSpec((None, K, LANES), lambda i, j: (g_layer, 0, 0))],
        out_specs=[pl.BlockSpec((tm, tn), lambda i, j: (i, j)),
                   pl.BlockSpec((tm, LANES), lambda i, j: (i, 0))],
        out_shape=[jax.ShapeDtypeStruct((M, n_main), F32), jax.ShapeDtypeStruct((M, LANES), F32)],
        scratch_shapes=[pltpu.VMEM((tm, K), BF16)],
        compiler_params=_params("parallel", "arbitrary"), name="proj_gates")(x, w_all, wg_all)


def _proj_qkv_body(x_ref, w_ref, q_ref, k_ref, v_ref, xb_ref, *, nq_tiles):
    j = pl.program_id(1)

    @pl.when(j == 0)
    def _():
        xb_ref[...] = x_ref[...].astype(BF16)

    y = jnp.dot(xb_ref[...], w_ref[...], preferred_element_type=F32)

    @pl.when(j < nq_tiles)
    def _():
        q_ref[...] = y.astype(q_ref.dtype)

    @pl.when(j == nq_tiles)
    def _():
        d_kv = k_ref.shape[1]
        k_ref[...] = y[:, :d_kv]
        v_ref[...] = y[:, d_kv:]


def _proj_qkv(x, w_all, layer, *, d_q, d_kv, tm):
    M, K = x.shape
    tn = 2 * d_kv
    nq_tiles = d_q // tn
    assert nq_tiles * tn == d_q
    return pl.pallas_call(
        functools.partial(_proj_qkv_body, nq_tiles=nq_tiles),
        grid=(M // tm, nq_tiles + 1),
        in_specs=[pl.BlockSpec((tm, K), lambda i, j: (i, 0)),
                  pl.BlockSpec((None, K, tn), lambda i, j: (layer, 0, j))],
        out_specs=[pl.BlockSpec((tm, tn), lambda i, j: (i, jnp.minimum(j, nq_tiles - 1))),
                   pl.BlockSpec((tm, d_kv), lambda i, j: (i, 0)),
                   pl.BlockSpec((tm, d_kv), lambda i, j: (i, 0))],
        out_shape=[jax.ShapeDtypeStruct((M, d_q), BF16),
                   jax.ShapeDtypeStruct((M, d_kv), F32),
                   jax.ShapeDtypeStruct((M, d_kv), F32)],
        scratch_shapes=[pltpu.VMEM((tm, K), BF16)],
        compiler_params=_params("parallel", "arbitrary"), name="proj_qkv")(x, w_all)


def _proj_ln_body(*refs, n_a, alpha):
    a_refs = refs[:n_a]
    w_ref, res_ref, g_ref, b_ref, o_ref = refs[n_a:]
    tm = res_ref.shape[0]
    sub = min(tm, LN_SUB_ROWS)
    for r0 in range(0, tm, sub):
        rows = slice(r0, r0 + sub)
        y = alpha * res_ref[rows, :]
        k0 = 0
        for a_ref in a_refs:
            ka = a_ref.shape[1]
            y = y + jnp.dot(a_ref[rows, :], w_ref[k0:k0 + ka, :], preferred_element_type=F32)
            k0 += ka
        o_ref[rows, :] = _layer_norm(y, g_ref[...], b_ref[...])


def _proj_ln(a_list, w_all, layer, res, g, b, *, alpha, tm):
    M, D = res.shape
    K = w_all.shape[1]
    in_specs = [pl.BlockSpec((tm, a.shape[1]), lambda i: (i, 0)) for a in a_list]
    in_specs += [pl.BlockSpec((None, K, D), lambda i: (layer, 0, 0)),
                 pl.BlockSpec((tm, D), lambda i: (i, 0)),
                 pl.BlockSpec((1, D), lambda i: (0, 0)),
                 pl.BlockSpec((1, D), lambda i: (0, 0))]
    return pl.pallas_call(
        functools.partial(_proj_ln_body, n_a=len(a_list), alpha=alpha),
        grid=(M // tm,), in_specs=in_specs,
        out_specs=pl.BlockSpec((tm, D), lambda i: (i, 0)),
        out_shape=jax.ShapeDtypeStruct((M, D), F32),
        compiler_params=_params("parallel"), name="proj_ln")(*a_list, w_all, res, g, b)


def _ffn_body(x_ref, wu_ref, wd_ref, g_ref, b_ref, o_ref, xb_ref, acc_ref, *, alpha):
    f = pl.program_id(1)

    @pl.when(f == 0)
    def _():
        xb_ref[...] = x_ref[...].astype(BF16)
        acc_ref[...] = jnp.zeros_like(acc_ref)

    h = jnp.dot(xb_ref[...], wu_ref[...], preferred_element_type=F32)
    h = jnp.square(jnp.maximum(h, 0.0)).astype(BF16)
    acc_ref[...] += jnp.dot(h, wd_ref[...], preferred_element_type=F32)

    @pl.when(f == pl.num_programs(1) - 1)
    def _():
        y = alpha * x_ref[...] + acc_ref[...]
        o_ref[...] = _layer_norm(y, g_ref[...], b_ref[...])


def _ffn(x, wu_all, wd_all, layer, g, b, *, alpha, tm, tf):
    M, D = x.shape
    Fd = wu_all.shape[2]
    return pl.pallas_call(
        functools.partial(_ffn_body, alpha=alpha),
        grid=(M // tm, Fd // tf),
        in_specs=[pl.BlockSpec((tm, D), lambda i, f: (i, 0)),
                  pl.BlockSpec((None, D, tf), lambda i, f: (layer, 0, f)),
                  pl.BlockSpec((None, tf, D), lambda i, f: (layer, f, 0)),
                  pl.BlockSpec((1, D), lambda i, f: (0, 0)),
                  pl.BlockSpec((1, D), lambda i, f: (0, 0))],
        out_specs=pl.BlockSpec((tm, D), lambda i, f: (i, 0)),
        out_shape=jax.ShapeDtypeStruct((M, D), F32),
        scratch_shapes=[pltpu.VMEM((tm, D), BF16), pltpu.VMEM((tm, D), F32)],
        compiler_params=_params("parallel", "arbitrary"), name="ffn")(x, wu_all, wd_all, g, b)


def _cast_tile(w_ref, wb_ref):
    wb = w_ref[...].astype(BF16)
    wb_ref[...] = wb
    return wb


def _proj_gates_step_body(x_ref, w_ref, wg_ref, o_ref, g_ref, wb_ref):
    xb = x_ref[...].astype(BF16)

    @pl.when(pl.program_id(0) == 0)
    def _():
        g_ref[...] = jnp.dot(xb, wg_ref[...], preferred_element_type=F32)

    o_ref[...] = jnp.dot(xb, _cast_tile(w_ref, wb_ref), preferred_element_type=F32)


def _proj_gates_step(x, w_all, wg_all, layer, *, n_main, tn):
    R, K = x.shape
    return pl.pallas_call(
        _proj_gates_step_body, grid=(n_main // tn,),
        in_specs=[pl.BlockSpec((R, K), lambda j: (0, 0)),
                  pl.BlockSpec((None, K, tn), lambda j: (layer, 0, j)),
                  pl.BlockSpec((None, K, LANES), lambda j: (layer, 0, 0))],
        out_specs=[pl.BlockSpec((R, tn), lambda j: (0, j)),
                   pl.BlockSpec((R, LANES), lambda j: (0, 0)),
                   pl.BlockSpec((K, tn), lambda j: (0, j))],
        out_shape=[jax.ShapeDtypeStruct((R, n_main), F32), jax.ShapeDtypeStruct((R, LANES), F32),
                   jax.ShapeDtypeStruct((K, n_main), BF16)],
        compiler_params=_params("arbitrary"), name="proj_gates_step")(x, w_all, wg_all)


def _proj_qkv_step_body(x_ref, w_ref, q_ref, k_ref, v_ref, wb_ref, *, nq_tiles):
    j = pl.program_id(0)
    y = jnp.dot(x_ref[...].astype(BF16), _cast_tile(w_ref, wb_ref), preferred_element_type=F32)

    @pl.when(j < nq_tiles)
    def _():
        q_ref[...] = y.astype(q_ref.dtype)

    @pl.when(j == nq_tiles)
    def _():
        d_kv = k_ref.shape[1]
        k_ref[...] = y[:, :d_kv]
        v_ref[...] = y[:, d_kv:]


def _proj_qkv_step(x, w_all, layer, *, d_q, d_kv):
    R, K = x.shape
    tn = 2 * d_kv
    nq_tiles = d_q // tn
    assert nq_tiles * tn == d_q
    return pl.pallas_call(
        functools.partial(_proj_qkv_step_body, nq_tiles=nq_tiles),
        grid=(nq_tiles + 1,),
        in_specs=[pl.BlockSpec((R, K), lambda j: (0, 0)),
                  pl.BlockSpec((None, K, tn), lambda j: (layer, 0, j))],
        out_specs=[pl.BlockSpec((R, tn), lambda j: (0, jnp.minimum(j, nq_tiles - 1))),
                   pl.BlockSpec((R, d_kv), lambda j: (0, 0)),
                   pl.BlockSpec((R, d_kv), lambda j: (0, 0)),
                   pl.BlockSpec((K, tn), lambda j: (0, j))],
        out_shape=[jax.ShapeDtypeStruct((R, d_q), BF16),
                   jax.ShapeDtypeStruct((R, d_kv), F32),
                   jax.ShapeDtypeStruct((R, d_kv), F32),
                   jax.ShapeDtypeStruct((K, d_q + 2 * d_kv), BF16)],
        compiler_params=_params("arbitrary"), name="proj_qkv_step")(x, w_all)


def _proj_ln_step_body(a_ref, w_ref, res_ref, g_ref, b_ref, o_ref, wb_ref, acc_ref, *, alpha):
    k = pl.program_id(0)

    @pl.when(k == 0)
    def _():
        acc_ref[...] = alpha * res_ref[...]

    acc_ref[...] += jnp.dot(a_ref[...], _cast_tile(w_ref, wb_ref), preferred_element_type=F32)

    @pl.when(k == pl.num_programs(0) - 1)
    def _():
        o_ref[...] = _layer_norm(acc_ref[...], g_ref[...], b_ref[...])


def _proj_ln_step(a, w_all, layer, res, g, b, *, alpha, tk):
    R, D = res.shape
    K = w_all.shape[1]
    return pl.pallas_call(
        functools.partial(_proj_ln_step_body, alpha=alpha),
        grid=(K // tk,),
        in_specs=[pl.BlockSpec((R, tk), lambda k: (0, k)),
                  pl.BlockSpec((None, tk, D), lambda k: (layer, k, 0)),
                  pl.BlockSpec((R, D), lambda k: (0, 0)),
                  pl.BlockSpec((1, D), lambda k: (0, 0)),
                  pl.BlockSpec((1, D), lambda k: (0, 0))],
        out_specs=[pl.BlockSpec((R, D), lambda k: (0, 0)),
                   pl.BlockSpec((tk, D), lambda k: (k, 0))],
        out_shape=[jax.ShapeDtypeStruct((R, D), F32), jax.ShapeDtypeStruct((K, D), BF16)],
        scratch_shapes=[pltpu.VMEM((R, D), F32)],
        compiler_params=_params("arbitrary"), name="proj_ln_step")(a, w_all, res, g, b)


def _ffn_step_body(x_ref, wu_ref, wd_ref, g_ref, b_ref, o_ref, wub_ref, wdb_ref, acc_ref, *, alpha):
    f = pl.program_id(0)

    @pl.when(f == 0)
    def _():
        acc_ref[...] = jnp.zeros_like(acc_ref)

    h = jnp.dot(x_ref[...].astype(BF16), _cast_tile(wu_ref, wub_ref), preferred_element_type=F32)
    h = jnp.square(jnp.maximum(h, 0.0)).astype(BF16)
    acc_ref[...] += jnp.dot(h, _cast_tile(wd_ref, wdb_ref), preferred_element_type=F32)

    @pl.when(f == pl.num_programs(0) - 1)
    def _():
        y = alpha * x_ref[...] + acc_ref[...]
        o_ref[...] = _layer_norm(y, g_ref[...], b_ref[...])


def _ffn_step(x, wu_all, wd_all, layer, g, b, *, alpha, tf):
    R, D = x.shape
    Fd = wu_all.shape[2]
    return pl.pallas_call(
        functools.partial(_ffn_step_body, alpha=alpha),
        grid=(Fd // tf,),
        in_specs=[pl.BlockSpec((R, D), lambda f: (0, 0)),
                  pl.BlockSpec((None, D, tf), lambda f: (layer, 0, f)),
                  pl.BlockSpec((None, tf, D), lambda f: (layer, f, 0)),
                  pl.BlockSpec((1, D), lambda f: (0, 0)),
                  pl.BlockSpec((1, D), lambda f: (0, 0))],
        out_specs=[pl.BlockSpec((R, D), lambda f: (0, 0)),
                   pl.BlockSpec((D, tf), lambda f: (0, f)),
                   pl.BlockSpec((tf, D), lambda f: (f, 0))],
        out_shape=[jax.ShapeDtypeStruct((R, D), F32),
                   jax.ShapeDtypeStruct((D, Fd), BF16),
                   jax.ShapeDtypeStruct((Fd, D), BF16)],
        scratch_shapes=[pltpu.VMEM((R, D), F32)],
        compiler_params=_params("arbitrary"), name="ffn_step")(x, wu_all, wd_all, g, b)


def _pool_prompt_body(u_ref, wp_ref, sc_ref, y_ref, ext_ref, *, T, gc):
    t = pl.program_id(1)

    @pl.when(t == 0)
    def _():
        ext_ref[0:POOL_HALO, :] = jnp.zeros((POOL_HALO, ext_ref.shape[1]), F32)

    ext_ref[POOL_HALO:POOL_HALO + T, :] = u_ref[...]
    pos = t * T + lax.broadcasted_iota(jnp.int32, (T, 1), 0)
    for gi, w in enumerate(POOL_WINDOWS):
        c0, c1 = gi * gc, (gi + 1) * gc
        tok = ext_ref[POOL_HALO:POOL_HALO + T, c0:c1]
        acc = tok
        for k in range(1, w):
            acc = acc + ext_ref[POOL_HALO - k:POOL_HALO - k + T, c0:c1]
        cnt = jnp.minimum(w, pos + 1).astype(F32)
        d = acc / cnt - tok
        y = jnp.dot(d.astype(BF16), wp_ref[gi], preferred_element_type=F32) * sc_ref[:, c0:c1]
        y_ref[:, c0:c1] = y.astype(y_ref.dtype)
    ext_ref[0:POOL_HALO, :] = ext_ref[T:T + POOL_HALO, :]


def _pool_prompt(z, wp_all, layer, sc, *, B, S, d_pool, T):
    nt = S // T
    gc = d_pool // len(POOL_WINDOWS)
    return pl.pallas_call(
        functools.partial(_pool_prompt_body, T=T, gc=gc),
        grid=(B, nt),
        in_specs=[pl.BlockSpec((T, d_pool), lambda b, t: (b * nt + t, 0)),
                  pl.BlockSpec((None,) + wp_all.shape[1:], lambda b, t: (layer, 0, 0, 0)),
                  pl.BlockSpec((1, d_pool), lambda b, t: (0, 0))],
        out_specs=pl.BlockSpec((T, d_pool), lambda b, t: (b * nt + t, 0)),
        out_shape=jax.ShapeDtypeStruct((B * S, d_pool), BF16),
        scratch_shapes=[pltpu.VMEM((POOL_HALO + T, d_pool), F32)],
        compiler_params=_params("parallel", "arbitrary"), name="pool_prompt")(z, wp_all, sc)


def _pool_step_body(buf_ref, u_ref, wp_ref, sc_ref, y_ref, *, gc, pos0):
    u = u_ref[...]
    for gi, w in enumerate(POOL_WINDOWS):
        c0, c1 = gi * gc, (gi + 1) * gc
        tok = u[:, c0:c1]
        acc = tok
        for k in range(1, w):
            acc = acc + buf_ref[POOL_BUF - k, :, c0:c1]
        d = acc / float(min(w, pos0 + 1)) - tok
        y = jnp.dot(d.astype(BF16), wp_ref[gi], preferred_element_type=F32) * sc_ref[:, c0:c1]
        y_ref[:, c0:c1] = y.astype(y_ref.dtype)


def _pool_step(buf_t, u, wp, sc, *, pos0):
    Bs, d_pool = u.shape
    gc = d_pool // len(POOL_WINDOWS)
    return pl.pallas_call(
        functools.partial(_pool_step_body, gc=gc, pos0=pos0),
        out_shape=jax.ShapeDtypeStruct((Bs, d_pool), BF16),
        compiler_params=pltpu.CompilerParams(vmem_limit_bytes=VMEM_LIMIT),
        name="pool_step")(buf_t, u, wp, sc)


def _head_norm_gate(h, o, g):
    mu = jnp.mean(h, axis=-1, keepdims=True)
    hc = h - mu
    var = jnp.mean(hc * hc, axis=-1, keepdims=True)
    return hc * lax.rsqrt(var + HEAD_NORM_EPS) * g * jax.nn.sigmoid(o)


def _mlstm_prompt_body(q_ref, k_ref, v_ref, o_ref, gt_ref, bias_ref, hg_ref,
                       hn_ref, C_ref, n_ref, m_ref, *, L, dk):
    c = pl.program_id(1)

    @pl.when(c == 0)
    def _():
        C_ref[...] = jnp.zeros_like(C_ref)
        n_ref[...] = jnp.zeros_like(n_ref)
        m_ref[...] = jnp.zeros_like(m_ref)

    gt = gt_ref[...] + bias_ref[...]
    gT = gt.T
    row = lax.broadcasted_iota(jnp.int32, (L, L), 0)
    col = lax.broadcasted_iota(jnp.int32, (L, L), 1)
    causal = col <= row
    neg_inf = jnp.float32(-jnp.inf)
    k_scale = dk ** -0.5
    for h in range(MH):
        c0, c1 = h * dk, (h + 1) * dk
        ig_col = gt[:, h:h + 1]
        lf_col = _log_sigmoid(gt[:, MH + h:MH + h + 1])
        ig_row = gT[h:h + 1, :]
        lf_row = _log_sigmoid(gT[MH + h:MH + h + 1, :])
        b_col = jnp.sum(jnp.where(causal, lf_row, 0.0), axis=1, keepdims=True)
        b_row = jnp.sum(jnp.where(row <= col, lf_col, 0.0), axis=0, keepdims=True)
        a_row = ig_row - b_row
        a_col = ig_col - b_col
        cmax = jnp.max(jnp.where(causal, a_row, neg_inf), axis=1, keepdims=True)
        m_prev = m_ref[0, h, :, 0:1]
        m_t = b_col + jnp.maximum(m_prev, cmax)
        w_inter = jnp.exp(b_col + m_prev - m_t)
        dmat = jnp.exp(jnp.where(causal, a_row + b_col - m_t, neg_inf))
        q = q_ref[:, c0:c1]
        ks = k_ref[:, c0:c1] * k_scale
        qb = q.astype(BF16)
        kb = ks.astype(BF16)
        vb = v_ref[:, c0:c1].astype(BF16)
        s = lax.dot_general(qb, kb, (((1,), (1,)), ((), ())), preferred_element_type=F32) * dmat
        C_prev = C_ref[0, h]
        n_prev = n_ref[0, h]
        num = (w_inter * jnp.dot(qb, C_prev.astype(BF16), preferred_element_type=F32)
               + jnp.dot(s.astype(BF16), vb, preferred_element_type=F32))
        den = (w_inter * jnp.sum(q * n_prev, axis=1, keepdims=True)
               + jnp.sum(s, axis=1, keepdims=True))
        hh = num * (1.0 / jnp.maximum(jnp.abs(den), jnp.exp(-m_t)))
        hn_ref[:, c0:c1] = _head_norm_gate(hh, o_ref[:, c0:c1], hg_ref[:, c0:c1]).astype(hn_ref.dtype)

        m_new = m_t[L - 1:L, :]
        b_last = b_col[L - 1:L, :]
        w_old = jnp.exp(b_last + m_prev - m_new)
        kw = ks * jnp.exp(a_col + b_last - m_new)
        C_ref[0, h] = w_old * C_prev + lax.dot_general(
            kw.astype(BF16), vb, (((0,), (0,)), ((), ())), preferred_element_type=F32)
        n_ref[0, h] = w_old * n_prev + jnp.sum(kw, axis=0, keepdims=True)
        m_ref[0, h] = jnp.broadcast_to(m_new, (1, LANES))


def _mlstm_prompt(z, gates, bias, hg, *, B, S, d_pool, d_ml, L):
    nc = S // L
    dk = d_ml // MH
    blk0 = d_pool // d_ml
    assert blk0 * d_ml == d_pool

    def zspec(part):
        return pl.BlockSpec((L, d_ml), lambda b, c: (b * nc + c, blk0 + part))

    return pl.pallas_call(
        functools.partial(_mlstm_prompt_body, L=L, dk=dk),
        grid=(B, nc),
        in_specs=[zspec(0), zspec(1), zspec(2), zspec(3),
                  pl.BlockSpec((L, LANES), lambda b, c: (b * nc + c, 0)),
                  pl.BlockSpec((1, LANES), lambda b, c: (0, 0)),
                  pl.BlockSpec((1, d_ml), lambda b, c: (0, 0))],
        out_specs=[pl.BlockSpec((L, d_ml), lambda b, c: (b * nc + c, 0)),
                   pl.BlockSpec((1, MH, dk, dk), lambda b, c: (b, 0, 0, 0)),
                   pl.BlockSpec((1, MH, 1, dk), lambda b, c: (b, 0, 0, 0)),
                   pl.BlockSpec((1, MH, 1, LANES), lambda b, c: (b, 0, 0, 0))],
        out_shape=[jax.ShapeDtypeStruct((B * S, d_ml), BF16),
                   jax.ShapeDtypeStruct((B, MH, dk, dk), F32),
                   jax.ShapeDtypeStruct((B, MH, 1, dk), F32),
                   jax.ShapeDtypeStruct((B, MH, 1, LANES), F32)],
        compiler_params=_params("parallel", "arbitrary"), name="mlstm_prompt")(
            z, z, z, z, gates, bias, hg)


def _row_to_col(x_row, n):
    r = lax.broadcasted_iota(jnp.int32, (n, n), 0)
    c = lax.broadcasted_iota(jnp.int32, (n, n), 1)
    return jnp.sum(jnp.where(r == c, x_row, 0.0), axis=1, keepdims=True)


def _mlstm_step_body(z_ref, gt_ref, bias_ref, hg_ref, C0_ref, n0_ref, m0_ref,
                     hn_ref, C_ref, n_ref, m_ref, *, d_pool, d_ml, dk):
    gt = gt_ref[0] + bias_ref[...]
    k_scale = dk ** -0.5
    for h in range(MH):
        base = d_pool + h * dk
        q = z_ref[0, :, base:base + dk]
        ks = z_ref[0, :, base + d_ml:base + d_ml + dk] * k_scale
        v = z_ref[0, :, base + 2 * d_ml:base + 2 * d_ml + dk]
        o = z_ref[0, :, base + 3 * d_ml:base + 3 * d_ml + dk]
        ig = gt[:, h:h + 1]
        lf = _log_sigmoid(gt[:, MH + h:MH + h + 1])
        m_prev = m0_ref[0, h]
        m_new = jnp.maximum(m_prev + lf, ig)
        w_old = jnp.exp(lf + m_prev - m_new)
        w_s = jnp.exp(ig - m_new)
        C_new = w_old * C0_ref[0, h] + _row_to_col(ks * w_s, dk) * v
        n_new = w_old * n0_ref[0, h] + w_s * ks
        num = jnp.sum(_row_to_col(q, dk) * C_new, axis=0, keepdims=True)
        den = jnp.sum(q * n_new, axis=1, keepdims=True)
        hh = num / jnp.maximum(jnp.abs(den), jnp.exp(-m_new))
        c0 = h * dk
        hn_ref[0, :, c0:c0 + dk] = _head_norm_gate(hh, o, hg_ref[:, c0:c0 + dk]).astype(hn_ref.dtype)
        C_ref[0, h] = C_new
        n_ref[0, h] = n_new
        m_ref[0, h] = m_new


def _mlstm_step(z3, g3, bias, hg, C0, n0, m0, *, d_pool, d_ml):
    Bs = C0.shape[0]
    dk = d_ml // MH
    dz = z3.shape[-1]
    st = lambda *shape: pl.BlockSpec((1,) + shape, lambda b: (b,) + (0,) * len(shape))
    return pl.pallas_call(
        functools.partial(_mlstm_step_body, d_pool=d_pool, d_ml=d_ml, dk=dk),
        grid=(Bs,),
        in_specs=[st(1, dz), st(1, LANES),
                  pl.BlockSpec((1, LANES), lambda b: (0, 0)),
                  pl.BlockSpec((1, d_ml), lambda b: (0, 0)),
                  st(MH, dk, dk), st(MH, 1, dk), st(MH, 1, 1)],
        out_specs=[st(1, d_ml), st(MH, dk, dk), st(MH, 1, dk), st(MH, 1, 1)],
        out_shape=[jax.ShapeDtypeStruct((Bs, 1, d_ml), BF16),
                   jax.ShapeDtypeStruct((Bs, MH, dk, dk), F32),
                   jax.ShapeDtypeStruct((Bs, MH, 1, dk), F32),
                   jax.ShapeDtypeStruct((Bs, MH, 1, 1), F32)],
        compiler_params=_params("parallel"), name="mlstm_step")(z3, g3, bias, hg, C0, n0, m0)


KM_ROWS = 16


F32_BIG = 1.0e30


def _moba_seq_body(q_ref, k_ref, v_ref, o_ref, kb_ref, vT_ref, km_ref, qT_ref, pT_ref,
                   m_ref, l_ref, acc_ref, *, nblk, hd):
    blk = MOBA_BLOCK
    R = GROUP * blk
    nchunk = R // LANES
    c_exp = hd ** -0.5 * LOG2E
    neg_inf = jnp.float32(-jnp.inf)

    kb_ref[...] = k_ref[...].astype(BF16)
    vT_ref[...] = v_ref[...].T.astype(BF16)
    km = [jnp.sum(k_ref[n * blk:(n + 1) * blk, :], axis=0, keepdims=True) * (1.0 / blk) for n in range(nblk)]
    km.append(jnp.zeros((KM_ROWS - nblk, hd), F32))
    km_ref[...] = jnp.concatenate(km, axis=0).astype(BF16)
    sub = lax.broadcasted_iota(jnp.int32, (nblk, R), 0)

    def own_block(j, buf, p_rows_ref):
        ms, ls = [], []
        for c in range(nchunk):
            lanes = slice(c * LANES, (c + 1) * LANES)
            q0 = (c * LANES) % blk
            nk = q0 + LANES
            s = jnp.dot(kb_ref[j * blk:j * blk + nk, :], qT_ref[buf, :, lanes], preferred_element_type=F32)
            key = lax.broadcasted_iota(jnp.int32, (nk, LANES), 0)
            qry = lax.broadcasted_iota(jnp.int32, (nk, LANES), 1) + q0
            s = jnp.where(key <= qry, s, neg_inf)
            m = jnp.max(s, axis=0, keepdims=True)
            p = jnp.exp2((s - m) * c_exp)
            ms.append(m)
            ls.append(jnp.sum(p, axis=0, keepdims=True))
            p_rows_ref[0:nk, lanes] = p.astype(BF16)
            if nk < blk:
                p_rows_ref[nk:blk, lanes] = jnp.zeros((blk - nk, LANES), BF16)
        return ms, ls

    def write_out(j, acc, l_all):
        inv_l = 1.0 / l_all
        for g in range(GROUP):
            cols = slice(g * blk, (g + 1) * blk)
            o_ref[j * blk:(j + 1) * blk, g * hd:(g + 1) * hd] = (acc[:, cols] * inv_l[:, cols]).T.astype(o_ref.dtype)

    for j in range(nblk):
        buf = j % 2
        for g in range(GROUP):
            qT_ref[buf, :, g * blk:(g + 1) * blk] = (
                q_ref[j * blk:(j + 1) * blk, g * hd:(g + 1) * hd].astype(F32).T.astype(BF16))

        sel = None
        if j > MOBA_TOPK:
            gate = jnp.dot(km_ref[...], qT_ref[buf], preferred_element_type=F32)[0:nblk, :]
            rows = []
            for n in range(j):
                gn = gate[n:n + 1, :]
                beats = ((gate > gn) | ((gate == gn) & (sub < n))) & (sub < j)
                rows.append((jnp.sum(beats.astype(F32), axis=0, keepdims=True) < MOBA_TOPK).astype(F32))
            sel = rows

        ms, ls = own_block(j, buf, pT_ref.at[buf, j * blk:(j + 1) * blk])
        for n in range(j):
            kb = kb_ref[n * blk:(n + 1) * blk, :]
            for g in range(GROUP):
                s2 = jnp.dot(kb, qT_ref[buf, :, g * blk:(g + 1) * blk], preferred_element_type=F32)
                for half in range(blk // LANES):
                    c = g * (blk // LANES) + half
                    lanes = slice(c * LANES, (c + 1) * LANES)
                    s = s2[:, half * LANES:(half + 1) * LANES]
                    if sel is not None:
                        s = jnp.where(sel[n][:, lanes] > 0.5, s, neg_inf)
                    p = jnp.exp2((s - ms[c]) * c_exp)
                    ls[c] = ls[c] + jnp.sum(p, axis=0, keepdims=True)
                    pT_ref[buf, n * blk:(n + 1) * blk, lanes] = p.astype(BF16)
        nkeys = (j + 1) * blk
        acc = jnp.dot(vT_ref[:, 0:nkeys], pT_ref[buf, 0:nkeys, :], preferred_element_type=F32)
        l_all = jnp.concatenate(ls, axis=1)
        write_out(j, acc, l_all)

        if j > 0:
            @pl.when(jnp.logical_not(jnp.max(l_all) < F32_BIG))
            def _(j=j, buf=buf, sel=sel):
                ms, ls = own_block(j, buf, pT_ref.at[buf, 0:blk])
                m_ref[...] = jnp.concatenate(ms, axis=1)
                l_ref[...] = jnp.concatenate(ls, axis=1)
                acc_ref[...] = jnp.dot(vT_ref[:, j * blk:(j + 1) * blk], pT_ref[buf, 0:blk, :],
                                       preferred_element_type=F32)
                selmat = None if sel is None else jnp.concatenate(
                    sel + [jnp.zeros((nblk - j, R), F32)], axis=0)

                def body(n, carry):
                    start = pl.multiple_of(n * blk, blk)
                    kb = kb_ref[pl.ds(start, blk), :]
                    vtb = vT_ref[:, pl.ds(start, blk)]
                    if selmat is not None:
                        sel_n = jnp.sum(jnp.where(sub == n, selmat, 0.0), axis=0, keepdims=True)
                    for c in range(nchunk):
                        lanes = slice(c * LANES, (c + 1) * LANES)
                        s = jnp.dot(kb, qT_ref[buf, :, lanes], preferred_element_type=F32)
                        if selmat is not None:
                            s = jnp.where(sel_n[:, lanes] > 0.5, s, neg_inf)
                        m_old = m_ref[:, lanes]
                        m_new = jnp.maximum(m_old, jnp.max(s, axis=0, keepdims=True))
                        alpha = jnp.exp2((m_old - m_new) * c_exp)
                        p = jnp.exp2((s - m_new) * c_exp)
                        m_ref[:, lanes] = m_new
                        l_ref[:, lanes] = alpha * l_ref[:, lanes] + jnp.sum(p, axis=0, keepdims=True)
                        acc_ref[:, lanes] = acc_ref[:, lanes] * alpha + jnp.dot(
                            vtb, p.astype(BF16), preferred_element_type=F32)
                    return carry

                lax.fori_loop(0, j, body, 0)
                write_out(j, acc_ref[...], l_ref[...])


def _moba_seq(q, k, v, *, B, S, hd):
    assert S % MOBA_BLOCK == 0 and S // MOBA_BLOCK <= KM_ROWS
    nq = S // MOBA_BLOCK
    gw = GROUP * hd
    R = GROUP * MOBA_BLOCK
    return pl.pallas_call(
        functools.partial(_moba_seq_body, nblk=nq, hd=hd),
        grid=(B, N_KV_HEADS),
        in_specs=[pl.BlockSpec((S, gw), lambda b, h: (b, h)),
                  pl.BlockSpec((S, hd), lambda b, h: (b, h)),
                  pl.BlockSpec((S, hd), lambda b, h: (b, h))],
        out_specs=pl.BlockSpec((S, gw), lambda b, h: (b, h)),
        out_shape=jax.ShapeDtypeStruct((B * S, N_HEADS * hd), BF16),
        scratch_shapes=[pltpu.VMEM((S, hd), BF16),
                        pltpu.VMEM((hd, S), BF16),
                        pltpu.VMEM((KM_ROWS, hd), BF16),
                        pltpu.VMEM((2, hd, R), BF16),
                        pltpu.VMEM((2, S, R), BF16),
                        pltpu.VMEM((1, R), F32),
                        pltpu.VMEM((1, R), F32),
                        pltpu.VMEM((hd, R), F32)],
        compiler_params=_params("parallel", "parallel"), name="moba_seq")(q, k, v)


PAGES_PER_STEP = 16
PAGES_PER_BLOCK = MOBA_BLOCK // PAGE_SIZE
TOKENS_PER_TILE = SUBLANES // N_KV_HEADS
KV_SHIFT = N_KV_HEADS.bit_length() - 1
GROUP_SHIFT = GROUP.bit_length() - 1
assert 1 << KV_SHIFT == N_KV_HEADS and 1 << GROUP_SHIFT == GROUP and TOKENS_PER_TILE * N_KV_HEADS == SUBLANES


def _kmean_body(pt_ref, *refs):
    page_refs, o_ref = refs[:PAGES_PER_STEP], refs[PAGES_PER_STEP]
    rows, hd = page_refs[0].shape[2:]
    for blk in range(PAGES_PER_STEP // PAGES_PER_BLOCK):
        acc = None
        for r in range(PAGES_PER_BLOCK):
            pg = page_refs[blk * PAGES_PER_BLOCK + r][0, 0]
            part = jnp.sum(pg.reshape(rows // SUBLANES, SUBLANES, hd), axis=0)
            acc = part if acc is None else acc + part
        per_head = acc[0:N_KV_HEADS]
        for t in range(1, TOKENS_PER_TILE):
            per_head = per_head + acc[t * N_KV_HEADS:(t + 1) * N_KV_HEADS]
        o_ref[0, 0, blk * N_KV_HEADS:(blk + 1) * N_KV_HEADS, :] = per_head * (1.0 / MOBA_BLOCK)


def _kmean_paged(cache_k4, pt_flat, *, Bs, n_pages):
    n_attn, _, rows, hd = cache_k4.shape
    ng = n_pages // PAGES_PER_STEP
    out_rows = PAGES_PER_STEP // PAGES_PER_BLOCK * N_KV_HEADS

    def page_spec(r):
        return pl.BlockSpec(
            (1, 1, rows, hd),
            lambda l, b, g, pt: (l, pt[b * n_pages + g * PAGES_PER_STEP + r], 0, 0))

    return pl.pallas_call(
        _kmean_body,
        grid_spec=pltpu.PrefetchScalarGridSpec(
            num_scalar_prefetch=1, grid=(n_attn, Bs, ng),
            in_specs=[page_spec(r) for r in range(PAGES_PER_STEP)],
            out_specs=pl.BlockSpec((1, 1, out_rows, hd), lambda l, b, g, pt: (l, b, g, 0))),
        out_shape=jax.ShapeDtypeStruct((n_attn, Bs, ng * out_rows, hd), F32),
        compiler_params=_params("parallel", "parallel", "arbitrary"), name="kmean_paged")(
            pt_flat, *([cache_k4] * PAGES_PER_STEP))


def _topk_body(q_ref, km_ref, idx_ref, *, ncol):
    qb = q_ref[0].astype(BF16)
    km = km_ref[0, 0].astype(BF16)
    gate = lax.dot_general(qb, km, (((1,), (1,)), ((), ())), preferred_element_type=F32)
    head = lax.broadcasted_iota(jnp.int32, (N_HEADS, ncol), 0)
    col = lax.broadcasted_iota(jnp.int32, (N_HEADS, ncol), 1)
    gate = jnp.where((col & (N_KV_HEADS - 1)) == (head >> GROUP_SHIFT), gate, -jnp.inf)
    out_lane = lax.broadcasted_iota(jnp.int32, (N_HEADS, LANES), 1)
    out = jnp.zeros((N_HEADS, LANES), jnp.int32)
    for r in range(MOBA_TOPK):
        mx = jnp.max(gate, axis=1, keepdims=True)
        idx = jnp.min(jnp.where(gate == mx, col, ncol), axis=1, keepdims=True)
        out = jnp.where(out_lane == r, idx >> KV_SHIFT, out)
        gate = jnp.where(col == idx, -jnp.inf, gate)
    idx_ref[0] = out


def _topk_blocks(q3, kmean, layer):
    Bs, _, hd = q3.shape
    ncol = kmean.shape[2]
    assert ncol >= MOBA_TOPK * N_KV_HEADS
    return pl.pallas_call(
        functools.partial(_topk_body, ncol=ncol),
        grid=(Bs,),
        in_specs=[pl.BlockSpec((1, N_HEADS, hd), lambda b: (b, 0, 0)),
                  pl.BlockSpec((1, 1, ncol, hd), lambda b: (layer, b, 0, 0))],
        out_specs=pl.BlockSpec((1, N_HEADS, LANES), lambda b: (b, 0, 0)),
        out_shape=jax.ShapeDtypeStruct((Bs, N_HEADS, LANES), jnp.int32),
        compiler_params=_params("parallel"), name="topk_blocks")(q3, kmean)


N_SEL_PAGES = MOBA_TOPK * PAGES_PER_BLOCK


def _moba_step_body(idx_ref, pt_ref, q_ref, kn_ref, vn_ref, *refs, hd):
    n_in = GROUP * N_SEL_PAGES
    k_refs, v_refs, o_ref = refs[:n_in], refs[n_in:2 * n_in], refs[2 * n_in]
    kv_head = pl.program_id(1)
    scale = hd ** -0.5
    rows = SAMPLE_ROWS
    ncol = k_refs[0].shape[2]
    mine = (lax.broadcasted_iota(jnp.int32, (rows, ncol), 1) & (N_KV_HEADS - 1)) == kv_head
    k_new = kn_ref[0, 0].astype(BF16).astype(F32)
    v_new = vn_ref[0, 0].astype(BF16).astype(F32)
    for g in range(GROUP):
        qb = jnp.broadcast_to(q_ref[0, g], (rows, hd)).astype(BF16)
        pages = slice(g * N_SEL_PAGES, (g + 1) * N_SEL_PAGES)
        s_pages = [jnp.where(mine,
                             lax.dot_general(qb, kr[0, 0].astype(BF16), (((1,), (1,)), ((), ())),
                                             preferred_element_type=F32) * scale,
                             -jnp.inf) for kr in k_refs[pages]]
        s_own = jnp.sum(qb.astype(F32) * k_new, axis=1, keepdims=True) * scale
        m = s_own
        for s in s_pages:
            m = jnp.maximum(m, jnp.max(s, axis=1, keepdims=True))
        p_own = jnp.exp(s_own - m)
        l = p_own
        acc = p_own * v_new
        for s, vr in zip(s_pages, v_refs[pages]):
            p = jnp.exp(s - m)
            l = l + jnp.sum(p, axis=1, keepdims=True)
            acc = acc + jnp.dot(p.astype(BF16), vr[0, 0].astype(BF16), preferred_element_type=F32)
        o_ref[0, g] = (acc / l)[0:1, :].astype(o_ref.dtype)


def _moba_step(q4, kn4, vn4, cache_k4, cache_v4, idx_flat, pt_flat, *, layer, n_pages):
    Bs, _, _, hd = q4.shape
    rows = cache_k4.shape[2]

    def page_spec(r):
        g, r = divmod(r, N_SEL_PAGES)
        sel, half = divmod(r, PAGES_PER_BLOCK)

        def index_map(b, h, idx, pt):
            blk = idx[(b * N_HEADS + h * GROUP + g) * MOBA_TOPK + sel]
            return (layer, pt[b * n_pages + blk * PAGES_PER_BLOCK + half], 0, 0)

        return pl.BlockSpec((1, 1, rows, hd), index_map)

    group_spec = pl.BlockSpec((1, GROUP, 1, hd), lambda b, h, idx, pt: (b, h, 0, 0))
    kv_spec = pl.BlockSpec((1, 1, 1, hd), lambda b, h, idx, pt: (b, h, 0, 0))
    n_in = GROUP * N_SEL_PAGES
    return pl.pallas_call(
        functools.partial(_moba_step_body, hd=hd),
        grid_spec=pltpu.PrefetchScalarGridSpec(
            num_scalar_prefetch=2, grid=(Bs, N_KV_HEADS),
            in_specs=[group_spec, kv_spec, kv_spec] + [page_spec(r) for r in range(n_in)] * 2,
            out_specs=group_spec),
        out_shape=jax.ShapeDtypeStruct((Bs, N_HEADS, 1, hd), BF16),
        compiler_params=_params("parallel", "arbitrary"), name="moba_step")(
            idx_flat, pt_flat, q4, kn4, vn4, *([cache_k4] * n_in), *([cache_v4] * n_in))


def _pad_rows(a, rows):
    return jnp.pad(a, ((0, rows - a.shape[0]),) + ((0, 0),) * (a.ndim - 1))


def kernel(x_prompt, x_sample, cache_k, cache_v, state_pool, state_C, state_n, state_m, page_table,
           w_in_mix, b_if, w_pool, pool_scale, mlstm_norm_g, w_out_mix, w_qkv, w_o,
           ln_mix_g, ln_mix_b, ln_ffn_g, ln_ffn_b, w_up, w_down):
    B, S, D = x_prompt.shape
    Bs = x_sample.shape[0]
    assert x_sample.shape[1] == 1 and Bs <= SAMPLE_ROWS
    depth = w_up.shape[0]
    d_pool = state_pool.shape[-1]
    dk = state_C.shape[-1]
    d_ml = MH * dk
    d_main = d_pool + 4 * d_ml
    assert w_in_mix.shape[2] == d_main + 2 * MH
    hd = cache_k.shape[-1]
    d_q, d_kv = N_HEADS * hd, N_KV_HEADS * hd
    n_pages = page_table.shape[1]
    past_len = n_pages * PAGE_SIZE
    assert cache_k.shape[2] == PAGE_SIZE and past_len % MOBA_BLOCK == 0
    alpha = (2 * depth) ** 0.25
    M = B * S
    L = math.gcd(S, MLSTM_CHUNK)

    w_g_b = jnp.pad(w_in_mix[:, :, d_main:], ((0, 0), (0, 0), (0, LANES - 2 * MH))).astype(BF16)
    bias_g = jnp.pad(b_if, ((0, 0), (0, LANES - 2 * MH)))[:, None, :]
    w_pool_b = w_pool.astype(BF16)

    cache_k4 = cache_k.reshape(cache_k.shape[:2] + (PAGE_SIZE * N_KV_HEADS, hd))
    cache_v4 = cache_v.reshape(cache_v.shape[:2] + (PAGE_SIZE * N_KV_HEADS, hd))
    pt_flat = page_table.reshape(-1)
    kmean_past = _kmean_paged(cache_k4, pt_flat, Bs=Bs, n_pages=n_pages)

    xs = _pad_rows(x_sample.reshape(Bs, D), SAMPLE_ROWS)
    ks_, vs_, pool_s, C_s, n_s, m_s = [], [], [], [], [], []
    w_mix_b, w_post_b, w_up_b, w_down_b = [], [], [], []
    for l in range(depth):
        i = l // 2
        ln_g, ln_b = ln_mix_g[l][None, :], ln_mix_b[l][None, :]
        if l % 2 == 0:
            sc = pool_scale[i][None, :]
            hg = mlstm_norm_g[i][None, :]
            zs, gs, wb = _proj_gates_step(xs, w_in_mix, w_g_b, i, n_main=d_main, tn=1024)
            us = zs[:Bs, :d_pool]
            ys_pool = _pool_step(jnp.transpose(state_pool[i], (1, 0, 2)), us, w_pool_b[i], sc, pos0=past_len)
            hns, C, n, m = _mlstm_step(
                zs[:Bs, None, :], gs[:Bs, None, :], bias_g[i], hg, state_C[i],
                state_n[i][:, :, None, :], state_m[i][:, :, None, None], d_pool=d_pool, d_ml=d_ml)
            a = _pad_rows(jnp.concatenate([ys_pool, hns[:, 0, :]], axis=1), SAMPLE_ROWS)
            xs, wpb = _proj_ln_step(a, w_out_mix, i, xs, ln_g, ln_b, alpha=alpha, tk=512)
            pool_s.append(jnp.concatenate([state_pool[i][:, 1:, :], us[:, None, :]], axis=1))
            C_s.append(C)
            n_s.append(n[:, :, 0, :])
            m_s.append(m[:, :, 0, 0])
        else:
            qs, k_new, v_new, wb = _proj_qkv_step(xs, w_qkv, i, d_q=d_q, d_kv=d_kv)
            q_s = qs[:Bs].astype(F32).reshape(Bs, N_HEADS, hd)
            k_new = k_new[:Bs].reshape(Bs, N_KV_HEADS, 1, hd)
            v_new = v_new[:Bs].reshape(Bs, N_KV_HEADS, 1, hd)
            idx = _topk_blocks(q_s, kmean_past, i)[:, :, :MOBA_TOPK].reshape(-1)
            os_ = _moba_step(q_s[:, :, None, :], k_new, v_new, cache_k4, cache_v4, idx, pt_flat,
                             layer=i, n_pages=n_pages)
            xs, wpb = _proj_ln_step(_pad_rows(os_.reshape(Bs, d_q), SAMPLE_ROWS), w_o, i, xs, ln_g, ln_b,
                                    alpha=alpha, tk=512)
            ks_.append(k_new.reshape(Bs, 1, N_KV_HEADS, hd))
            vs_.append(v_new.reshape(Bs, 1, N_KV_HEADS, hd))
        xs, wub, wdb = _ffn_step(xs, w_up, w_down, l, ln_ffn_g[l][None, :], ln_ffn_b[l][None, :],
                                 alpha=alpha, tf=512)
        w_mix_b.append(wb[None])
        w_post_b.append(wpb[None])
        w_up_b.append(wub[None])
        w_down_b.append(wdb[None])

    xp = x_prompt.reshape(M, D)
    kp, vp, pool_p, C_p, n_p, m_p = [], [], [], [], [], []
    for l in range(depth):
        i = l // 2
        ln_g, ln_b = ln_mix_g[l][None, :], ln_mix_b[l][None, :]
        if l % 2 == 0:
            sc = pool_scale[i][None, :]
            hg = mlstm_norm_g[i][None, :]
            z, gates = _proj_gates(xp, w_mix_b[l], 0, w_g_b, i, n_main=d_main, tm=1024, tn=1024)
            y_pool = _pool_prompt(z, w_pool_b, i, sc, B=B, S=S, d_pool=d_pool, T=256)
            hn, C, n, m = _mlstm_prompt(z, gates, bias_g[i], hg, B=B, S=S, d_pool=d_pool, d_ml=d_ml, L=L)
            xp = _proj_ln([y_pool, hn], w_post_b[l], 0, xp, ln_g, ln_b, alpha=alpha, tm=512)
            pool_p.append(z.reshape(B, S, d_main)[:, S - POOL_BUF:, :d_pool])
            C_p.append(C)
            n_p.append(n.reshape(B, MH, dk))
            m_p.append(m[:, :, 0, 0])
        else:
            q, k, v = _proj_qkv(xp, w_mix_b[l], 0, d_q=d_q, d_kv=d_kv, tm=1024)
            o = _moba_seq(q, k, v, B=B, S=S, hd=hd)
            xp = _proj_ln([o], w_post_b[l], 0, xp, ln_g, ln_b, alpha=alpha, tm=512)
            kp.append(k.reshape(B, S, N_KV_HEADS, hd))
            vp.append(v.reshape(B, S, N_KV_HEADS, hd))
        xp = _ffn(xp, w_up_b[l], w_down_b[l], 0, ln_ffn_g[l][None, :], ln_ffn_b[l][None, :],
                  alpha=alpha, tm=512, tf=1024)

    return (xp.reshape(B, S, D), xs[:Bs].reshape(Bs, 1, D),
            jnp.stack(kp), jnp.stack(vp), jnp.stack(ks_), jnp.stack(vs_),
            jnp.stack(pool_p), jnp.stack(C_p), jnp.stack(n_p), jnp.stack(m_p),
            jnp.stack(pool_s), jnp.stack(C_s), jnp.stack(n_s), jnp.stack(m_s))
```

```python
import functools
import math

import jax
import jax.numpy as jnp
from jax import lax
from jax.experimental import pallas as pl
from jax.experimental.pallas import tpu as pltpu

F32 = jnp.float32
BF16 = jnp.bfloat16

POOL_WINDOWS = (2, 4, 8, 16)
POOL_BUF = max(POOL_WINDOWS) - 1
POOL_HALO = 16
MH = 4
N_HEADS = 16
N_KV_HEADS = 4
GROUP = N_HEADS // N_KV_HEADS
MOBA_BLOCK = 256
MOBA_TOPK = 3
PAGE_SIZE = 128
LN_EPS = 1e-5
HEAD_NORM_EPS = 1e-6
LANES = 128
SUBLANES = 8
SAMPLE_ROWS = 16
LOG2E = 1.4426950408889634

VMEM_LIMIT = 56 * 1024 * 1024
MLSTM_CHUNK = 256
LN_SUB_ROWS = 256


def _params(*sem):
    return pltpu.CompilerParams(dimension_semantics=sem, vmem_limit_bytes=VMEM_LIMIT)


def _layer_norm(y, g, b):
    mu = jnp.mean(y, axis=-1, keepdims=True)
    yc = y - mu
    var = jnp.mean(yc * yc, axis=-1, keepdims=True)
    return yc * lax.rsqrt(var + LN_EPS) * g + b


def _log_sigmoid(x):
    return jnp.minimum(x, 0.0) - jnp.log1p(jnp.exp(-jnp.abs(x)))


def _proj_gates_body(x_ref, w_ref, wg_ref, o_ref, g_ref, xb_ref):
    @pl.when(pl.program_id(1) == 0)
    def _():
        xb = x_ref[...].astype(BF16)
        xb_ref[...] = xb
        g_ref[...] = jnp.dot(xb, wg_ref[...], preferred_element_type=F32)

    o_ref[...] = jnp.dot(xb_ref[...], w_ref[...], preferred_element_type=F32)


def _proj_gates(x, w_all, layer, wg_all, g_layer, *, n_main, tm, tn):
    M, K = x.shape
    return pl.pallas_call(
        _proj_gates_body, grid=(M // tm, n_main // tn),
        in_specs=[pl.BlockSpec((tm, K), lambda i, j: (i, 0)),
                  pl.BlockSpec((None, K, tn), lambda i, j: (layer, 0, j)),
                  pl.BlockSpec((None, K, LANES), lambda i, j: (g_layer, 0, 0))],
        out_specs=[pl.BlockSpec((tm, tn), lambda i, j: (i, j)),
                   pl.BlockSpec((tm, LANES), lambda i, j: (i, 0))],
        out_shape=[jax.ShapeDtypeStruct((M, n_main), F32), jax.ShapeDtypeStruct((M, LANES), F32)],
        scratch_shapes=[pltpu.VMEM((tm, K), BF16)],
        compiler_params=_params("parallel", "arbitrary"), name="proj_gates")(x, w_all, wg_all)


def _proj_qkv_body(x_ref, w_ref, q_ref, k_ref, v_ref, xb_ref, *, nq_tiles):
    j = pl.program_id(1)

    @pl.when(j == 0)
    def _():
        xb_ref[...] = x_ref[...].astype(BF16)

    y = jnp.dot(xb_ref[...], w_ref[...], preferred_element_type=F32)

    @pl.when(j < nq_tiles)
    def _():
        q_ref[...] = y.astype(q_ref.dtype)

    @pl.when(j == nq_tiles)
    def _():
        d_kv = k_ref.shape[1]
        k_ref[...] = y[:, :d_kv]
        v_ref[...] = y[:, d_kv:]


def _proj_qkv(x, w_all, layer, *, d_q, d_kv, tm):
    M, K = x.shape
    tn = 2 * d_kv
    nq_tiles = d_q // tn
    assert nq_tiles * tn == d_q
    return pl.pallas_call(
        functools.partial(_proj_qkv_body, nq_tiles=nq_tiles),
        grid=(M // tm, nq_tiles + 1),
        in_specs=[pl.BlockSpec((tm, K), lambda i, j: (i, 0)),
                  pl.BlockSpec((None, K, tn), lambda i, j: (layer, 0, j))],
        out_specs=[pl.BlockSpec((tm, tn), lambda i, j: (i, jnp.minimum(j, nq_tiles - 1))),
                   pl.BlockSpec((tm, d_kv), lambda i, j: (i, 0)),
                   pl.BlockSpec((tm, d_kv), lambda i, j: (i, 0))],
        out_shape=[jax.ShapeDtypeStruct((M, d_q), BF16),
                   jax.ShapeDtypeStruct((M, d_kv), F32),
                   jax.ShapeDtypeStruct((M, d_kv), F32)],
        scratch_shapes=[pltpu.VMEM((tm, K), BF16)],
        compiler_params=_params("parallel", "arbitrary"), name="proj_qkv")(x, w_all)


def _proj_ln_body(*refs, n_a, alpha):
    a_refs = refs[:n_a]
    w_ref, res_ref, g_ref, b_ref, o_ref = refs[n_a:]
    tm = res_ref.shape[0]
    sub = min(tm, LN_SUB_ROWS)
    for r0 in range(0, tm, sub):
        rows = slice(r0, r0 + sub)
        y = alpha * res_ref[rows, :]
        k0 = 0
        for a_ref in a_refs:
            ka = a_ref.shape[1]
            y = y + jnp.dot(a_ref[rows, :], w_ref[k0:k0 + ka, :], preferred_element_type=F32)
            k0 += ka
        o_ref[rows, :] = _layer_norm(y, g_ref[...], b_ref[...])


def _proj_ln(a_list, w_all, layer, res, g, b, *, alpha, tm):
    M, D = res.shape
    K = w_all.shape[1]
    in_specs = [pl.BlockSpec((tm, a.shape[1]), lambda i: (i, 0)) for a in a_list]
    in_specs += [pl.BlockSpec((None, K, D), lambda i: (layer, 0, 0)),
                 pl.BlockSpec((tm, D), lambda i: (i, 0)),
                 pl.BlockSpec((1, D), lambda i: (0, 0)),
                 pl.BlockSpec((1, D), lambda i: (0, 0))]
    return pl.pallas_call(
        functools.partial(_proj_ln_body, n_a=len(a_list), alpha=alpha),
        grid=(M // tm,), in_specs=in_specs,
        out_specs=pl.BlockSpec((tm, D), lambda i: (i, 0)),
        out_shape=jax.ShapeDtypeStruct((M, D), F32),
        compiler_params=_params("parallel"), name="proj_ln")(*a_list, w_all, res, g, b)


def _ffn_body(*refs, alpha, n_page_refs):
    if n_page_refs:
        refs = refs[1:]
    x_ref, wu_ref, wd_ref, g_ref, b_ref = refs[:5]
    page_refs = refs[5:5 + n_page_refs]
    if n_page_refs:
        o_ref, km_ref, xb_ref, acc_ref = refs[5 + n_page_refs:]
    else:
        o_ref, xb_ref, acc_ref = refs[5:]
    f = pl.program_id(1)
    last = pl.num_programs(1) - 1

    def hidden(rows):
        h = jnp.dot(xb_ref[rows, :], wu_ref[...], preferred_element_type=F32)
        h = jnp.square(jnp.maximum(h, 0.0)).astype(BF16)
        return jnp.dot(h, wd_ref[...], preferred_element_type=F32)

    def page_stream():
        if n_page_refs:
            _block_mean_keys(page_refs, km_ref)

    @pl.when(f == 0)
    def _():
        xb_ref[...] = x_ref[...].astype(BF16)
        acc_ref[...] = jnp.zeros_like(acc_ref)

    @pl.when(f < last)
    def _():
        acc_ref[...] += hidden(slice(None))
        page_stream()

    @pl.when(f == last)
    def _():
        page_stream()
        tm = x_ref.shape[0]
        sub = min(tm, LN_SUB_ROWS)
        for r0 in range(0, tm, sub):
            rows = slice(r0, r0 + sub)
            y = alpha * x_ref[rows, :] + (acc_ref[rows, :] + hidden(rows))
            o_ref[rows, :] = _layer_norm(y, g_ref[...], b_ref[...])


def _ffn(x, wu_all, wd_all, layer, g, b, *, alpha, tm, tf, paged=None):
    M, D = x.shape
    Fd = wu_all.shape[2]
    grid = (M // tm, Fd // tf)
    in_specs = [pl.BlockSpec((tm, D), lambda i, f, *_: (i, 0)),
                pl.BlockSpec((None, D, tf), lambda i, f, *_: (layer, 0, f)),
                pl.BlockSpec((None, tf, D), lambda i, f, *_: (layer, f, 0)),
                pl.BlockSpec((1, D), lambda i, f, *_: (0, 0)),
                pl.BlockSpec((1, D), lambda i, f, *_: (0, 0))]
    o_spec = pl.BlockSpec((tm, D), lambda i, f, *_: (i, 0))
    o_shape = jax.ShapeDtypeStruct((M, D), F32)
    scratch = [pltpu.VMEM((tm, D), BF16), pltpu.VMEM((tm, D), F32)]
    if paged is None:
        return pl.pallas_call(
            functools.partial(_ffn_body, alpha=alpha, n_page_refs=0),
            grid=grid, in_specs=in_specs, out_specs=o_spec, out_shape=o_shape, scratch_shapes=scratch,
            compiler_params=_params("parallel", "arbitrary"), name="ffn")(x, wu_all, wd_all, g, b)

    cache_k4, pt_flat, Bs, n_pages = paged
    n_attn, _, rows, hd = cache_k4.shape
    assert grid == (n_attn * Bs, n_pages // PAGES_PER_STEP)
    out_rows = PAGES_PER_STEP // PAGES_PER_BLOCK * N_KV_HEADS

    def page_spec(r):
        return pl.BlockSpec(
            (1, 1, rows, hd),
            lambda i, f, pt: (i // Bs, pt[(i % Bs) * n_pages + f * PAGES_PER_STEP + r], 0, 0))

    return pl.pallas_call(
        functools.partial(_ffn_body, alpha=alpha, n_page_refs=PAGES_PER_STEP),
        grid_spec=pltpu.PrefetchScalarGridSpec(
            num_scalar_prefetch=1, grid=grid,
            in_specs=in_specs + [page_spec(r) for r in range(PAGES_PER_STEP)],
            out_specs=[o_spec, pl.BlockSpec((1, 1, out_rows, hd), lambda i, f, pt: (i // Bs, i % Bs, f, 0))],
            scratch_shapes=scratch),
        out_shape=[o_shape, jax.ShapeDtypeStruct((n_attn, Bs, grid[1] * out_rows, hd), F32)],
        compiler_params=_params("arbitrary", "arbitrary"), name="ffn_kmean")(
            pt_flat, x, wu_all, wd_all, g, b, *([cache_k4] * PAGES_PER_STEP))


def _cast_tile(w_ref, wb_ref):
    wb = w_ref[...].astype(BF16)
    wb_ref[...] = wb
    return wb


def _proj_gates_step_body(x_ref, w_ref, wg_ref, o_ref, g_ref, wb_ref):
    xb = x_ref[...].astype(BF16)

    @pl.when(pl.program_id(0) == 0)
    def _():
        g_ref[...] = jnp.dot(xb, wg_ref[...], preferred_element_type=F32)

    o_ref[...] = jnp.dot(xb, _cast_tile(w_ref, wb_ref), preferred_element_type=F32)


def _proj_gates_step(x, w_all, wg_all, layer, *, n_main, tn):
    R, K = x.shape
    return pl.pallas_call(
        _proj_gates_step_body, grid=(n_main // tn,),
        in_specs=[pl.BlockSpec((R, K), lambda j: (0, 0)),
                  pl.BlockSpec((None, K, tn), lambda j: (layer, 0, j)),
                  pl.BlockSpec((None, K, LANES), lambda j: (layer, 0, 0))],
        out_specs=[pl.BlockSpec((R, tn), lambda j: (0, j)),
                   pl.BlockSpec((R, LANES), lambda j: (0, 0)),
                   pl.BlockSpec((K, tn), lambda j: (0, j))],
        out_shape=[jax.ShapeDtypeStruct((R, n_main), F32), jax.ShapeDtypeStruct((R, LANES), F32),
                   jax.ShapeDtypeStruct((K, n_main), BF16)],
        compiler_params=_params("arbitrary"), name="proj_gates_step")(x, w_all, wg_all)


def _proj_qkv_step_body(x_ref, w_ref, q_ref, k_ref, v_ref, wb_ref, *, nq_tiles):
    j = pl.program_id(0)
    y = jnp.dot(x_ref[...].astype(BF16), _cast_tile(w_ref, wb_ref), preferred_element_type=F32)

    @pl.when(j < nq_tiles)
    def _():
        q_ref[...] = y.astype(q_ref.dtype)

    @pl.when(j == nq_tiles)
    def _():
        d_kv = k_ref.shape[1]
        k_ref[...] = y[:, :d_kv]
        v_ref[...] = y[:, d_kv:]


def _proj_qkv_step(x, w_all, layer, *, d_q, d_kv):
    R, K = x.shape
    tn = 2 * d_kv
    nq_tiles = d_q // tn
    assert nq_tiles * tn == d_q
    return pl.pallas_call(
        functools.partial(_proj_qkv_step_body, nq_tiles=nq_tiles),
        grid=(nq_tiles + 1,),
        in_specs=[pl.BlockSpec((R, K), lambda j: (0, 0)),
                  pl.BlockSpec((None, K, tn), lambda j: (layer, 0, j))],
        out_specs=[pl.BlockSpec((R, tn), lambda j: (0, jnp.minimum(j, nq_tiles - 1))),
                   pl.BlockSpec((R, d_kv), lambda j: (0, 0)),
                   pl.BlockSpec((R, d_kv), lambda j: (0, 0)),
                   pl.BlockSpec((K, tn), lambda j: (0, j))],
        out_shape=[jax.ShapeDtypeStruct((R, d_q), BF16),
                   jax.ShapeDtypeStruct((R, d_kv), F32),
                   jax.ShapeDtypeStruct((R, d_kv), F32),
                   jax.ShapeDtypeStruct((K, d_q + 2 * d_kv), BF16)],
        compiler_params=_params("arbitrary"), name="proj_qkv_step")(x, w_all)


def _proj_ln_step_body(a_ref, w_ref, res_ref, g_ref, b_ref, o_ref, wb_ref, acc_ref, *, alpha):
    k = pl.program_id(0)

    @pl.when(k == 0)
    def _():
        acc_ref[...] = alpha * res_ref[...]

    acc_ref[...] += jnp.dot(a_ref[...], _cast_tile(w_ref, wb_ref), preferred_element_type=F32)

    @pl.when(k == pl.num_programs(0) - 1)
    def _():
        o_ref[...] = _layer_norm(acc_ref[...], g_ref[...], b_ref[...])


def _proj_ln_step(a, w_all, layer, res, g, b, *, alpha, tk):
    R, D = res.shape
    K = w_all.shape[1]
    return pl.pallas_call(
        functools.partial(_proj_ln_step_body, alpha=alpha),
        grid=(K // tk,),
        in_specs=[pl.BlockSpec((R, tk), lambda k: (0, k)),
                  pl.BlockSpec((None, tk, D), lambda k: (layer, k, 0)),
                  pl.BlockSpec((R, D), lambda k: (0, 0)),
                  pl.BlockSpec((1, D), lambda k: (0, 0)),
                  pl.BlockSpec((1, D), lambda k: (0, 0))],
        out_specs=[pl.BlockSpec((R, D), lambda k: (0, 0)),
                   pl.BlockSpec((tk, D), lambda k: (k, 0))],
        out_shape=[jax.ShapeDtypeStruct((R, D), F32), jax.ShapeDtypeStruct((K, D), BF16)],
        scratch_shapes=[pltpu.VMEM((R, D), F32)],
        compiler_params=_params("arbitrary"), name="proj_ln_step")(a, w_all, res, g, b)


def _ffn_step_body(x_ref, wu_ref, wd_ref, g_ref, b_ref, o_ref, wub_ref, wdb_ref, acc_ref, *, alpha):
    f = pl.program_id(0)

    @pl.when(f == 0)
    def _():
        acc_ref[...] = jnp.zeros_like(acc_ref)

    h = jnp.dot(x_ref[...].astype(BF16), _cast_tile(wu_ref, wub_ref), preferred_element_type=F32)
    h = jnp.square(jnp.maximum(h, 0.0)).astype(BF16)
    acc_ref[...] += jnp.dot(h, _cast_tile(wd_ref, wdb_ref), preferred_element_type=F32)

    @pl.when(f == pl.num_programs(0) - 1)
    def _():
        y = alpha * x_ref[...] + acc_ref[...]
        o_ref[...] = _layer_norm(y, g_ref[...], b_ref[...])


def _ffn_step(x, wu_all, wd_all, layer, g, b, *, alpha, tf):
    R, D = x.shape
    Fd = wu_all.shape[2]
    return pl.pallas_call(
        functools.partial(_ffn_step_body, alpha=alpha),
        grid=(Fd // tf,),
        in_specs=[pl.BlockSpec((R, D), lambda f: (0, 0)),
                  pl.BlockSpec((None, D, tf), lambda f: (layer, 0, f)),
                  pl.BlockSpec((None, tf, D), lambda f: (layer, f, 0)),
                  pl.BlockSpec((1, D), lambda f: (0, 0)),
                  pl.BlockSpec((1, D), lambda f: (0, 0))],
        out_specs=[pl.BlockSpec((R, D), lambda f: (0, 0)),
                   pl.BlockSpec((D, tf), lambda f: (0, f)),
                   pl.BlockSpec((tf, D), lambda f: (f, 0))],
        out_shape=[jax.ShapeDtypeStruct((R, D), F32),
                   jax.ShapeDtypeStruct((D, Fd), BF16),
                   jax.ShapeDtypeStruct((Fd, D), BF16)],
        scratch_shapes=[pltpu.VMEM((R, D), F32)],
        compiler_params=_params("arbitrary"), name="ffn_step")(x, wu_all, wd_all, g, b)


def _pool_prompt_body(u_ref, wp_ref, sc_ref, y_ref, ext_ref, *, T, gc):
    t = pl.program_id(1)

    @pl.when(t == 0)
    def _():
        ext_ref[0:POOL_HALO, :] = jnp.zeros((POOL_HALO, ext_ref.shape[1]), F32)

    ext_ref[POOL_HALO:POOL_HALO + T, :] = u_ref[...]
    pos = t * T + lax.broadcasted_iota(jnp.int32, (T, 1), 0)
    for gi, w in enumerate(POOL_WINDOWS):
        c0, c1 = gi * gc, (gi + 1) * gc
        tok = ext_ref[POOL_HALO:POOL_HALO + T, c0:c1]
        acc = tok
        for k in range(1, w):
            acc = acc + ext_ref[POOL_HALO - k:POOL_HALO - k + T, c0:c1]
        cnt = jnp.minimum(w, pos + 1).astype(F32)
        d = acc / cnt - tok
        y = jnp.dot(d.astype(BF16), wp_ref[gi], preferred_element_type=F32) * sc_ref[:, c0:c1]
        y_ref[:, c0:c1] = y.astype(y_ref.dtype)
    ext_ref[0:POOL_HALO, :] = ext_ref[T:T + POOL_HALO, :]


def _pool_prompt(z, wp_all, layer, sc, *, B, S, d_pool, T):
    nt = S // T
    gc = d_pool // len(POOL_WINDOWS)
    return pl.pallas_call(
        functools.partial(_pool_prompt_body, T=T, gc=gc),
        grid=(B, nt),
        in_specs=[pl.BlockSpec((T, d_pool), lambda b, t: (b * nt + t, 0)),
                  pl.BlockSpec((None,) + wp_all.shape[1:], lambda b, t: (layer, 0, 0, 0)),
                  pl.BlockSpec((1, d_pool), lambda b, t: (0, 0))],
        out_specs=pl.BlockSpec((T, d_pool), lambda b, t: (b * nt + t, 0)),
        out_shape=jax.ShapeDtypeStruct((B * S, d_pool), BF16),
        scratch_shapes=[pltpu.VMEM((POOL_HALO + T, d_pool), F32)],
        compiler_params=_params("parallel", "arbitrary"), name="pool_prompt")(z, wp_all, sc)


def _pool_step_body(buf_ref, u_ref, wp_ref, sc_ref, y_ref, *, gc, pos0):
    u = u_ref[...]
    for gi, w in enumerate(POOL_WINDOWS):
        c0, c1 = gi * gc, (gi + 1) * gc
        tok = u[:, c0:c1]
        acc = tok
        for k in range(1, w):
            acc = acc + buf_ref[POOL_BUF - k, :, c0:c1]
        d = acc / float(min(w, pos0 + 1)) - tok
        y = jnp.dot(d.astype(BF16), wp_ref[gi], preferred_element_type=F32) * sc_ref[:, c0:c1]
        y_ref[:, c0:c1] = y.astype(y_ref.dtype)


def _pool_step(buf_t, u, wp, sc, *, pos0):
    Bs, d_pool = u.shape
    gc = d_pool // len(POOL_WINDOWS)
    return pl.pallas_call(
        functools.partial(_pool_step_body, gc=gc, pos0=pos0),
        out_shape=jax.ShapeDtypeStruct((Bs, d_pool), BF16),
        compiler_params=pltpu.CompilerParams(vmem_limit_bytes=VMEM_LIMIT),
        name="pool_step")(buf_t, u, wp, sc)


def _head_norm_gate(h, o, g):
    mu = jnp.mean(h, axis=-1, keepdims=True)
    hc = h - mu
    var = jnp.mean(hc * hc, axis=-1, keepdims=True)
    return hc * lax.rsqrt(var + HEAD_NORM_EPS) * g * jax.nn.sigmoid(o)


def _mlstm_prompt_body(q_ref, k_ref, v_ref, o_ref, gt_ref, bias_ref, hg_ref,
                       hn_ref, C_ref, n_ref, m_ref, *, L, dk):
    c = pl.program_id(1)

    @pl.when(c == 0)
    def _():
        C_ref[...] = jnp.zeros_like(C_ref)
        n_ref[...] = jnp.zeros_like(n_ref)
        m_ref[...] = jnp.zeros_like(m_ref)

    gt = gt_ref[...] + bias_ref[...]
    gT = gt.T
    row = lax.broadcasted_iota(jnp.int32, (L, L), 0)
    col = lax.broadcasted_iota(jnp.int32, (L, L), 1)
    causal = col <= row
    neg_inf = jnp.float32(-jnp.inf)
    k_scale = dk ** -0.5
    for h in range(MH):
        c0, c1 = h * dk, (h + 1) * dk
        ig_col = gt[:, h:h + 1]
        lf_col = _log_sigmoid(gt[:, MH + h:MH + h + 1])
        ig_row = gT[h:h + 1, :]
        lf_row = _log_sigmoid(gT[MH + h:MH + h + 1, :])
        b_col = jnp.sum(jnp.where(causal, lf_row, 0.0), axis=1, keepdims=True)
        b_row = jnp.sum(jnp.where(row <= col, lf_col, 0.0), axis=0, keepdims=True)
        a_row = ig_row - b_row
        a_col = ig_col - b_col
        cmax = jnp.max(jnp.where(causal, a_row, neg_inf), axis=1, keepdims=True)
        m_prev = m_ref[0, h, :, 0:1]
        m_t = b_col + jnp.maximum(m_prev, cmax)
        w_inter = jnp.exp(b_col + m_prev - m_t)
        dmat = jnp.exp(jnp.where(causal, a_row + b_col - m_t, neg_inf))
        q = q_ref[:, c0:c1]
        ks = k_ref[:, c0:c1] * k_scale
        qb = q.astype(BF16)
        kb = ks.astype(BF16)
        vb = v_ref[:, c0:c1].astype(BF16)
        s = lax.dot_general(qb, kb, (((1,), (1,)), ((), ())), preferred_element_type=F32) * dmat
        C_prev = C_ref[0, h]
        n_prev = n_ref[0, h]
        num = (w_inter * jnp.dot(qb, C_prev.astype(BF16), preferred_element_type=F32)
               + jnp.dot(s.astype(BF16), vb, preferred_element_type=F32))
        den = (w_inter * jnp.sum(q * n_prev, axis=1, keepdims=True)
               + jnp.sum(s, axis=1, keepdims=True))
        hh = num * (1.0 / jnp.maximum(jnp.abs(den), jnp.exp(-m_t)))
        hn_ref[:, c0:c1] = _head_norm_gate(hh, o_ref[:, c0:c1], hg_ref[:, c0:c1]).astype(hn_ref.dtype)

        m_new = m_t[L - 1:L, :]
        b_last = b_col[L - 1:L, :]
        w_old = jnp.exp(b_last + m_prev - m_new)
        kw = ks * jnp.exp(a_col + b_last - m_new)
        C_ref[0, h] = w_old * C_prev + lax.dot_general(
            kw.astype(BF16), vb, (((0,), (0,)), ((), ())), preferred_element_type=F32)
        n_ref[0, h] = w_old * n_prev + jnp.sum(kw, axis=0, keepdims=True)
        m_ref[0, h] = jnp.broadcast_to(m_new, (1, LANES))


def _mlstm_prompt(z, gates, bias, hg, *, B, S, d_pool, d_ml, L):
    nc = S // L
    dk = d_ml // MH
    blk0 = d_pool // d_ml
    assert blk0 * d_ml == d_pool

    def zspec(part):
        return pl.BlockSpec((L, d_ml), lambda b, c: (b * nc + c, blk0 + part))

    return pl.pallas_call(
        functools.partial(_mlstm_prompt_body, L=L, dk=dk),
        grid=(B, nc),
        in_specs=[zspec(0), zspec(1), zspec(2), zspec(3),
                  pl.BlockSpec((L, LANES), lambda b, c: (b * nc + c, 0)),
                  pl.BlockSpec((1, LANES), lambda b, c: (0, 0)),
                  pl.BlockSpec((1, d_ml), lambda b, c: (0, 0))],
        out_specs=[pl.BlockSpec((L, d_ml), lambda b, c: (b * nc + c, 0)),
                   pl.BlockSpec((1, MH, dk, dk), lambda b, c: (b, 0, 0, 0)),
                   pl.BlockSpec((1, MH, 1, dk), lambda b, c: (b, 0, 0, 0)),
                   pl.BlockSpec((1, MH, 1, LANES), lambda b, c: (b, 0, 0, 0))],
        out_shape=[jax.ShapeDtypeStruct((B * S, d_ml), BF16),
                   jax.ShapeDtypeStruct((B, MH, dk, dk), F32),
                   jax.ShapeDtypeStruct((B, MH, 1, dk), F32),
                   jax.ShapeDtypeStruct((B, MH, 1, LANES), F32)],
        compiler_params=_params("parallel", "arbitrary"), name="mlstm_prompt")(
            z, z, z, z, gates, bias, hg)


def _row_to_col(x_row, n):
    r = lax.broadcasted_iota(jnp.int32, (n, n), 0)
    c = lax.broadcasted_iota(jnp.int32, (n, n), 1)
    return jnp.sum(jnp.where(r == c, x_row, 0.0), axis=1, keepdims=True)


def _mlstm_step_body(z_ref, gt_ref, bias_ref, hg_ref, C0_ref, n0_ref, m0_ref,
                     hn_ref, C_ref, n_ref, m_ref, *, d_pool, d_ml, dk):
    gt = gt_ref[0] + bias_ref[...]
    k_scale = dk ** -0.5
    for h in range(MH):
        base = d_pool + h * dk
        q = z_ref[0, :, base:base + dk]
        ks = z_ref[0, :, base + d_ml:base + d_ml + dk] * k_scale
        v = z_ref[0, :, base + 2 * d_ml:base + 2 * d_ml + dk]
        o = z_ref[0, :, base + 3 * d_ml:base + 3 * d_ml + dk]
        ig = gt[:, h:h + 1]
        lf = _log_sigmoid(gt[:, MH + h:MH + h + 1])
        m_prev = m0_ref[0, h]
        m_new = jnp.maximum(m_prev + lf, ig)
        w_old = jnp.exp(lf + m_prev - m_new)
        w_s = jnp.exp(ig - m_new)
        C_new = w_old * C0_ref[0, h] + _row_to_col(ks * w_s, dk) * v
        n_new = w_old * n0_ref[0, h] + w_s * ks
        num = jnp.sum(_row_to_col(q, dk) * C_new, axis=0, keepdims=True)
        den = jnp.sum(q * n_new, axis=1, keepdims=True)
        hh = num / jnp.maximum(jnp.abs(den), jnp.exp(-m_new))
        c0 = h * dk
        hn_ref[0, :, c0:c0 + dk] = _head_norm_gate(hh, o, hg_ref[:, c0:c0 + dk]).astype(hn_ref.dtype)
        C_ref[0, h] = C_new
        n_ref[0, h] = n_new
        m_ref[0, h] = m_new


def _mlstm_step(z3, g3, bias, hg, C0, n0, m0, *, d_pool, d_ml):
    Bs = C0.shape[0]
    dk = d_ml // MH
    dz = z3.shape[-1]
    st = lambda *shape: pl.BlockSpec((1,) + shape, lambda b: (b,) + (0,) * len(shape))
    return pl.pallas_call(
        functools.partial(_mlstm_step_body, d_pool=d_pool, d_ml=d_ml, dk=dk),
        grid=(Bs,),
        in_specs=[st(1, dz), st(1, LANES),
                  pl.BlockSpec((1, LANES), lambda b: (0, 0)),
                  pl.BlockSpec((1, d_ml), lambda b: (0, 0)),
                  st(MH, dk, dk), st(MH, 1, dk), st(MH, 1, 1)],
        out_specs=[st(1, d_ml), st(MH, dk, dk), st(MH, 1, dk), st(MH, 1, 1)],
        out_shape=[jax.ShapeDtypeStruct((Bs, 1, d_ml), BF16),
                   jax.ShapeDtypeStruct((Bs, MH, dk, dk), F32),
                   jax.ShapeDtypeStruct((Bs, MH, 1, dk), F32),
                   jax.ShapeDtypeStruct((Bs, MH, 1, 1), F32)],
        compiler_params=_params("parallel"), name="mlstm_step")(z3, g3, bias, hg, C0, n0, m0)


KM_ROWS = 16


F32_BIG = 1.0e30


def _moba_seq_body(q_ref, k_ref, v_ref, o_ref, kb_ref, vT_ref, km_ref, qT_ref, pT_ref,
                   m_ref, l_ref, acc_ref, *, nblk, hd):
    blk = MOBA_BLOCK
    R = GROUP * blk
    nchunk = R // LANES
    c_exp = hd ** -0.5 * LOG2E
    neg_inf = jnp.float32(-jnp.inf)

    kb_ref[...] = k_ref[...].astype(BF16)
    vT_ref[...] = v_ref[...].T.astype(BF16)
    km = [jnp.sum(k_ref[n * blk:(n + 1) * blk, :], axis=0, keepdims=True) * (1.0 / blk) for n in range(nblk)]
    km.append(jnp.zeros((KM_ROWS - nblk, hd), F32))
    km_ref[...] = jnp.concatenate(km, axis=0).astype(BF16)
    sub = lax.broadcasted_iota(jnp.int32, (nblk, R), 0)

    def own_block(j, buf, p_rows_ref):
        ms, ls = [], []
        for c in range(nchunk):
            lanes = slice(c * LANES, (c + 1) * LANES)
            q0 = (c * LANES) % blk
            nk = q0 + LANES
            s = jnp.dot(kb_ref[j * blk:j * blk + nk, :], qT_ref[buf, :, lanes], preferred_element_type=F32)
            key = lax.broadcasted_iota(jnp.int32, (nk, LANES), 0)
            qry = lax.broadcasted_iota(jnp.int32, (nk, LANES), 1) + q0
            s = jnp.where(key <= qry, s, neg_inf)
            m = jnp.max(s, axis=0, keepdims=True)
            p = jnp.exp2((s - m) * c_exp)
            ms.append(m)
            ls.append(jnp.sum(p, axis=0, keepdims=True))
            p_rows_ref[0:nk, lanes] = p.astype(BF16)
            if nk < blk:
                p_rows_ref[nk:blk, lanes] = jnp.zeros((blk - nk, LANES), BF16)
        return ms, ls

    def write_out(j, acc, l_all):
        inv_l = 1.0 / l_all
        for g in range(GROUP):
            cols = slice(g * blk, (g + 1) * blk)
            o_ref[j * blk:(j + 1) * blk, g * hd:(g + 1) * hd] = (acc[:, cols] * inv_l[:, cols]).T.astype(o_ref.dtype)

    for j in range(nblk):
        buf = j % 2
        for g in range(GROUP):
            qT_ref[buf, :, g * blk:(g + 1) * blk] = (
                q_ref[j * blk:(j + 1) * blk, g * hd:(g + 1) * hd].astype(F32).T.astype(BF16))

        sel = None
        if j > MOBA_TOPK:
            gate = jnp.dot(km_ref[...], qT_ref[buf], preferred_element_type=F32)[0:nblk, :]
            rows = []
            for n in range(j):
                gn = gate[n:n + 1, :]
                beats = ((gate > gn) | ((gate == gn) & (sub < n))) & (sub < j)
                rows.append((jnp.sum(beats.astype(F32), axis=0, keepdims=True) < MOBA_TOPK).astype(F32))
            sel = rows

        ms, ls = own_block(j, buf, pT_ref.at[buf, j * blk:(j + 1) * blk])
        for n in range(j):
            kb = kb_ref[n * blk:(n + 1) * blk, :]
            for g in range(GROUP):
                s2 = jnp.dot(kb, qT_ref[buf, :, g * blk:(g + 1) * blk], preferred_element_type=F32)
                for half in range(blk // LANES):
                    c = g * (blk // LANES) + half
                    lanes = slice(c * LANES, (c + 1) * LANES)
                    s = s2[:, half * LANES:(half + 1) * LANES]
                    if sel is not None:
                        s = jnp.where(sel[n][:, lanes] > 0.5, s, neg_inf)
                    p = jnp.exp2((s - ms[c]) * c_exp)
                    ls[c] = ls[c] + jnp.sum(p, axis=0, keepdims=True)
                    pT_ref[buf, n * blk:(n + 1) * blk, lanes] = p.astype(BF16)
        nkeys = (j + 1) * blk
        acc = jnp.dot(vT_ref[:, 0:nkeys], pT_ref[buf, 0:nkeys, :], preferred_element_type=F32)
        l_all = jnp.concatenate(ls, axis=1)
        write_out(j, acc, l_all)

        if j > 0:
            @pl.when(jnp.logical_not(jnp.max(l_all) < F32_BIG))
            def _(j=j, buf=buf, sel=sel):
                ms, ls = own_block(j, buf, pT_ref.at[buf, 0:blk])
                m_ref[...] = jnp.concatenate(ms, axis=1)
                l_ref[...] = jnp.concatenate(ls, axis=1)
                acc_ref[...] = jnp.dot(vT_ref[:, j * blk:(j + 1) * blk], pT_ref[buf, 0:blk, :],
                                       preferred_element_type=F32)
                selmat = None if sel is None else jnp.concatenate(
                    sel + [jnp.zeros((nblk - j, R), F32)], axis=0)

                def body(n, carry):
                    start = pl.multiple_of(n * blk, blk)
                    kb = kb_ref[pl.ds(start, blk), :]
                    vtb = vT_ref[:, pl.ds(start, blk)]
                    if selmat is not None:
                        sel_n = jnp.sum(jnp.where(sub == n, selmat, 0.0), axis=0, keepdims=True)
                    for c in range(nchunk):
                        lanes = slice(c * LANES, (c + 1) * LANES)
                        s = jnp.dot(kb, qT_ref[buf, :, lanes], preferred_element_type=F32)
                        if selmat is not None:
                            s = jnp.where(sel_n[:, lanes] > 0.5, s, neg_inf)
                        m_old = m_ref[:, lanes]
                        m_new = jnp.maximum(m_old, jnp.max(s, axis=0, keepdims=True))
                        alpha = jnp.exp2((m_old - m_new) * c_exp)
                        p = jnp.exp2((s - m_new) * c_exp)
                        m_ref[:, lanes] = m_new
                        l_ref[:, lanes] = alpha * l_ref[:, lanes] + jnp.sum(p, axis=0, keepdims=True)
                        acc_ref[:, lanes] = acc_ref[:, lanes] * alpha + jnp.dot(
                            vtb, p.astype(BF16), preferred_element_type=F32)
                    return carry

                lax.fori_loop(0, j, body, 0)
                write_out(j, acc_ref[...], l_ref[...])


def _moba_seq(q, k, v, *, B, S, hd):
    assert S % MOBA_BLOCK == 0 and S // MOBA_BLOCK <= KM_ROWS
    nq = S // MOBA_BLOCK
    gw = GROUP * hd
    R = GROUP * MOBA_BLOCK
    return pl.pallas_call(
        functools.partial(_moba_seq_body, nblk=nq, hd=hd),
        grid=(B, N_KV_HEADS),
        in_specs=[pl.BlockSpec((S, gw), lambda b, h: (b, h)),
                  pl.BlockSpec((S, hd), lambda b, h: (b, h)),
                  pl.BlockSpec((S, hd), lambda b, h: (b, h))],
        out_specs=pl.BlockSpec((S, gw), lambda b, h: (b, h)),
        out_shape=jax.ShapeDtypeStruct((B * S, N_HEADS * hd), BF16),
        scratch_shapes=[pltpu.VMEM((S, hd), BF16),
                        pltpu.VMEM((hd, S), BF16),
                        pltpu.VMEM((KM_ROWS, hd), BF16),
                        pltpu.VMEM((2, hd, R), BF16),
                        pltpu.VMEM((2, S, R), BF16),
                        pltpu.VMEM((1, R), F32),
                        pltpu.VMEM((1, R), F32),
                        pltpu.VMEM((hd, R), F32)],
        compiler_params=_params("parallel", "parallel"), name="moba_seq")(q, k, v)


PAGES_PER_STEP = 16
PAGES_PER_BLOCK = MOBA_BLOCK // PAGE_SIZE
TOKENS_PER_TILE = SUBLANES // N_KV_HEADS
KV_SHIFT = N_KV_HEADS.bit_length() - 1
GROUP_SHIFT = GROUP.bit_length() - 1
assert 1 << KV_SHIFT == N_KV_HEADS and 1 << GROUP_SHIFT == GROUP and TOKENS_PER_TILE * N_KV_HEADS == SUBLANES


def _block_mean_keys(page_refs, o_ref):
    rows, hd = page_refs[0].shape[2:]
    for blk in range(PAGES_PER_STEP // PAGES_PER_BLOCK):
        acc = None
        for r in range(PAGES_PER_BLOCK):
            pg = page_refs[blk * PAGES_PER_BLOCK + r][0, 0]
            part = jnp.sum(pg.reshape(rows // SUBLANES, SUBLANES, hd), axis=0)
            acc = part if acc is None else acc + part
        per_head = acc[0:N_KV_HEADS]
        for t in range(1, TOKENS_PER_TILE):
            per_head = per_head + acc[t * N_KV_HEADS:(t + 1) * N_KV_HEADS]
        o_ref[0, 0, blk * N_KV_HEADS:(blk + 1) * N_KV_HEADS, :] = per_head * (1.0 / MOBA_BLOCK)


def _topk_body(q_ref, km_ref, idx_ref, *, ncol):
    qb = q_ref[0].astype(BF16)
    km = km_ref[0, 0].astype(BF16)
    gate = lax.dot_general(qb, km, (((1,), (1,)), ((), ())), preferred_element_type=F32)
    head = lax.broadcasted_iota(jnp.int32, (N_HEADS, ncol), 0)
    col = lax.broadcasted_iota(jnp.int32, (N_HEADS, ncol), 1)
    gate = jnp.where((col & (N_KV_HEADS - 1)) == (head >> GROUP_SHIFT), gate, -jnp.inf)
    out_lane = lax.broadcasted_iota(jnp.int32, (N_HEADS, LANES), 1)
    out = jnp.zeros((N_HEADS, LANES), jnp.int32)
    for r in range(MOBA_TOPK):
        mx = jnp.max(gate, axis=1, keepdims=True)
        idx = jnp.min(jnp.where(gate == mx, col, ncol), axis=1, keepdims=True)
        out = jnp.where(out_lane == r, idx >> KV_SHIFT, out)
        gate = jnp.where(col == idx, -jnp.inf, gate)
    idx_ref[0] = out


def _topk_blocks(q3, kmean, layer):
    Bs, _, hd = q3.shape
    ncol = kmean.shape[2]
    assert ncol >= MOBA_TOPK * N_KV_HEADS
    return pl.pallas_call(
        functools.partial(_topk_body, ncol=ncol),
        grid=(Bs,),
        in_specs=[pl.BlockSpec((1, N_HEADS, hd), lambda b: (b, 0, 0)),
                  pl.BlockSpec((1, 1, ncol, hd), lambda b: (layer, b, 0, 0))],
        out_specs=pl.BlockSpec((1, N_HEADS, LANES), lambda b: (b, 0, 0)),
        out_shape=jax.ShapeDtypeStruct((Bs, N_HEADS, LANES), jnp.int32),
        compiler_params=_params("parallel"), name="topk_blocks")(q3, kmean)


N_SEL_PAGES = MOBA_TOPK * PAGES_PER_BLOCK


def _moba_step_body(idx_ref, pt_ref, q_ref, kn_ref, vn_ref, *refs, hd):
    n_in = GROUP * N_SEL_PAGES
    k_refs, v_refs, o_ref = refs[:n_in], refs[n_in:2 * n_in], refs[2 * n_in]
    kv_head = pl.program_id(1)
    scale = hd ** -0.5
    rows = SAMPLE_ROWS
    ncol = k_refs[0].shape[2]
    mine = (lax.broadcasted_iota(jnp.int32, (rows, ncol), 1) & (N_KV_HEADS - 1)) == kv_head
    k_new = kn_ref[0, 0].astype(BF16).astype(F32)
    v_new = vn_ref[0, 0].astype(BF16).astype(F32)
    for g in range(GROUP):
        qb = jnp.broadcast_to(q_ref[0, g], (rows, hd)).astype(BF16)
        pages = slice(g * N_SEL_PAGES, (g + 1) * N_SEL_PAGES)
        s_pages = [jnp.where(mine,
                             lax.dot_general(qb, kr[0, 0].astype(BF16), (((1,), (1,)), ((), ())),
                                             preferred_element_type=F32) * scale,
                             -jnp.inf) for kr in k_refs[pages]]
        s_own = jnp.sum(qb.astype(F32) * k_new, axis=1, keepdims=True) * scale
        m = s_own
        for s in s_pages:
            m = jnp.maximum(m, jnp.max(s, axis=1, keepdims=True))
        p_own = jnp.exp(s_own - m)
        l = p_own
        acc = p_own * v_new
        for s, vr in zip(s_pages, v_refs[pages]):
            p = jnp.exp(s - m)
            l = l + jnp.sum(p, axis=1, keepdims=True)
            acc = acc + jnp.dot(p.astype(BF16), vr[0, 0].astype(BF16), preferred_element_type=F32)
        o_ref[0, g] = (acc / l)[0:1, :].astype(o_ref.dtype)


def _moba_step(q4, kn4, vn4, cache_k4, cache_v4, idx_flat, pt_flat, *, layer, n_pages):
    Bs, _, _, hd = q4.shape
    rows = cache_k4.shape[2]

    def page_spec(r):
        g, r = divmod(r, N_SEL_PAGES)
        sel, half = divmod(r, PAGES_PER_BLOCK)

        def index_map(b, h, idx, pt):
            blk = idx[(b * N_HEADS + h * GROUP + g) * MOBA_TOPK + sel]
            return (layer, pt[b * n_pages + blk * PAGES_PER_BLOCK + half], 0, 0)

        return pl.BlockSpec((1, 1, rows, hd), index_map)

    group_spec = pl.BlockSpec((1, GROUP, 1, hd), lambda b, h, idx, pt: (b, h, 0, 0))
    kv_spec = pl.BlockSpec((1, 1, 1, hd), lambda b, h, idx, pt: (b, h, 0, 0))
    n_in = GROUP * N_SEL_PAGES
    return pl.pallas_call(
        functools.partial(_moba_step_body, hd=hd),
        grid_spec=pltpu.PrefetchScalarGridSpec(
            num_scalar_prefetch=2, grid=(Bs, N_KV_HEADS),
            in_specs=[group_spec, kv_spec, kv_spec] + [page_spec(r) for r in range(n_in)] * 2,
            out_specs=group_spec),
        out_shape=jax.ShapeDtypeStruct((Bs, N_HEADS, 1, hd), BF16),
        compiler_params=_params("parallel", "arbitrary"), name="moba_step")(
            idx_flat, pt_flat, q4, kn4, vn4, *([cache_k4] * n_in), *([cache_v4] * n_in))


def _pad_rows(a, rows):
    return jnp.pad(a, ((0, rows - a.shape[0]),) + ((0, 0),) * (a.ndim - 1))


def kernel(x_prompt, x_sample, cache_k, cache_v, state_pool, state_C, state_n, state_m, page_table,
           w_in_mix, b_if, w_pool, pool_scale, mlstm_norm_g, w_out_mix, w_qkv, w_o,
           ln_mix_g, ln_mix_b, ln_ffn_g, ln_ffn_b, w_up, w_down):
    B, S, D = x_prompt.shape
    Bs = x_sample.shape[0]
    assert x_sample.shape[1] == 1 and Bs <= SAMPLE_ROWS
    depth = w_up.shape[0]
    d_pool = state_pool.shape[-1]
    dk = state_C.shape[-1]
    d_ml = MH * dk
    d_main = d_pool + 4 * d_ml
    assert w_in_mix.shape[2] == d_main + 2 * MH
    hd = cache_k.shape[-1]
    d_q, d_kv = N_HEADS * hd, N_KV_HEADS * hd
    n_pages = page_table.shape[1]
    past_len = n_pages * PAGE_SIZE
    assert cache_k.shape[2] == PAGE_SIZE and past_len % MOBA_BLOCK == 0
    alpha = (2 * depth) ** 0.25
    M = B * S
    L = math.gcd(S, MLSTM_CHUNK)

    w_g_b = jnp.pad(w_in_mix[:, :, d_main:], ((0, 0), (0, 0), (0, LANES - 2 * MH))).astype(BF16)
    bias_g = jnp.pad(b_if, ((0, 0), (0, LANES - 2 * MH)))[:, None, :]
    w_pool_b = w_pool.astype(BF16)

    cache_k4 = cache_k.reshape(cache_k.shape[:2] + (PAGE_SIZE * N_KV_HEADS, hd))
    cache_v4 = cache_v.reshape(cache_v.shape[:2] + (PAGE_SIZE * N_KV_HEADS, hd))
    pt_flat = page_table.reshape(-1)
    kmean_past = None

    xs = _pad_rows(x_sample.reshape(Bs, D), SAMPLE_ROWS)
    xp = x_prompt.reshape(M, D)
    ks_, vs_, pool_s, C_s, n_s, m_s = [], [], [], [], [], []
    kp, vp, pool_p, C_p, n_p, m_p = [], [], [], [], [], []
    for l in range(depth):
        i = l // 2
        ln_g, ln_b = ln_mix_g[l][None, :], ln_mix_b[l][None, :]
        fg, fb = ln_ffn_g[l][None, :], ln_ffn_b[l][None, :]
        if l % 2 == 0:
            sc = pool_scale[i][None, :]
            hg = mlstm_norm_g[i][None, :]
            zs, gs, wb = _proj_gates_step(xs, w_in_mix, w_g_b, i, n_main=d_main, tn=1024)
            us = zs[:Bs, :d_pool]
            ys_pool = _pool_step(jnp.transpose(state_pool[i], (1, 0, 2)), us, w_pool_b[i], sc, pos0=past_len)
            hns, C, n, m = _mlstm_step(
                zs[:Bs, None, :], gs[:Bs, None, :], bias_g[i], hg, state_C[i],
                state_n[i][:, :, None, :], state_m[i][:, :, None, None], d_pool=d_pool, d_ml=d_ml)
            a = _pad_rows(jnp.concatenate([ys_pool, hns[:, 0, :]], axis=1), SAMPLE_ROWS)
            xs, wpb = _proj_ln_step(a, w_out_mix, i, xs, ln_g, ln_b, alpha=alpha, tk=512)
            pool_s.append(jnp.concatenate([state_pool[i][:, 1:, :], us[:, None, :]], axis=1))
            C_s.append(C)
            n_s.append(n[:, :, 0, :])
            m_s.append(m[:, :, 0, 0])
        else:
            qs, k_new, v_new, wb = _proj_qkv_step(xs, w_qkv, i, d_q=d_q, d_kv=d_kv)
            q_s = qs[:Bs].astype(F32).reshape(Bs, N_HEADS, hd)
            k_new = k_new[:Bs].reshape(Bs, N_KV_HEADS, 1, hd)
            v_new = v_new[:Bs].reshape(Bs, N_KV_HEADS, 1, hd)
            idx = _topk_blocks(q_s, kmean_past, i)[:, :, :MOBA_TOPK].reshape(-1)
            os_ = _moba_step(q_s[:, :, None, :], k_new, v_new, cache_k4, cache_v4, idx, pt_flat,
                             layer=i, n_pages=n_pages)
            xs, wpb = _proj_ln_step(_pad_rows(os_.reshape(Bs, d_q), SAMPLE_ROWS), w_o, i, xs, ln_g, ln_b,
                                    alpha=alpha, tk=512)
            ks_.append(k_new.reshape(Bs, 1, N_KV_HEADS, hd))
            vs_.append(v_new.reshape(Bs, 1, N_KV_HEADS, hd))
        xs, wub, wdb = _ffn_step(xs, w_up, w_down, l, fg, fb, alpha=alpha, tf=512)

        if l % 2 == 0:
            z, gates = _proj_gates(xp, wb[None], 0, w_g_b, i, n_main=d_main, tm=1024, tn=1024)
            y_pool = _pool_prompt(z, w_pool_b, i, sc, B=B, S=S, d_pool=d_pool, T=256)
            hn, C, n, m = _mlstm_prompt(z, gates, bias_g[i], hg, B=B, S=S, d_pool=d_pool, d_ml=d_ml, L=L)
            xp = _proj_ln([y_pool, hn], wpb[None], 0, xp, ln_g, ln_b, alpha=alpha, tm=512)
            pool_p.append(z.reshape(B, S, d_main)[:, S - POOL_BUF:, :d_pool])
            C_p.append(C)
            n_p.append(n.reshape(B, MH, dk))
            m_p.append(m[:, :, 0, 0])
        else:
            q, k, v = _proj_qkv(xp, wb[None], 0, d_q=d_q, d_kv=d_kv, tm=1024)
            o = _moba_seq(q, k, v, B=B, S=S, hd=hd)
            xp = _proj_ln([o], wpb[None], 0, xp, ln_g, ln_b, alpha=alpha, tm=512)
            kp.append(k.reshape(B, S, N_KV_HEADS, hd))
            vp.append(v.reshape(B, S, N_KV_HEADS, hd))
        if kmean_past is None:
            xp, kmean_past = _ffn(xp, wub[None], wdb[None], 0, fg, fb, alpha=alpha, tm=512, tf=1024,
                                  paged=(cache_k4, pt_flat, Bs, n_pages))
        else:
            xp = _ffn(xp, wub[None], wdb[None], 0, fg, fb, alpha=alpha, tm=512, tf=1024)

    return (xp.reshape(B, S, D), xs[:Bs].reshape(Bs, 1, D),
            jnp.stack(kp), jnp.stack(vp), jnp.stack(ks_), jnp.stack(vs_),
            jnp.stack(pool_p), jnp.stack(C_p), jnp.stack(n_p), jnp.stack(m_p),
            jnp.stack(pool_s), jnp.stack(C_s), jnp.stack(n_s), jnp.stack(m_s))
```

```python
import functools
import math

import jax
import jax.numpy as jnp
from jax import lax
from jax.experimental import pallas as pl
from jax.experimental.pallas import tpu as pltpu

F32 = jnp.float32
BF16 = jnp.bfloat16

POOL_WINDOWS = (2, 4, 8, 16)
POOL_BUF = max(POOL_WINDOWS) - 1
POOL_HALO = 16
MH = 4
N_HEADS = 16
N_KV_HEADS = 4
GROUP = N_HEADS // N_KV_HEADS
MOBA_BLOCK = 256
MOBA_TOPK = 3
PAGE_SIZE = 128
LN_EPS = 1e-5
HEAD_NORM_EPS = 1e-6
LANES = 128
SUBLANES = 8
SAMPLE_ROWS = 16
LOG2E = 1.4426950408889634

VMEM_LIMIT = 56 * 1024 * 1024
MLSTM_CHUNK = 256
LN_SUB_ROWS = 256


def _params(*sem):
    return pltpu.CompilerParams(dimension_semantics=sem, vmem_limit_bytes=VMEM_LIMIT)


def _layer_norm(y, g, b):
    mu = jnp.mean(y, axis=-1, keepdims=True)
    yc = y - mu
    var = jnp.mean(yc * yc, axis=-1, keepdims=True)
    return yc * lax.rsqrt(var + LN_EPS) * g + b


def _log_sigmoid(x):
    return jnp.minimum(x, 0.0) - jnp.log1p(jnp.exp(-jnp.abs(x)))


_NT = (((1,), (1,)), ((), ()))


def _proj_gates_body(x_ref, w_ref, wg_ref, o_ref, g_ref, xb_ref):
    @pl.when(pl.program_id(1) == 0)
    def _():
        xb = x_ref[...].astype(BF16)
        xb_ref[...] = xb
        g_ref[...] = lax.dot_general(xb, wg_ref[...], _NT, preferred_element_type=F32)

    o_ref[...] = lax.dot_general(xb_ref[...], w_ref[...], _NT, preferred_element_type=F32)


def _proj_gates(x, wt, wgt, *, tm, tn):
    M, K = x.shape
    n_main = wt.shape[0]
    return pl.pallas_call(
        _proj_gates_body, grid=(M // tm, n_main // tn),
        in_specs=[pl.BlockSpec((tm, K), lambda i, j: (i, 0)),
                  pl.BlockSpec((tn, K), lambda i, j: (j, 0)),
                  pl.BlockSpec((LANES, K), lambda i, j: (0, 0))],
        out_specs=[pl.BlockSpec((tm, tn), lambda i, j: (i, j)),
                   pl.BlockSpec((tm, LANES), lambda i, j: (i, 0))],
        out_shape=[jax.ShapeDtypeStruct((M, n_main), F32), jax.ShapeDtypeStruct((M, LANES), F32)],
        scratch_shapes=[pltpu.VMEM((tm, K), BF16)],
        compiler_params=_params("parallel", "arbitrary"), name="proj_gates")(x, wt, wgt)


def _proj_qkv_body(x_ref, w_ref, *rest, nq_tiles, n_alias):
    q_ref, k_ref, v_ref, kf_ref, vf_ref, xb_ref = rest[n_alias:]
    j = pl.program_id(1)

    @pl.when(j == 0)
    def _():
        xb_ref[...] = x_ref[...].astype(BF16)

    y = jnp.dot(xb_ref[...], w_ref[...], preferred_element_type=F32)

    @pl.when(j < nq_tiles)
    def _():
        q_ref[...] = y.astype(q_ref.dtype)

    @pl.when(j == nq_tiles)
    def _():
        tm, d_kv = k_ref.shape
        hd = kf_ref.shape[1]
        nh = d_kv // hd
        k_ref[...] = y[:, :d_kv]
        v_ref[...] = y[:, d_kv:]
        for h in range(nh):
            kf_ref[pl.ds(h, tm, stride=nh), :] = y[:, h * hd:(h + 1) * hd]
            vf_ref[pl.ds(h, tm, stride=nh), :] = y[:, d_kv + h * hd:d_kv + (h + 1) * hd]


def _proj_qkv(x, w_all, layer, *, d_q, d_kv, hd, tm, slab, n_slabs, prev=None):
    M, K = x.shape
    tn = 2 * d_kv
    nq_tiles = d_q // tn
    nh = d_kv // hd
    assert nq_tiles * tn == d_q
    flat_spec = pl.BlockSpec((None, tm * nh, hd), lambda i, j: (slab, i, 0))
    flat_shape = jax.ShapeDtypeStruct((n_slabs, M * nh, hd), F32)
    in_specs = [pl.BlockSpec((tm, K), lambda i, j: (i, 0)),
                pl.BlockSpec((None, K, tn), lambda i, j: (layer, 0, j))]
    args = [x, w_all]
    aliases = {}
    if prev is not None:
        in_specs += [pl.BlockSpec(memory_space=pl.ANY)] * 2
        args += list(prev)
        aliases = {2: 3, 3: 4}
    return pl.pallas_call(
        functools.partial(_proj_qkv_body, nq_tiles=nq_tiles, n_alias=len(aliases)),
        grid=(M // tm, nq_tiles + 1),
        in_specs=in_specs,
        out_specs=[pl.BlockSpec((tm, tn), lambda i, j: (i, jnp.minimum(j, nq_tiles - 1))),
                   pl.BlockSpec((tm, d_kv), lambda i, j: (i, 0)),
                   pl.BlockSpec((tm, d_kv), lambda i, j: (i, 0)),
                   flat_spec, flat_spec],
        out_shape=[jax.ShapeDtypeStruct((M, d_q), BF16),
                   jax.ShapeDtypeStruct((M, d_kv), F32),
                   jax.ShapeDtypeStruct((M, d_kv), F32),
                   flat_shape, flat_shape],
        scratch_shapes=[pltpu.VMEM((tm, K), BF16)],
        input_output_aliases=aliases,
        compiler_params=_params("parallel", "arbitrary"), name="proj_qkv")(*args)


def _proj_ln_body(*refs, n_a, alpha):
    a_refs = refs[:n_a]
    w_ref, res_ref, g_ref, b_ref, o_ref = refs[n_a:]
    tm = res_ref.shape[0]
    sub = min(tm, LN_SUB_ROWS)
    for r0 in range(0, tm, sub):
        rows = slice(r0, r0 + sub)
        y = alpha * res_ref[rows, :]
        k0 = 0
        for a_ref in a_refs:
            ka = a_ref.shape[1]
            y = y + jnp.dot(a_ref[rows, :], w_ref[k0:k0 + ka, :], preferred_element_type=F32)
            k0 += ka
        o_ref[rows, :] = _layer_norm(y, g_ref[...], b_ref[...])


def _proj_ln(a_list, w_all, layer, res, g, b, *, alpha, tm):
    M, D = res.shape
    K = w_all.shape[1]
    in_specs = [pl.BlockSpec((tm, a.shape[1]), lambda i: (i, 0)) for a in a_list]
    in_specs += [pl.BlockSpec((None, K, D), lambda i: (layer, 0, 0)),
                 pl.BlockSpec((tm, D), lambda i: (i, 0)),
                 pl.BlockSpec((1, D), lambda i: (0, 0)),
                 pl.BlockSpec((1, D), lambda i: (0, 0))]
    return pl.pallas_call(
        functools.partial(_proj_ln_body, n_a=len(a_list), alpha=alpha),
        grid=(M // tm,), in_specs=in_specs,
        out_specs=pl.BlockSpec((tm, D), lambda i: (i, 0)),
        out_shape=jax.ShapeDtypeStruct((M, D), F32),
        compiler_params=_params("parallel"), name="proj_ln")(*a_list, w_all, res, g, b)


def _ffn_body(*refs, alpha, n_page_refs):
    if n_page_refs:
        refs = refs[1:]
    x_ref, wu_ref, wd_ref, g_ref, b_ref = refs[:5]
    page_refs = refs[5:5 + n_page_refs]
    if n_page_refs:
        o_ref, km_ref, xb_ref, acc_ref = refs[5 + n_page_refs:]
    else:
        o_ref, xb_ref, acc_ref = refs[5:]
    f = pl.program_id(1)
    last = pl.num_programs(1) - 1

    def hidden(rows):
        h = jnp.dot(xb_ref[rows, :], wu_ref[...], preferred_element_type=F32)
        h = jnp.square(jnp.maximum(h, 0.0)).astype(BF16)
        return jnp.dot(h, wd_ref[...], preferred_element_type=F32)

    def page_stream():
        if n_page_refs:
            _block_mean_keys(page_refs, km_ref)

    @pl.when(f == 0)
    def _():
        xb_ref[...] = x_ref[...].astype(BF16)
        acc_ref[...] = jnp.zeros_like(acc_ref)

    @pl.when(f < last)
    def _():
        acc_ref[...] += hidden(slice(None))
        page_stream()

    @pl.when(f == last)
    def _():
        page_stream()
        tm = x_ref.shape[0]
        sub = min(tm, LN_SUB_ROWS)
        for r0 in range(0, tm, sub):
            rows = slice(r0, r0 + sub)
            y = alpha * x_ref[rows, :] + (acc_ref[rows, :] + hidden(rows))
            o_ref[rows, :] = _layer_norm(y, g_ref[...], b_ref[...])


def _ffn(x, wu_all, wd_all, layer, g, b, *, alpha, tm, tf, paged=None):
    M, D = x.shape
    Fd = wu_all.shape[2]
    grid = (M // tm, Fd // tf)
    in_specs = [pl.BlockSpec((tm, D), lambda i, f, *_: (i, 0)),
                pl.BlockSpec((None, D, tf), lambda i, f, *_: (layer, 0, f)),
                pl.BlockSpec((None, tf, D), lambda i, f, *_: (layer, f, 0)),
                pl.BlockSpec((1, D), lambda i, f, *_: (0, 0)),
                pl.BlockSpec((1, D), lambda i, f, *_: (0, 0))]
    o_spec = pl.BlockSpec((tm, D), lambda i, f, *_: (i, 0))
    o_shape = jax.ShapeDtypeStruct((M, D), F32)
    scratch = [pltpu.VMEM((tm, D), BF16), pltpu.VMEM((tm, D), F32)]
    if paged is None:
        return pl.pallas_call(
            functools.partial(_ffn_body, alpha=alpha, n_page_refs=0),
            grid=grid, in_specs=in_specs, out_specs=o_spec, out_shape=o_shape, scratch_shapes=scratch,
            compiler_params=_params("parallel", "arbitrary"), name="ffn")(x, wu_all, wd_all, g, b)

    cache_k4, pt_flat, Bs, n_pages = paged
    n_attn, _, rows, hd = cache_k4.shape
    assert grid == (n_attn * Bs, n_pages // PAGES_PER_STEP)
    out_rows = PAGES_PER_STEP // PAGES_PER_BLOCK * N_KV_HEADS

    def page_spec(r):
        return pl.BlockSpec(
            (1, 1, rows, hd),
            lambda i, f, pt: (i // Bs, pt[(i % Bs) * n_pages + f * PAGES_PER_STEP + r], 0, 0))

    return pl.pallas_call(
        functools.partial(_ffn_body, alpha=alpha, n_page_refs=PAGES_PER_STEP),
        grid_spec=pltpu.PrefetchScalarGridSpec(
            num_scalar_prefetch=1, grid=grid,
            in_specs=in_specs + [page_spec(r) for r in range(PAGES_PER_STEP)],
            out_specs=[o_spec, pl.BlockSpec((1, 1, out_rows, hd), lambda i, f, pt: (i // Bs, i % Bs, f, 0))],
            scratch_shapes=scratch),
        out_shape=[o_shape, jax.ShapeDtypeStruct((n_attn, Bs, grid[1] * out_rows, hd), F32)],
        compiler_params=_params("arbitrary", "arbitrary"), name="ffn_kmean")(
            pt_flat, x, wu_all, wd_all, g, b, *([cache_k4] * PAGES_PER_STEP))


def _cast_tile(w_ref, wb_ref):
    wb = w_ref[...].astype(BF16)
    wb_ref[...] = wb
    return wb


def _proj_gates_step_body(x_ref, w_ref, wg_ref, o_ref, g_ref, wb_ref, wgb_ref):
    xb = x_ref[...].astype(BF16)

    @pl.when(pl.program_id(0) == 0)
    def _():
        n_gate, K = wg_ref.shape
        wg = jnp.concatenate([wg_ref[...], jnp.zeros((LANES - n_gate, K), F32)], axis=0).astype(BF16)
        wgb_ref[...] = wg
        g_ref[...] = lax.dot_general(xb, wg, _NT, preferred_element_type=F32)

    o_ref[...] = lax.dot_general(xb, _cast_tile(w_ref, wb_ref), _NT, preferred_element_type=F32)


def _proj_gates_step(x, wt_all, layer, *, n_main, n_gate, tn):
    R, K = x.shape
    assert n_main % n_gate == 0 and n_gate % SUBLANES == 0
    return pl.pallas_call(
        _proj_gates_step_body, grid=(n_main // tn,),
        in_specs=[pl.BlockSpec((R, K), lambda j: (0, 0)),
                  pl.BlockSpec((None, tn, K), lambda j: (layer, j, 0)),
                  pl.BlockSpec((None, n_gate, K), lambda j: (layer, n_main // n_gate, 0))],
        out_specs=[pl.BlockSpec((R, tn), lambda j: (0, j)),
                   pl.BlockSpec((R, LANES), lambda j: (0, 0)),
                   pl.BlockSpec((tn, K), lambda j: (j, 0)),
                   pl.BlockSpec((LANES, K), lambda j: (0, 0))],
        out_shape=[jax.ShapeDtypeStruct((R, n_main), F32), jax.ShapeDtypeStruct((R, LANES), F32),
                   jax.ShapeDtypeStruct((n_main, K), BF16), jax.ShapeDtypeStruct((LANES, K), BF16)],
        compiler_params=_params("arbitrary"), name="proj_gates_step")(x, wt_all, wt_all)


def _proj_qkv_step_body(x_ref, w_ref, q_ref, k_ref, v_ref, wb_ref, *, nq_tiles):
    j = pl.program_id(0)
    y = jnp.dot(x_ref[...].astype(BF16), _cast_tile(w_ref, wb_ref), preferred_element_type=F32)

    @pl.when(j < nq_tiles)
    def _():
        q_ref[...] = y.astype(q_ref.dtype)

    @pl.when(j == nq_tiles)
    def _():
        d_kv = k_ref.shape[1]
        k_ref[...] = y[:, :d_kv]
        v_ref[...] = y[:, d_kv:]


def _proj_qkv_step(x, w_all, layer, *, d_q, d_kv):
    R, K = x.shape
    tn = 2 * d_kv
    nq_tiles = d_q // tn
    assert nq_tiles * tn == d_q
    return pl.pallas_call(
        functools.partial(_proj_qkv_step_body, nq_tiles=nq_tiles),
        grid=(nq_tiles + 1,),
        in_specs=[pl.BlockSpec((R, K), lambda j: (0, 0)),
                  pl.BlockSpec((None, K, tn), lambda j: (layer, 0, j))],
        out_specs=[pl.BlockSpec((R, tn), lambda j: (0, jnp.minimum(j, nq_tiles - 1))),
                   pl.BlockSpec((R, d_kv), lambda j: (0, 0)),
                   pl.BlockSpec((R, d_kv), lambda j: (0, 0)),
                   pl.BlockSpec((K, tn), lambda j: (0, j))],
        out_shape=[jax.ShapeDtypeStruct((R, d_q), BF16),
                   jax.ShapeDtypeStruct((R, d_kv), F32),
                   jax.ShapeDtypeStruct((R, d_kv), F32),
                   jax.ShapeDtypeStruct((K, d_q + 2 * d_kv), BF16)],
        compiler_params=_params("arbitrary"), name="proj_qkv_step")(x, w_all)


def _proj_ln_step_body(a_ref, w_ref, res_ref, g_ref, b_ref, o_ref, wb_ref, acc_ref, *, alpha):
    k = pl.program_id(0)

    @pl.when(k == 0)
    def _():
        acc_ref[...] = alpha * res_ref[...]

    acc_ref[...] += jnp.dot(a_ref[...], _cast_tile(w_ref, wb_ref), preferred_element_type=F32)

    @pl.when(k == pl.num_programs(0) - 1)
    def _():
        o_ref[...] = _layer_norm(acc_ref[...], g_ref[...], b_ref[...])


def _proj_ln_step(a, w_all, layer, res, g, b, *, alpha, tk):
    R, D = res.shape
    K = w_all.shape[1]
    return pl.pallas_call(
        functools.partial(_proj_ln_step_body, alpha=alpha),
        grid=(K // tk,),
        in_specs=[pl.BlockSpec((R, tk), lambda k: (0, k)),
                  pl.BlockSpec((None, tk, D), lambda k: (layer, k, 0)),
                  pl.BlockSpec((R, D), lambda k: (0, 0)),
                  pl.BlockSpec((1, D), lambda k: (0, 0)),
                  pl.BlockSpec((1, D), lambda k: (0, 0))],
        out_specs=[pl.BlockSpec((R, D), lambda k: (0, 0)),
                   pl.BlockSpec((tk, D), lambda k: (k, 0))],
        out_shape=[jax.ShapeDtypeStruct((R, D), F32), jax.ShapeDtypeStruct((K, D), BF16)],
        scratch_shapes=[pltpu.VMEM((R, D), F32)],
        compiler_params=_params("arbitrary"), name="proj_ln_step")(a, w_all, res, g, b)


def _ffn_step_body(x_ref, wu_ref, wd_ref, g_ref, b_ref, o_ref, wub_ref, wdb_ref, acc_ref, *, alpha):
    f = pl.program_id(0)

    @pl.when(f == 0)
    def _():
        acc_ref[...] = jnp.zeros_like(acc_ref)

    h = jnp.dot(x_ref[...].astype(BF16), _cast_tile(wu_ref, wub_ref), preferred_element_type=F32)
    h = jnp.square(jnp.maximum(h, 0.0)).astype(BF16)
    acc_ref[...] += jnp.dot(h, _cast_tile(wd_ref, wdb_ref), preferred_element_type=F32)

    @pl.when(f == pl.num_programs(0) - 1)
    def _():
        y = alpha * x_ref[...] + acc_ref[...]
        o_ref[...] = _layer_norm(y, g_ref[...], b_ref[...])


def _ffn_step(x, wu_all, wd_all, layer, g, b, *, alpha, tf):
    R, D = x.shape
    Fd = wu_all.shape[2]
    return pl.pallas_call(
        functools.partial(_ffn_step_body, alpha=alpha),
        grid=(Fd // tf,),
        in_specs=[pl.BlockSpec((R, D), lambda f: (0, 0)),
                  pl.BlockSpec((None, D, tf), lambda f: (layer, 0, f)),
                  pl.BlockSpec((None, tf, D), lambda f: (layer, f, 0)),
                  pl.BlockSpec((1, D), lambda f: (0, 0)),
                  pl.BlockSpec((1, D), lambda f: (0, 0))],
        out_specs=[pl.BlockSpec((R, D), lambda f: (0, 0)),
                   pl.BlockSpec((D, tf), lambda f: (0, f)),
                   pl.BlockSpec((tf, D), lambda f: (f, 0))],
        out_shape=[jax.ShapeDtypeStruct((R, D), F32),
                   jax.ShapeDtypeStruct((D, Fd), BF16),
                   jax.ShapeDtypeStruct((Fd, D), BF16)],
        scratch_shapes=[pltpu.VMEM((R, D), F32)],
        compiler_params=_params("arbitrary"), name="ffn_step")(x, wu_all, wd_all, g, b)


def _pool_prompt_body(u_ref, wp_ref, sc_ref, y_ref, ext_ref, *, T, gc):
    t = pl.program_id(1)

    @pl.when(t == 0)
    def _():
        ext_ref[0:POOL_HALO, :] = jnp.zeros((POOL_HALO, ext_ref.shape[1]), F32)

    ext_ref[POOL_HALO:POOL_HALO + T, :] = u_ref[...]
    pos = t * T + lax.broadcasted_iota(jnp.int32, (T, 1), 0)
    for gi, w in enumerate(POOL_WINDOWS):
        c0, c1 = gi * gc, (gi + 1) * gc
        tok = ext_ref[POOL_HALO:POOL_HALO + T, c0:c1]
        acc = tok
        for k in range(1, w):
            acc = acc + ext_ref[POOL_HALO - k:POOL_HALO - k + T, c0:c1]
        cnt = jnp.minimum(w, pos + 1).astype(F32)
        d = acc / cnt - tok
        y = jnp.dot(d.astype(BF16), wp_ref[gi], preferred_element_type=F32) * sc_ref[:, c0:c1]
        y_ref[:, c0:c1] = y.astype(y_ref.dtype)
    ext_ref[0:POOL_HALO, :] = ext_ref[T:T + POOL_HALO, :]


def _pool_prompt(z, wp_all, layer, sc, *, B, S, d_pool, T):
    nt = S // T
    gc = d_pool // len(POOL_WINDOWS)
    return pl.pallas_call(
        functools.partial(_pool_prompt_body, T=T, gc=gc),
        grid=(B, nt),
        in_specs=[pl.BlockSpec((T, d_pool), lambda b, t: (b * nt + t, 0)),
                  pl.BlockSpec((None,) + wp_all.shape[1:], lambda b, t: (layer, 0, 0, 0)),
                  pl.BlockSpec((1, d_pool), lambda b, t: (0, 0))],
        out_specs=pl.BlockSpec((T, d_pool), lambda b, t: (b * nt + t, 0)),
        out_shape=jax.ShapeDtypeStruct((B * S, d_pool), BF16),
        scratch_shapes=[pltpu.VMEM((POOL_HALO + T, d_pool), F32)],
        compiler_params=_params("parallel", "arbitrary"), name="pool_prompt")(z, wp_all, sc)


def _pool_step_body(buf_ref, u_ref, wp_ref, sc_ref, y_ref, *, gc, pos0):
    u = u_ref[...]
    for gi, w in enumerate(POOL_WINDOWS):
        c0, c1 = gi * gc, (gi + 1) * gc
        tok = u[:, c0:c1]
        acc = tok
        for k in range(1, w):
            acc = acc + buf_ref[POOL_BUF - k, :, c0:c1]
        d = acc / float(min(w, pos0 + 1)) - tok
        y = jnp.dot(d.astype(BF16), wp_ref[gi], preferred_element_type=F32) * sc_ref[:, c0:c1]
        y_ref[:, c0:c1] = y.astype(y_ref.dtype)


def _pool_step(buf_t, u, wp, sc, *, pos0):
    Bs, d_pool = u.shape
    gc = d_pool // len(POOL_WINDOWS)
    return pl.pallas_call(
        functools.partial(_pool_step_body, gc=gc, pos0=pos0),
        out_shape=jax.ShapeDtypeStruct((Bs, d_pool), BF16),
        compiler_params=pltpu.CompilerParams(vmem_limit_bytes=VMEM_LIMIT),
        name="pool_step")(buf_t, u, wp, sc)


def _head_norm_gate(h, o, g):
    mu = jnp.mean(h, axis=-1, keepdims=True)
    hc = h - mu
    var = jnp.mean(hc * hc, axis=-1, keepdims=True)
    return hc * lax.rsqrt(var + HEAD_NORM_EPS) * g * jax.nn.sigmoid(o)


def _mlstm_prompt_body(q_ref, k_ref, v_ref, o_ref, gt_ref, bias_ref, hg_ref,
                       hn_ref, C_ref, n_ref, m_ref, *, L, dk):
    c = pl.program_id(1)

    @pl.when(c == 0)
    def _():
        C_ref[...] = jnp.zeros_like(C_ref)
        n_ref[...] = jnp.zeros_like(n_ref)
        m_ref[...] = jnp.zeros_like(m_ref)

    gt = gt_ref[...] + bias_ref[...]
    gT = gt.T
    row = lax.broadcasted_iota(jnp.int32, (L, L), 0)
    col = lax.broadcasted_iota(jnp.int32, (L, L), 1)
    causal = col <= row
    neg_inf = jnp.float32(-jnp.inf)
    k_scale = dk ** -0.5
    for h in range(MH):
        c0, c1 = h * dk, (h + 1) * dk
        ig_col = gt[:, h:h + 1]
        lf_col = _log_sigmoid(gt[:, MH + h:MH + h + 1])
        ig_row = gT[h:h + 1, :]
        lf_row = _log_sigmoid(gT[MH + h:MH + h + 1, :])
        b_col = jnp.sum(jnp.where(causal, lf_row, 0.0), axis=1, keepdims=True)
        b_row = jnp.sum(jnp.where(row <= col, lf_col, 0.0), axis=0, keepdims=True)
        a_row = ig_row - b_row
        a_col = ig_col - b_col
        cmax = jnp.max(jnp.where(causal, a_row, neg_inf), axis=1, keepdims=True)
        m_prev = m_ref[0, h, :, 0:1]
        m_t = b_col + jnp.maximum(m_prev, cmax)
        w_inter = jnp.exp(b_col + m_prev - m_t)
        dmat = jnp.exp(jnp.where(causal, a_row + b_col - m_t, neg_inf))
        q = q_ref[:, c0:c1]
        ks = k_ref[:, c0:c1] * k_scale
        qb = q.astype(BF16)
        kb = ks.astype(BF16)
        vb = v_ref[:, c0:c1].astype(BF16)
        s = lax.dot_general(qb, kb, (((1,), (1,)), ((), ())), preferred_element_type=F32) * dmat
        C_prev = C_ref[0, h]
        n_prev = n_ref[0, h]
        num = (w_inter * jnp.dot(qb, C_prev.astype(BF16), preferred_element_type=F32)
               + jnp.dot(s.astype(BF16), vb, preferred_element_type=F32))
        den = (w_inter * jnp.sum(q * n_prev, axis=1, keepdims=True)
               + jnp.sum(s, axis=1, keepdims=True))
        hh = num * (1.0 / jnp.maximum(jnp.abs(den), jnp.exp(-m_t)))
        hn_ref[:, c0:c1] = _head_norm_gate(hh, o_ref[:, c0:c1], hg_ref[:, c0:c1]).astype(hn_ref.dtype)

        m_new = m_t[L - 1:L, :]
        b_last = b_col[L - 1:L, :]
        w_old = jnp.exp(b_last + m_prev - m_new)
        kw = ks * jnp.exp(a_col + b_last - m_new)
        C_ref[0, h] = w_old * C_prev + lax.dot_general(
            kw.astype(BF16), vb, (((0,), (0,)), ((), ())), preferred_element_type=F32)
        n_ref[0, h] = w_old * n_prev + jnp.sum(kw, axis=0, keepdims=True)
        m_ref[0, h] = jnp.broadcast_to(m_new, (1, LANES))


def _mlstm_prompt(z, gates, bias, hg, *, B, S, d_pool, d_ml, L):
    nc = S // L
    dk = d_ml // MH
    blk0 = d_pool // d_ml
    assert blk0 * d_ml == d_pool

    def zspec(part):
        return pl.BlockSpec((L, d_ml), lambda b, c: (b * nc + c, blk0 + part))

    return pl.pallas_call(
        functools.partial(_mlstm_prompt_body, L=L, dk=dk),
        grid=(B, nc),
        in_specs=[zspec(0), zspec(1), zspec(2), zspec(3),
                  pl.BlockSpec((L, LANES), lambda b, c: (b * nc + c, 0)),
                  pl.BlockSpec((1, LANES), lambda b, c: (0, 0)),
                  pl.BlockSpec((1, d_ml), lambda b, c: (0, 0))],
        out_specs=[pl.BlockSpec((L, d_ml), lambda b, c: (b * nc + c, 0)),
                   pl.BlockSpec((1, MH, dk, dk), lambda b, c: (b, 0, 0, 0)),
                   pl.BlockSpec((1, MH, 1, dk), lambda b, c: (b, 0, 0, 0)),
                   pl.BlockSpec((1, MH, 1, LANES), lambda b, c: (b, 0, 0, 0))],
        out_shape=[jax.ShapeDtypeStruct((B * S, d_ml), BF16),
                   jax.ShapeDtypeStruct((B, MH, dk, dk), F32),
                   jax.ShapeDtypeStruct((B, MH, 1, dk), F32),
                   jax.ShapeDtypeStruct((B, MH, 1, LANES), F32)],
        compiler_params=_params("parallel", "arbitrary"), name="mlstm_prompt")(
            z, z, z, z, gates, bias, hg)


def _row_to_col(x_row, n):
    r = lax.broadcasted_iota(jnp.int32, (n, n), 0)
    c = lax.broadcasted_iota(jnp.int32, (n, n), 1)
    return jnp.sum(jnp.where(r == c, x_row, 0.0), axis=1, keepdims=True)


def _mlstm_step_body(z_ref, gt_ref, bias_ref, hg_ref, C0_ref, n0_ref, m0_ref,
                     hn_ref, C_ref, n_ref, m_ref, *, d_pool, d_ml, dk):
    gt = gt_ref[0] + bias_ref[...]
    k_scale = dk ** -0.5
    for h in range(MH):
        base = d_pool + h * dk
        q = z_ref[0, :, base:base + dk]
        ks = z_ref[0, :, base + d_ml:base + d_ml + dk] * k_scale
        v = z_ref[0, :, base + 2 * d_ml:base + 2 * d_ml + dk]
        o = z_ref[0, :, base + 3 * d_ml:base + 3 * d_ml + dk]
        ig = gt[:, h:h + 1]
        lf = _log_sigmoid(gt[:, MH + h:MH + h + 1])
        m_prev = m0_ref[0, h]
        m_new = jnp.maximum(m_prev + lf, ig)
        w_old = jnp.exp(lf + m_prev - m_new)
        w_s = jnp.exp(ig - m_new)
        C_new = w_old * C0_ref[0, h] + _row_to_col(ks * w_s, dk) * v
        n_new = w_old * n0_ref[0, h] + w_s * ks
        num = jnp.sum(_row_to_col(q, dk) * C_new, axis=0, keepdims=True)
        den = jnp.sum(q * n_new, axis=1, keepdims=True)
        hh = num / jnp.maximum(jnp.abs(den), jnp.exp(-m_new))
        c0 = h * dk
        hn_ref[0, :, c0:c0 + dk] = _head_norm_gate(hh, o, hg_ref[:, c0:c0 + dk]).astype(hn_ref.dtype)
        C_ref[0, h] = C_new
        n_ref[0, h] = n_new
        m_ref[0, h] = m_new


def _mlstm_step(z3, g3, bias, hg, C0, n0, m0, *, d_pool, d_ml):
    Bs = C0.shape[0]
    dk = d_ml // MH
    dz = z3.shape[-1]
    st = lambda *shape: pl.BlockSpec((1,) + shape, lambda b: (b,) + (0,) * len(shape))
    return pl.pallas_call(
        functools.partial(_mlstm_step_body, d_pool=d_pool, d_ml=d_ml, dk=dk),
        grid=(Bs,),
        in_specs=[st(1, dz), st(1, LANES),
                  pl.BlockSpec((1, LANES), lambda b: (0, 0)),
                  pl.BlockSpec((1, d_ml), lambda b: (0, 0)),
                  st(MH, dk, dk), st(MH, 1, dk), st(MH, 1, 1)],
        out_specs=[st(1, d_ml), st(MH, dk, dk), st(MH, 1, dk), st(MH, 1, 1)],
        out_shape=[jax.ShapeDtypeStruct((Bs, 1, d_ml), BF16),
                   jax.ShapeDtypeStruct((Bs, MH, dk, dk), F32),
                   jax.ShapeDtypeStruct((Bs, MH, 1, dk), F32),
                   jax.ShapeDtypeStruct((Bs, MH, 1, 1), F32)],
        compiler_params=_params("parallel"), name="mlstm_step")(z3, g3, bias, hg, C0, n0, m0)


KM_ROWS = 16


F32_BIG = 1.0e30


def _moba_seq_body(q_ref, k_ref, v_ref, o_ref, kb_ref, vT_ref, km_ref, qT_ref, pT_ref,
                   m_ref, l_ref, acc_ref, *, nblk, hd):
    blk = MOBA_BLOCK
    R = GROUP * blk
    nchunk = R // LANES
    c_exp = hd ** -0.5 * LOG2E
    neg_inf = jnp.float32(-jnp.inf)

    kb_ref[...] = k_ref[...].astype(BF16)
    vT_ref[...] = v_ref[...].T.astype(BF16)
    km = [jnp.sum(k_ref[n * blk:(n + 1) * blk, :], axis=0, keepdims=True) * (1.0 / blk) for n in range(nblk)]
    km.append(jnp.zeros((KM_ROWS - nblk, hd), F32))
    km_ref[...] = jnp.concatenate(km, axis=0).astype(BF16)
    sub = lax.broadcasted_iota(jnp.int32, (nblk, R), 0)

    def own_block(j, buf, p_rows_ref):
        ms, ls = [], []
        for c in range(nchunk):
            lanes = slice(c * LANES, (c + 1) * LANES)
            q0 = (c * LANES) % blk
            nk = q0 + LANES
            s = jnp.dot(kb_ref[j * blk:j * blk + nk, :], qT_ref[buf, :, lanes], preferred_element_type=F32)
            key = lax.broadcasted_iota(jnp.int32, (nk, LANES), 0)
            qry = lax.broadcasted_iota(jnp.int32, (nk, LANES), 1) + q0
            s = jnp.where(key <= qry, s, neg_inf)
            m = jnp.max(s, axis=0, keepdims=True)
            p = jnp.exp2((s - m) * c_exp)
            ms.append(m)
            ls.append(jnp.sum(p, axis=0, keepdims=True))
            p_rows_ref[0:nk, lanes] = p.astype(BF16)
            if nk < blk:
                p_rows_ref[nk:blk, lanes] = jnp.zeros((blk - nk, LANES), BF16)
        return ms, ls

    def write_out(j, acc, l_all):
        inv_l = 1.0 / l_all
        for g in range(GROUP):
            cols = slice(g * blk, (g + 1) * blk)
            o_ref[j * blk:(j + 1) * blk, g * hd:(g + 1) * hd] = (acc[:, cols] * inv_l[:, cols]).T.astype(o_ref.dtype)

    for j in range(nblk):
        buf = j % 2
        for g in range(GROUP):
            qT_ref[buf, :, g * blk:(g + 1) * blk] = (
                q_ref[j * blk:(j + 1) * blk, g * hd:(g + 1) * hd].astype(F32).T.astype(BF16))

        sel = None
        if j > MOBA_TOPK:
            gate = jnp.dot(km_ref[...], qT_ref[buf], preferred_element_type=F32)[0:nblk, :]
            rows = []
            for n in range(j):
                gn = gate[n:n + 1, :]
                beats = ((gate > gn) | ((gate == gn) & (sub < n))) & (sub < j)
                rows.append((jnp.sum(beats.astype(F32), axis=0, keepdims=True) < MOBA_TOPK).astype(F32))
            sel = rows

        ms, ls = own_block(j, buf, pT_ref.at[buf, j * blk:(j + 1) * blk])
        for n in range(j):
            kb = kb_ref[n * blk:(n + 1) * blk, :]
            for g in range(GROUP):
                s2 = jnp.dot(kb, qT_ref[buf, :, g * blk:(g + 1) * blk], preferred_element_type=F32)
                for half in range(blk // LANES):
                    c = g * (blk // LANES) + half
                    lanes = slice(c * LANES, (c + 1) * LANES)
                    s = s2[:, half * LANES:(half + 1) * LANES]
                    if sel is not None:
                        s = jnp.where(sel[n][:, lanes] > 0.5, s, neg_inf)
                    p = jnp.exp2((s - ms[c]) * c_exp)
                    ls[c] = ls[c] + jnp.sum(p, axis=0, keepdims=True)
                    pT_ref[buf, n * blk:(n + 1) * blk, lanes] = p.astype(BF16)
        nkeys = (j + 1) * blk
        acc = jnp.dot(vT_ref[:, 0:nkeys], pT_ref[buf, 0:nkeys, :], preferred_element_type=F32)
        l_all = jnp.concatenate(ls, axis=1)
        write_out(j, acc, l_all)

        if j > 0:
            @pl.when(jnp.logical_not(jnp.max(l_all) < F32_BIG))
            def _(j=j, buf=buf, sel=sel):
                ms, ls = own_block(j, buf, pT_ref.at[buf, 0:blk])
                m_ref[...] = jnp.concatenate(ms, axis=1)
                l_ref[...] = jnp.concatenate(ls, axis=1)
                acc_ref[...] = jnp.dot(vT_ref[:, j * blk:(j + 1) * blk], pT_ref[buf, 0:blk, :],
                                       preferred_element_type=F32)
                selmat = None if sel is None else jnp.concatenate(
                    sel + [jnp.zeros((nblk - j, R), F32)], axis=0)

                def body(n, carry):
                    start = pl.multiple_of(n * blk, blk)
                    kb = kb_ref[pl.ds(start, blk), :]
                    vtb = vT_ref[:, pl.ds(start, blk)]
                    if selmat is not None:
                        sel_n = jnp.sum(jnp.where(sub == n, selmat, 0.0), axis=0, keepdims=True)
                    for c in range(nchunk):
                        lanes = slice(c * LANES, (c + 1) * LANES)
                        s = jnp.dot(kb, qT_ref[buf, :, lanes], preferred_element_type=F32)
                        if selmat is not None:
                            s = jnp.where(sel_n[:, lanes] > 0.5, s, neg_inf)
                        m_old = m_ref[:, lanes]
                        m_new = jnp.maximum(m_old, jnp.max(s, axis=0, keepdims=True))
                        alpha = jnp.exp2((m_old - m_new) * c_exp)
                        p = jnp.exp2((s - m_new) * c_exp)
                        m_ref[:, lanes] = m_new
                        l_ref[:, lanes] = alpha * l_ref[:, lanes] + jnp.sum(p, axis=0, keepdims=True)
                        acc_ref[:, lanes] = acc_ref[:, lanes] * alpha + jnp.dot(
                            vtb, p.astype(BF16), preferred_element_type=F32)
                    return carry

                lax.fori_loop(0, j, body, 0)
                write_out(j, acc_ref[...], l_ref[...])


def _moba_seq(q, k, v, *, B, S, hd):
    assert S % MOBA_BLOCK == 0 and S // MOBA_BLOCK <= KM_ROWS
    nq = S // MOBA_BLOCK
    gw = GROUP * hd
    R = GROUP * MOBA_BLOCK
    return pl.pallas_call(
        functools.partial(_moba_seq_body, nblk=nq, hd=hd),
        grid=(B, N_KV_HEADS),
        in_specs=[pl.BlockSpec((S, gw), lambda b, h: (b, h)),
                  pl.BlockSpec((S, hd), lambda b, h: (b, h)),
                  pl.BlockSpec((S, hd), lambda b, h: (b, h))],
        out_specs=pl.BlockSpec((S, gw), lambda b, h: (b, h)),
        out_shape=jax.ShapeDtypeStruct((B * S, N_HEADS * hd), BF16),
        scratch_shapes=[pltpu.VMEM((S, hd), BF16),
                        pltpu.VMEM((hd, S), BF16),
                        pltpu.VMEM((KM_ROWS, hd), BF16),
                        pltpu.VMEM((2, hd, R), BF16),
                        pltpu.VMEM((2, S, R), BF16),
                        pltpu.VMEM((1, R), F32),
                        pltpu.VMEM((1, R), F32),
                        pltpu.VMEM((hd, R), F32)],
        compiler_params=_params("parallel", "parallel"), name="moba_seq")(q, k, v)


PAGES_PER_STEP = 16
PAGES_PER_BLOCK = MOBA_BLOCK // PAGE_SIZE
TOKENS_PER_TILE = SUBLANES // N_KV_HEADS
KV_SHIFT = N_KV_HEADS.bit_length() - 1
GROUP_SHIFT = GROUP.bit_length() - 1
assert 1 << KV_SHIFT == N_KV_HEADS and 1 << GROUP_SHIFT == GROUP and TOKENS_PER_TILE * N_KV_HEADS == SUBLANES


def _block_mean_keys(page_refs, o_ref):
    rows, hd = page_refs[0].shape[2:]
    for blk in range(PAGES_PER_STEP // PAGES_PER_BLOCK):
        acc = None
        for r in range(PAGES_PER_BLOCK):
            pg = page_refs[blk * PAGES_PER_BLOCK + r][0, 0]
            part = jnp.sum(pg.reshape(rows // SUBLANES, SUBLANES, hd), axis=0)
            acc = part if acc is None else acc + part
        per_head = acc[0:N_KV_HEADS]
        for t in range(1, TOKENS_PER_TILE):
            per_head = per_head + acc[t * N_KV_HEADS:(t + 1) * N_KV_HEADS]
        o_ref[0, 0, blk * N_KV_HEADS:(blk + 1) * N_KV_HEADS, :] = per_head * (1.0 / MOBA_BLOCK)


def _topk_body(q_ref, km_ref, idx_ref, *, ncol):
    qb = q_ref[0].astype(BF16)
    km = km_ref[0, 0].astype(BF16)
    gate = lax.dot_general(qb, km, (((1,), (1,)), ((), ())), preferred_element_type=F32)
    head = lax.broadcasted_iota(jnp.int32, (N_HEADS, ncol), 0)
    col = lax.broadcasted_iota(jnp.int32, (N_HEADS, ncol), 1)
    gate = jnp.where((col & (N_KV_HEADS - 1)) == (head >> GROUP_SHIFT), gate, -jnp.inf)
    out_lane = lax.broadcasted_iota(jnp.int32, (N_HEADS, LANES), 1)
    out = jnp.zeros((N_HEADS, LANES), jnp.int32)
    for r in range(MOBA_TOPK):
        mx = jnp.max(gate, axis=1, keepdims=True)
        idx = jnp.min(jnp.where(gate == mx, col, ncol), axis=1, keepdims=True)
        out = jnp.where(out_lane == r, idx >> KV_SHIFT, out)
        gate = jnp.where(col == idx, -jnp.inf, gate)
    idx_ref[0] = out


def _topk_blocks(q3, kmean, layer):
    Bs, _, hd = q3.shape
    ncol = kmean.shape[2]
    assert ncol >= MOBA_TOPK * N_KV_HEADS
    return pl.pallas_call(
        functools.partial(_topk_body, ncol=ncol),
        grid=(Bs,),
        in_specs=[pl.BlockSpec((1, N_HEADS, hd), lambda b: (b, 0, 0)),
                  pl.BlockSpec((1, 1, ncol, hd), lambda b: (layer, b, 0, 0))],
        out_specs=pl.BlockSpec((1, N_HEADS, LANES), lambda b: (b, 0, 0)),
        out_shape=jax.ShapeDtypeStruct((Bs, N_HEADS, LANES), jnp.int32),
        compiler_params=_params("parallel"), name="topk_blocks")(q3, kmean)


N_SEL_PAGES = MOBA_TOPK * PAGES_PER_BLOCK


def _moba_step_body(idx_ref, pt_ref, q_ref, kn_ref, vn_ref, *refs, hd):
    n_in = GROUP * N_SEL_PAGES
    k_refs, v_refs, o_ref = refs[:n_in], refs[n_in:2 * n_in], refs[2 * n_in]
    kv_head = pl.program_id(1)
    scale = hd ** -0.5
    rows = SAMPLE_ROWS
    ncol = k_refs[0].shape[2]
    mine = (lax.broadcasted_iota(jnp.int32, (rows, ncol), 1) & (N_KV_HEADS - 1)) == kv_head
    k_new = kn_ref[0, 0].astype(BF16).astype(F32)
    v_new = vn_ref[0, 0].astype(BF16).astype(F32)
    for g in range(GROUP):
        qb = jnp.broadcast_to(q_ref[0, g], (rows, hd)).astype(BF16)
        pages = slice(g * N_SEL_PAGES, (g + 1) * N_SEL_PAGES)
        s_pages = [jnp.where(mine,
                             lax.dot_general(qb, kr[0, 0].astype(BF16), (((1,), (1,)), ((), ())),
                                             preferred_element_type=F32) * scale,
                             -jnp.inf) for kr in k_refs[pages]]
        s_own = jnp.sum(qb.astype(F32) * k_new, axis=1, keepdims=True) * scale
        m = s_own
        for s in s_pages:
            m = jnp.maximum(m, jnp.max(s, axis=1, keepdims=True))
        p_own = jnp.exp(s_own - m)
        l = p_own
        acc = p_own * v_new
        for s, vr in zip(s_pages, v_refs[pages]):
            p = jnp.exp(s - m)
            l = l + jnp.sum(p, axis=1, keepdims=True)
            acc = acc + jnp.dot(p.astype(BF16), vr[0, 0].astype(BF16), preferred_element_type=F32)
        o_ref[0, g] = (acc / l)[0:1, :].astype(o_ref.dtype)


def _moba_step(q4, kn4, vn4, cache_k4, cache_v4, idx_flat, pt_flat, *, layer, n_pages):
    Bs, _, _, hd = q4.shape
    rows = cache_k4.shape[2]

    def page_spec(r):
        g, r = divmod(r, N_SEL_PAGES)
        sel, half = divmod(r, PAGES_PER_BLOCK)

        def index_map(b, h, idx, pt):
            blk = idx[(b * N_HEADS + h * GROUP + g) * MOBA_TOPK + sel]
            return (layer, pt[b * n_pages + blk * PAGES_PER_BLOCK + half], 0, 0)

        return pl.BlockSpec((1, 1, rows, hd), index_map)

    group_spec = pl.BlockSpec((1, GROUP, 1, hd), lambda b, h, idx, pt: (b, h, 0, 0))
    kv_spec = pl.BlockSpec((1, 1, 1, hd), lambda b, h, idx, pt: (b, h, 0, 0))
    n_in = GROUP * N_SEL_PAGES
    return pl.pallas_call(
        functools.partial(_moba_step_body, hd=hd),
        grid_spec=pltpu.PrefetchScalarGridSpec(
            num_scalar_prefetch=2, grid=(Bs, N_KV_HEADS),
            in_specs=[group_spec, kv_spec, kv_spec] + [page_spec(r) for r in range(n_in)] * 2,
            out_specs=group_spec),
        out_shape=jax.ShapeDtypeStruct((Bs, N_HEADS, 1, hd), BF16),
        compiler_params=_params("parallel", "arbitrary"), name="moba_step")(
            idx_flat, pt_flat, q4, kn4, vn4, *([cache_k4] * n_in), *([cache_v4] * n_in))


def _pad_rows(a, rows):
    return jnp.pad(a, ((0, rows - a.shape[0]),) + ((0, 0),) * (a.ndim - 1))


def kernel(x_prompt, x_sample, cache_k, cache_v, state_pool, state_C, state_n, state_m, page_table,
           w_in_mix, b_if, w_pool, pool_scale, mlstm_norm_g, w_out_mix, w_qkv, w_o,
           ln_mix_g, ln_mix_b, ln_ffn_g, ln_ffn_b, w_up, w_down):
    B, S, D = x_prompt.shape
    Bs = x_sample.shape[0]
    assert x_sample.shape[1] == 1 and Bs <= SAMPLE_ROWS
    depth = w_up.shape[0]
    d_pool = state_pool.shape[-1]
    dk = state_C.shape[-1]
    d_ml = MH * dk
    d_main = d_pool + 4 * d_ml
    assert w_in_mix.shape[2] == d_main + 2 * MH
    hd = cache_k.shape[-1]
    d_q, d_kv = N_HEADS * hd, N_KV_HEADS * hd
    n_pages = page_table.shape[1]
    past_len = n_pages * PAGE_SIZE
    assert cache_k.shape[2] == PAGE_SIZE and past_len % MOBA_BLOCK == 0
    alpha = (2 * depth) ** 0.25
    M = B * S
    L = math.gcd(S, MLSTM_CHUNK)

    w_in_t = jnp.swapaxes(w_in_mix, 1, 2)
    bias_g = jnp.pad(b_if, ((0, 0), (0, LANES - 2 * MH)))[:, None, :]
    w_pool_b = w_pool.astype(BF16)

    cache_k4 = cache_k.reshape(cache_k.shape[:2] + (PAGE_SIZE * N_KV_HEADS, hd))
    cache_v4 = cache_v.reshape(cache_v.shape[:2] + (PAGE_SIZE * N_KV_HEADS, hd))
    pt_flat = page_table.reshape(-1)
    kmean_past = None

    xs = _pad_rows(x_sample.reshape(Bs, D), SAMPLE_ROWS)
    xp = x_prompt.reshape(M, D)
    ks_, vs_, pool_s, C_s, n_s, m_s = [], [], [], [], [], []
    pool_p, C_p, n_p, m_p = [], [], [], []
    n_attn = cache_k.shape[0]
    kv_flat = None
    for l in range(depth):
        i = l // 2
        ln_g, ln_b = ln_mix_g[l][None, :], ln_mix_b[l][None, :]
        fg, fb = ln_ffn_g[l][None, :], ln_ffn_b[l][None, :]
        if l % 2 == 0:
            sc = pool_scale[i][None, :]
            hg = mlstm_norm_g[i][None, :]
            zs, gs, wb, wgb = _proj_gates_step(xs, w_in_t, i, n_main=d_main, n_gate=2 * MH, tn=1024)
            us = zs[:Bs, :d_pool]
            ys_pool = _pool_step(jnp.transpose(state_pool[i], (1, 0, 2)), us, w_pool_b[i], sc, pos0=past_len)
            hns, C, n, m = _mlstm_step(
                zs[:Bs, None, :], gs[:Bs, None, :], bias_g[i], hg, state_C[i],
                state_n[i][:, :, None, :], state_m[i][:, :, None, None], d_pool=d_pool, d_ml=d_ml)
            a = _pad_rows(jnp.concatenate([ys_pool, hns[:, 0, :]], axis=1), SAMPLE_ROWS)
            xs, wpb = _proj_ln_step(a, w_out_mix, i, xs, ln_g, ln_b, alpha=alpha, tk=512)
            pool_s.append(jnp.concatenate([state_pool[i][:, 1:, :], us[:, None, :]], axis=1))
            C_s.append(C)
            n_s.append(n[:, :, 0, :])
            m_s.append(m[:, :, 0, 0])
        else:
            qs, k_new, v_new, wb = _proj_qkv_step(xs, w_qkv, i, d_q=d_q, d_kv=d_kv)
            q_s = qs[:Bs].astype(F32).reshape(Bs, N_HEADS, hd)
            k_new = k_new[:Bs].reshape(Bs, N_KV_HEADS, 1, hd)
            v_new = v_new[:Bs].reshape(Bs, N_KV_HEADS, 1, hd)
            idx = _topk_blocks(q_s, kmean_past, i)[:, :, :MOBA_TOPK].reshape(-1)
            os_ = _moba_step(q_s[:, :, None, :], k_new, v_new, cache_k4, cache_v4, idx, pt_flat,
                             layer=i, n_pages=n_pages)
            xs, wpb = _proj_ln_step(_pad_rows(os_.reshape(Bs, d_q), SAMPLE_ROWS), w_o, i, xs, ln_g, ln_b,
                                    alpha=alpha, tk=512)
            ks_.append(k_new.reshape(Bs, 1, N_KV_HEADS, hd))
            vs_.append(v_new.reshape(Bs, 1, N_KV_HEADS, hd))
        xs, wub, wdb = _ffn_step(xs, w_up, w_down, l, fg, fb, alpha=alpha, tf=512)

        if l % 2 == 0:
            z, gates = _proj_gates(xp, wb, wgb, tm=1024, tn=1024)
            y_pool = _pool_prompt(z, w_pool_b, i, sc, B=B, S=S, d_pool=d_pool, T=256)
            hn, C, n, m = _mlstm_prompt(z, gates, bias_g[i], hg, B=B, S=S, d_pool=d_pool, d_ml=d_ml, L=L)
            xp = _proj_ln([y_pool, hn], wpb[None], 0, xp, ln_g, ln_b, alpha=alpha, tm=512)
            pool_p.append(z.reshape(B, S, d_main)[:, S - POOL_BUF:, :d_pool])
            C_p.append(C)
            n_p.append(n.reshape(B, MH, dk))
            m_p.append(m[:, :, 0, 0])
        else:
            q, k, v, *kv_flat = _proj_qkv(xp, wb[None], 0, d_q=d_q, d_kv=d_kv, hd=hd, tm=1024,
                                          slab=i, n_slabs=n_attn, prev=kv_flat)
            o = _moba_seq(q, k, v, B=B, S=S, hd=hd)
            xp = _proj_ln([o], wpb[None], 0, xp, ln_g, ln_b, alpha=alpha, tm=512)
        if kmean_past is None:
            xp, kmean_past = _ffn(xp, wub[None], wdb[None], 0, fg, fb, alpha=alpha, tm=512, tf=1024,
                                  paged=(cache_k4, pt_flat, Bs, n_pages))
        else:
            xp = _ffn(xp, wub[None], wdb[None], 0, fg, fb, alpha=alpha, tm=512, tf=1024)

    k_prompt, v_prompt = (t.reshape(n_attn, B, S, N_KV_HEADS, hd) for t in kv_flat)
    return (xp.reshape(B, S, D), xs[:Bs].reshape(Bs, 1, D),
            k_prompt, v_prompt, jnp.stack(ks_), jnp.stack(vs_),
            jnp.stack(pool_p), jnp.stack(C_p), jnp.stack(n_p), jnp.stack(m_p),
            jnp.stack(pool_s), jnp.stack(C_s), jnp.stack(n_s), jnp.stack(m_s))
```

```python
import functools
import math

import jax
import jax.numpy as jnp
from jax import lax
from jax.experimental import pallas as pl
from jax.experimental.pallas import tpu as pltpu

F32 = jnp.float32
BF16 = jnp.bfloat16

POOL_WINDOWS = (2, 4, 8, 16)
POOL_BUF = max(POOL_WINDOWS) - 1
POOL_HALO = 16
MH = 4
N_HEADS = 16
N_KV_HEADS = 4
GROUP = N_HEADS // N_KV_HEADS
MOBA_BLOCK = 256
MOBA_TOPK = 3
PAGE_SIZE = 128
LN_EPS = 1e-5
HEAD_NORM_EPS = 1e-6
LANES = 128
SUBLANES = 8
SAMPLE_ROWS = 16
LOG2E = 1.4426950408889634

VMEM_LIMIT = 56 * 1024 * 1024
MLSTM_CHUNK = 256
MLSTM_SEQS_PER_STEP = 2
LN_SUB_ROWS = 256


def _params(*sem):
    return pltpu.CompilerParams(dimension_semantics=sem, vmem_limit_bytes=VMEM_LIMIT)


def _layer_norm(y, g, b):
    mu = jnp.mean(y, axis=-1, keepdims=True)
    yc = y - mu
    var = jnp.mean(yc * yc, axis=-1, keepdims=True)
    return yc * lax.rsqrt(var + LN_EPS) * g + b


def _log_sigmoid(x):
    return jnp.minimum(x, 0.0) - jnp.log1p(jnp.exp(-jnp.abs(x)))


_NT = (((1,), (1,)), ((), ()))


def _proj_gates_body(x_ref, w_ref, wg_ref, o_ref, g_ref, xb_ref):
    @pl.when(pl.program_id(1) == 0)
    def _():
        xb = x_ref[...].astype(BF16)
        xb_ref[...] = xb
        g_ref[...] = lax.dot_general(xb, wg_ref[...], _NT, preferred_element_type=F32)

    o_ref[...] = lax.dot_general(xb_ref[...], w_ref[...], _NT, preferred_element_type=F32)


def _proj_gates(x, wt, wgt, *, tm, tn):
    M, K = x.shape
    n_main = wt.shape[0]
    return pl.pallas_call(
        _proj_gates_body, grid=(M // tm, n_main // tn),
        in_specs=[pl.BlockSpec((tm, K), lambda i, j: (i, 0)),
                  pl.BlockSpec((tn, K), lambda i, j: (j, 0)),
                  pl.BlockSpec((LANES, K), lambda i, j: (0, 0))],
        out_specs=[pl.BlockSpec((tm, tn), lambda i, j: (i, j)),
                   pl.BlockSpec((tm, LANES), lambda i, j: (i, 0))],
        out_shape=[jax.ShapeDtypeStruct((M, n_main), F32), jax.ShapeDtypeStruct((M, LANES), F32)],
        scratch_shapes=[pltpu.VMEM((tm, K), BF16)],
        compiler_params=_params("parallel", "arbitrary"), name="proj_gates")(x, wt, wgt)


def _proj_qkv_body(x_ref, w_ref, *rest, nq_tiles, n_alias):
    q_ref, k_ref, v_ref, kf_ref, vf_ref, xb_ref = rest[n_alias:]
    j = pl.program_id(1)

    @pl.when(j == 0)
    def _():
        xb_ref[...] = x_ref[...].astype(BF16)

    y = jnp.dot(xb_ref[...], w_ref[...], preferred_element_type=F32)

    @pl.when(j < nq_tiles)
    def _():
        q_ref[...] = y.astype(q_ref.dtype)

    @pl.when(j == nq_tiles)
    def _():
        tm, d_kv = k_ref.shape
        hd = kf_ref.shape[1]
        nh = d_kv // hd
        k_ref[...] = y[:, :d_kv]
        v_ref[...] = y[:, d_kv:]
        for h in range(nh):
            kf_ref[pl.ds(h, tm, stride=nh), :] = y[:, h * hd:(h + 1) * hd]
            vf_ref[pl.ds(h, tm, stride=nh), :] = y[:, d_kv + h * hd:d_kv + (h + 1) * hd]


def _proj_qkv(x, w_all, layer, *, d_q, d_kv, hd, tm, slab, n_slabs, prev=None):
    M, K = x.shape
    tn = 2 * d_kv
    nq_tiles = d_q // tn
    nh = d_kv // hd
    assert nq_tiles * tn == d_q
    flat_spec = pl.BlockSpec((None, tm * nh, hd), lambda i, j: (slab, i, 0))
    flat_shape = jax.ShapeDtypeStruct((n_slabs, M * nh, hd), F32)
    in_specs = [pl.BlockSpec((tm, K), lambda i, j: (i, 0)),
                pl.BlockSpec((None, K, tn), lambda i, j: (layer, 0, j))]
    args = [x, w_all]
    aliases = {}
    if prev is not None:
        in_specs += [pl.BlockSpec(memory_space=pl.ANY)] * 2
        args += list(prev)
        aliases = {2: 3, 3: 4}
    return pl.pallas_call(
        functools.partial(_proj_qkv_body, nq_tiles=nq_tiles, n_alias=len(aliases)),
        grid=(M // tm, nq_tiles + 1),
        in_specs=in_specs,
        out_specs=[pl.BlockSpec((tm, tn), lambda i, j: (i, jnp.minimum(j, nq_tiles - 1))),
                   pl.BlockSpec((tm, d_kv), lambda i, j: (i, 0)),
                   pl.BlockSpec((tm, d_kv), lambda i, j: (i, 0)),
                   flat_spec, flat_spec],
        out_shape=[jax.ShapeDtypeStruct((M, d_q), BF16),
                   jax.ShapeDtypeStruct((M, d_kv), F32),
                   jax.ShapeDtypeStruct((M, d_kv), F32),
                   flat_shape, flat_shape],
        scratch_shapes=[pltpu.VMEM((tm, K), BF16)],
        input_output_aliases=aliases,
        compiler_params=_params("parallel", "arbitrary"), name="proj_qkv")(*args)


def _proj_ln_body(*refs, n_a, alpha):
    a_refs = refs[:n_a]
    w_ref, res_ref, g_ref, b_ref, o_ref = refs[n_a:]
    tm = res_ref.shape[0]
    sub = min(tm, LN_SUB_ROWS)
    for r0 in range(0, tm, sub):
        rows = slice(r0, r0 + sub)
        y = alpha * res_ref[rows, :]
        k0 = 0
        for a_ref in a_refs:
            ka = a_ref.shape[1]
            y = y + jnp.dot(a_ref[rows, :], w_ref[k0:k0 + ka, :], preferred_element_type=F32)
            k0 += ka
        o_ref[rows, :] = _layer_norm(y, g_ref[...], b_ref[...])


def _proj_ln(a_list, w_all, layer, res, g, b, *, alpha, tm):
    M, D = res.shape
    K = w_all.shape[1]
    in_specs = [pl.BlockSpec((tm, a.shape[1]), lambda i: (i, 0)) for a in a_list]
    in_specs += [pl.BlockSpec((None, K, D), lambda i: (layer, 0, 0)),
                 pl.BlockSpec((tm, D), lambda i: (i, 0)),
                 pl.BlockSpec((1, D), lambda i: (0, 0)),
                 pl.BlockSpec((1, D), lambda i: (0, 0))]
    return pl.pallas_call(
        functools.partial(_proj_ln_body, n_a=len(a_list), alpha=alpha),
        grid=(M // tm,), in_specs=in_specs,
        out_specs=pl.BlockSpec((tm, D), lambda i: (i, 0)),
        out_shape=jax.ShapeDtypeStruct((M, D), F32),
        compiler_params=_params("parallel"), name="proj_ln")(*a_list, w_all, res, g, b)


def _ffn_body(*refs, alpha, n_page_refs, cast_next):
    if n_page_refs:
        refs = refs[1:]
    x_ref, wu_ref, wd_ref, g_ref, b_ref = refs[:5]
    refs = refs[5:]
    page_refs, refs = refs[:n_page_refs], refs[n_page_refs:]
    next_refs, refs = (refs[:2], refs[2:]) if cast_next else ((), refs)
    o_ref, refs = refs[0], refs[1:]
    km_ref, refs = (refs[0], refs[1:]) if n_page_refs else (None, refs)
    next_out_refs, refs = (refs[:2], refs[2:]) if cast_next else ((), refs)
    xb_ref, acc_ref = refs
    f = pl.program_id(1)
    last = pl.num_programs(1) - 1

    def hidden(rows):
        h = jnp.dot(xb_ref[rows, :], wu_ref[...], preferred_element_type=F32)
        h = jnp.square(jnp.maximum(h, 0.0)).astype(BF16)
        return jnp.dot(h, wd_ref[...], preferred_element_type=F32)

    def page_stream():
        if n_page_refs:
            _block_mean_keys(page_refs, km_ref)
        for src, dst in zip(next_refs, next_out_refs):
            dst[...] = src[...].astype(BF16)

    @pl.when(f == 0)
    def _():
        xb_ref[...] = x_ref[...].astype(BF16)
        acc_ref[...] = jnp.zeros_like(acc_ref)

    @pl.when(f < last)
    def _():
        acc_ref[...] += hidden(slice(None))
        page_stream()

    @pl.when(f == last)
    def _():
        page_stream()
        tm = x_ref.shape[0]
        sub = min(tm, LN_SUB_ROWS)
        for r0 in range(0, tm, sub):
            rows = slice(r0, r0 + sub)
            y = alpha * x_ref[rows, :] + (acc_ref[rows, :] + hidden(rows))
            o_ref[rows, :] = _layer_norm(y, g_ref[...], b_ref[...])


def _ffn(x, wu_all, wd_all, layer, g, b, *, alpha, tm, tf, paged=None, cast_next=None):
    M, D = x.shape
    Fd = wu_all.shape[2]
    grid = (M // tm, Fd // tf)
    n_steps = grid[0] * grid[1]
    in_specs = [pl.BlockSpec((tm, D), lambda i, f, *_: (i, 0)),
                pl.BlockSpec((None, D, tf), lambda i, f, *_: (layer, 0, f)),
                pl.BlockSpec((None, tf, D), lambda i, f, *_: (layer, f, 0)),
                pl.BlockSpec((1, D), lambda i, f, *_: (0, 0)),
                pl.BlockSpec((1, D), lambda i, f, *_: (0, 0))]
    args = [x, wu_all, wd_all, g, b]
    out_specs = [pl.BlockSpec((tm, D), lambda i, f, *_: (i, 0))]
    out_shape = [jax.ShapeDtypeStruct((M, D), F32)]
    scratch = [pltpu.VMEM((tm, D), BF16), pltpu.VMEM((tm, D), F32)]
    n_page_refs = 0
    if paged is not None:
        cache_k4, pt_flat, Bs, n_pages = paged
        n_attn, _, rows, hd = cache_k4.shape
        assert grid == (n_attn * Bs, n_pages // PAGES_PER_STEP)
        out_rows = PAGES_PER_STEP // PAGES_PER_BLOCK * N_KV_HEADS
        n_page_refs = PAGES_PER_STEP

        def page_spec(r):
            return pl.BlockSpec(
                (1, 1, rows, hd),
                lambda i, f, pt: (i // Bs, pt[(i % Bs) * n_pages + f * PAGES_PER_STEP + r], 0, 0))

        in_specs += [page_spec(r) for r in range(PAGES_PER_STEP)]
        args += [cache_k4] * PAGES_PER_STEP
        out_specs.append(pl.BlockSpec((1, 1, out_rows, hd), lambda i, f, pt: (i // Bs, i % Bs, f, 0)))
        out_shape.append(jax.ShapeDtypeStruct((n_attn, Bs, grid[1] * out_rows, hd), F32))
    if cast_next is not None:
        wu_next, wd_next, nxt = cast_next
        for w in (wu_next, wd_next):
            rows_w, cols_w = w.shape[1:]
            rs = rows_w // n_steps
            assert rs * n_steps == rows_w and rs % SAMPLE_ROWS == 0
            in_specs.append(pl.BlockSpec((None, rs, cols_w), lambda i, f, *_: (nxt, i * grid[1] + f, 0)))
            args.append(w)
            out_specs.append(pl.BlockSpec((rs, cols_w), lambda i, f, *_: (i * grid[1] + f, 0)))
            out_shape.append(jax.ShapeDtypeStruct((rows_w, cols_w), BF16))
    body = functools.partial(_ffn_body, alpha=alpha, n_page_refs=n_page_refs, cast_next=cast_next is not None)
    riders = paged is not None or cast_next is not None
    params = _params("arbitrary" if riders else "parallel", "arbitrary")
    if paged is None:
        return pl.pallas_call(body, grid=grid, in_specs=in_specs, out_specs=out_specs, out_shape=out_shape,
                              scratch_shapes=scratch, compiler_params=params, name="ffn")(*args)
    return pl.pallas_call(
        body,
        grid_spec=pltpu.PrefetchScalarGridSpec(num_scalar_prefetch=1, grid=grid, in_specs=in_specs,
                                               out_specs=out_specs, scratch_shapes=scratch),
        out_shape=out_shape, compiler_params=params, name="ffn_kmean")(pt_flat, *args)


def _cast_tile(w_ref, wb_ref):
    wb = w_ref[...].astype(BF16)
    wb_ref[...] = wb
    return wb


def _proj_gates_step_body(x_ref, w_ref, wg_ref, o_ref, g_ref, wb_ref, wgb_ref):
    xb = x_ref[...].astype(BF16)

    @pl.when(pl.program_id(0) == 0)
    def _():
        n_gate, K = wg_ref.shape
        wg = jnp.concatenate([wg_ref[...], jnp.zeros((LANES - n_gate, K), F32)], axis=0).astype(BF16)
        wgb_ref[...] = wg
        g_ref[...] = lax.dot_general(xb, wg, _NT, preferred_element_type=F32)

    o_ref[...] = lax.dot_general(xb, _cast_tile(w_ref, wb_ref), _NT, preferred_element_type=F32)


def _proj_gates_step(x, wt_all, layer, *, n_main, n_gate, tn):
    R, K = x.shape
    assert n_main % n_gate == 0 and n_gate % SUBLANES == 0
    return pl.pallas_call(
        _proj_gates_step_body, grid=(n_main // tn,),
        in_specs=[pl.BlockSpec((R, K), lambda j: (0, 0)),
                  pl.BlockSpec((None, tn, K), lambda j: (layer, j, 0)),
                  pl.BlockSpec((None, n_gate, K), lambda j: (layer, n_main // n_gate, 0))],
        out_specs=[pl.BlockSpec((R, tn), lambda j: (0, j)),
                   pl.BlockSpec((R, LANES), lambda j: (0, 0)),
                   pl.BlockSpec((tn, K), lambda j: (j, 0)),
                   pl.BlockSpec((LANES, K), lambda j: (0, 0))],
        out_shape=[jax.ShapeDtypeStruct((R, n_main), F32), jax.ShapeDtypeStruct((R, LANES), F32),
                   jax.ShapeDtypeStruct((n_main, K), BF16), jax.ShapeDtypeStruct((LANES, K), BF16)],
        compiler_params=_params("arbitrary"), name="proj_gates_step")(x, wt_all, wt_all)


def _proj_qkv_step_body(x_ref, w_ref, q_ref, k_ref, v_ref, wb_ref, *, nq_tiles):
    j = pl.program_id(0)
    y = jnp.dot(x_ref[...].astype(BF16), _cast_tile(w_ref, wb_ref), preferred_element_type=F32)

    @pl.when(j < nq_tiles)
    def _():
        q_ref[...] = y.astype(q_ref.dtype)

    @pl.when(j == nq_tiles)
    def _():
        d_kv = k_ref.shape[1]
        k_ref[...] = y[:, :d_kv]
        v_ref[...] = y[:, d_kv:]


def _proj_qkv_step(x, w_all, layer, *, d_q, d_kv):
    R, K = x.shape
    tn = 2 * d_kv
    nq_tiles = d_q // tn
    assert nq_tiles * tn == d_q
    return pl.pallas_call(
        functools.partial(_proj_qkv_step_body, nq_tiles=nq_tiles),
        grid=(nq_tiles + 1,),
        in_specs=[pl.BlockSpec((R, K), lambda j: (0, 0)),
                  pl.BlockSpec((None, K, tn), lambda j: (layer, 0, j))],
        out_specs=[pl.BlockSpec((R, tn), lambda j: (0, jnp.minimum(j, nq_tiles - 1))),
                   pl.BlockSpec((R, d_kv), lambda j: (0, 0)),
                   pl.BlockSpec((R, d_kv), lambda j: (0, 0)),
                   pl.BlockSpec((K, tn), lambda j: (0, j))],
        out_shape=[jax.ShapeDtypeStruct((R, d_q), BF16),
                   jax.ShapeDtypeStruct((R, d_kv), F32),
                   jax.ShapeDtypeStruct((R, d_kv), F32),
                   jax.ShapeDtypeStruct((K, d_q + 2 * d_kv), BF16)],
        compiler_params=_params("arbitrary"), name="proj_qkv_step")(x, w_all)


def _proj_ln_step_body(a_ref, w_ref, res_ref, g_ref, b_ref, o_ref, wb_ref, acc_ref, *, alpha):
    k = pl.program_id(0)

    @pl.when(k == 0)
    def _():
        acc_ref[...] = alpha * res_ref[...]

    acc_ref[...] += jnp.dot(a_ref[...], _cast_tile(w_ref, wb_ref), preferred_element_type=F32)

    @pl.when(k == pl.num_programs(0) - 1)
    def _():
        o_ref[...] = _layer_norm(acc_ref[...], g_ref[...], b_ref[...])


def _proj_ln_step(a, w_all, layer, res, g, b, *, alpha, tk):
    R, D = res.shape
    K = w_all.shape[1]
    return pl.pallas_call(
        functools.partial(_proj_ln_step_body, alpha=alpha),
        grid=(K // tk,),
        in_specs=[pl.BlockSpec((R, tk), lambda k: (0, k)),
                  pl.BlockSpec((None, tk, D), lambda k: (layer, k, 0)),
                  pl.BlockSpec((R, D), lambda k: (0, 0)),
                  pl.BlockSpec((1, D), lambda k: (0, 0)),
                  pl.BlockSpec((1, D), lambda k: (0, 0))],
        out_specs=[pl.BlockSpec((R, D), lambda k: (0, 0)),
                   pl.BlockSpec((tk, D), lambda k: (k, 0))],
        out_shape=[jax.ShapeDtypeStruct((R, D), F32), jax.ShapeDtypeStruct((K, D), BF16)],
        scratch_shapes=[pltpu.VMEM((R, D), F32)],
        compiler_params=_params("arbitrary"), name="proj_ln_step")(a, w_all, res, g, b)


def _ffn_step_body(x_ref, wu_ref, wd_ref, g_ref, b_ref, o_ref, wub_ref, wdb_ref, acc_ref, *, alpha):
    f = pl.program_id(0)

    @pl.when(f == 0)
    def _():
        acc_ref[...] = jnp.zeros_like(acc_ref)

    h = jnp.dot(x_ref[...].astype(BF16), _cast_tile(wu_ref, wub_ref), preferred_element_type=F32)
    h = jnp.square(jnp.maximum(h, 0.0)).astype(BF16)
    acc_ref[...] += jnp.dot(h, _cast_tile(wd_ref, wdb_ref), preferred_element_type=F32)

    @pl.when(f == pl.num_programs(0) - 1)
    def _():
        y = alpha * x_ref[...] + acc_ref[...]
        o_ref[...] = _layer_norm(y, g_ref[...], b_ref[...])


def _ffn_step(x, wu_all, wd_all, layer, g, b, *, alpha, tf):
    R, D = x.shape
    Fd = wu_all.shape[2]
    return pl.pallas_call(
        functools.partial(_ffn_step_body, alpha=alpha),
        grid=(Fd // tf,),
        in_specs=[pl.BlockSpec((R, D), lambda f: (0, 0)),
                  pl.BlockSpec((None, D, tf), lambda f: (layer, 0, f)),
                  pl.BlockSpec((None, tf, D), lambda f: (layer, f, 0)),
                  pl.BlockSpec((1, D), lambda f: (0, 0)),
                  pl.BlockSpec((1, D), lambda f: (0, 0))],
        out_specs=[pl.BlockSpec((R, D), lambda f: (0, 0)),
                   pl.BlockSpec((D, tf), lambda f: (0, f)),
                   pl.BlockSpec((tf, D), lambda f: (f, 0))],
        out_shape=[jax.ShapeDtypeStruct((R, D), F32),
                   jax.ShapeDtypeStruct((D, Fd), BF16),
                   jax.ShapeDtypeStruct((Fd, D), BF16)],
        scratch_shapes=[pltpu.VMEM((R, D), F32)],
        compiler_params=_params("arbitrary"), name="ffn_step")(x, wu_all, wd_all, g, b)


def _pool_prompt_body(u_ref, wp_ref, sc_ref, y_ref, ext_ref, *, T, gc):
    t = pl.program_id(1)

    @pl.when(t == 0)
    def _():
        ext_ref[0:POOL_HALO, :] = jnp.zeros((POOL_HALO, ext_ref.shape[1]), F32)

    ext_ref[POOL_HALO:POOL_HALO + T, :] = u_ref[...]
    pos = t * T + lax.broadcasted_iota(jnp.int32, (T, 1), 0)
    for gi, w in enumerate(POOL_WINDOWS):
        c0, c1 = gi * gc, (gi + 1) * gc
        tok = ext_ref[POOL_HALO:POOL_HALO + T, c0:c1]
        acc = tok
        for k in range(1, w):
            acc = acc + ext_ref[POOL_HALO - k:POOL_HALO - k + T, c0:c1]
        cnt = jnp.minimum(w, pos + 1).astype(F32)
        d = acc / cnt - tok
        y = jnp.dot(d.astype(BF16), wp_ref[gi], preferred_element_type=F32) * sc_ref[:, c0:c1]
        y_ref[:, c0:c1] = y.astype(y_ref.dtype)
    ext_ref[0:POOL_HALO, :] = ext_ref[T:T + POOL_HALO, :]


def _pool_prompt(z, wp_all, layer, sc, *, B, S, d_pool, T):
    nt = S // T
    gc = d_pool // len(POOL_WINDOWS)
    return pl.pallas_call(
        functools.partial(_pool_prompt_body, T=T, gc=gc),
        grid=(B, nt),
        in_specs=[pl.BlockSpec((T, d_pool), lambda b, t: (b * nt + t, 0)),
                  pl.BlockSpec((None,) + wp_all.shape[1:], lambda b, t: (layer, 0, 0, 0)),
                  pl.BlockSpec((1, d_pool), lambda b, t: (0, 0))],
        out_specs=pl.BlockSpec((T, d_pool), lambda b, t: (b * nt + t, 0)),
        out_shape=jax.ShapeDtypeStruct((B * S, d_pool), BF16),
        scratch_shapes=[pltpu.VMEM((POOL_HALO + T, d_pool), F32)],
        compiler_params=_params("parallel", "arbitrary"), name="pool_prompt")(z, wp_all, sc)


def _pool_step_body(buf_ref, u_ref, wp_ref, sc_ref, y_ref, *, gc, pos0):
    u = u_ref[...]
    for gi, w in enumerate(POOL_WINDOWS):
        c0, c1 = gi * gc, (gi + 1) * gc
        tok = u[:, c0:c1]
        acc = tok
        for k in range(1, w):
            acc = acc + buf_ref[POOL_BUF - k, :, c0:c1]
        d = acc / float(min(w, pos0 + 1)) - tok
        y = jnp.dot(d.astype(BF16), wp_ref[gi], preferred_element_type=F32) * sc_ref[:, c0:c1]
        y_ref[:, c0:c1] = y.astype(y_ref.dtype)


def _pool_step(buf_t, u, wp, sc, *, pos0):
    Bs, d_pool = u.shape
    gc = d_pool // len(POOL_WINDOWS)
    return pl.pallas_call(
        functools.partial(_pool_step_body, gc=gc, pos0=pos0),
        out_shape=jax.ShapeDtypeStruct((Bs, d_pool), BF16),
        compiler_params=pltpu.CompilerParams(vmem_limit_bytes=VMEM_LIMIT),
        name="pool_step")(buf_t, u, wp, sc)


def _head_norm_gate(h, o, g):
    mu = jnp.mean(h, axis=-1, keepdims=True)
    hc = h - mu
    var = jnp.mean(hc * hc, axis=-1, keepdims=True)
    return hc * lax.rsqrt(var + HEAD_NORM_EPS) * g * jax.nn.sigmoid(o)


def _mlstm_prompt_body(q_ref, k_ref, v_ref, o_ref, gt_ref, bias_ref, hg_ref,
                       hn_ref, C_ref, n_ref, m_ref, *, L, dk):
    c = pl.program_id(1)

    @pl.when(c == 0)
    def _():
        C_ref[...] = jnp.zeros_like(C_ref)
        n_ref[...] = jnp.zeros_like(n_ref)
        m_ref[...] = jnp.zeros_like(m_ref)

    row = lax.broadcasted_iota(jnp.int32, (L, L), 0)
    col = lax.broadcasted_iota(jnp.int32, (L, L), 1)
    causal = col <= row
    neg_inf = jnp.float32(-jnp.inf)
    k_scale = dk ** -0.5
    gts = [gt_ref[sq] + bias_ref[...] for sq in range(gt_ref.shape[0])]
    gTs = [gt.T for gt in gts]
    for sq, h in [(sq, h) for sq in range(len(gts)) for h in range(MH)]:
        gt, gT = gts[sq], gTs[sq]
        c0, c1 = h * dk, (h + 1) * dk
        ig_col = gt[:, h:h + 1]
        lf_col = _log_sigmoid(gt[:, MH + h:MH + h + 1])
        ig_row = gT[h:h + 1, :]
        lf_row = _log_sigmoid(gT[MH + h:MH + h + 1, :])
        b_col = jnp.sum(jnp.where(causal, lf_row, 0.0), axis=1, keepdims=True)
        b_row = jnp.sum(jnp.where(row <= col, lf_col, 0.0), axis=0, keepdims=True)
        a_row = ig_row - b_row
        a_col = ig_col - b_col
        cmax = jnp.max(jnp.where(causal, a_row, neg_inf), axis=1, keepdims=True)
        m_prev = m_ref[sq, h, :, 0:1]
        m_t = b_col + jnp.maximum(m_prev, cmax)
        w_inter = jnp.exp(b_col + m_prev - m_t)
        dmat = jnp.exp(jnp.where(causal, a_row + b_col - m_t, neg_inf))
        q = q_ref[sq, :, c0:c1]
        ks = k_ref[sq, :, c0:c1] * k_scale
        qb = q.astype(BF16)
        kb = ks.astype(BF16)
        vb = v_ref[sq, :, c0:c1].astype(BF16)
        s = lax.dot_general(qb, kb, (((1,), (1,)), ((), ())), preferred_element_type=F32) * dmat
        C_prev = C_ref[sq, h]
        n_prev = n_ref[sq, h]
        num = (w_inter * jnp.dot(qb, C_prev.astype(BF16), preferred_element_type=F32)
               + jnp.dot(s.astype(BF16), vb, preferred_element_type=F32))
        den = (w_inter * jnp.sum(q * n_prev, axis=1, keepdims=True)
               + jnp.sum(s, axis=1, keepdims=True))
        hh = num * (1.0 / jnp.maximum(jnp.abs(den), jnp.exp(-m_t)))
        hn_ref[sq, :, c0:c1] = _head_norm_gate(hh, o_ref[sq, :, c0:c1], hg_ref[:, c0:c1]).astype(hn_ref.dtype)

        m_new = m_t[L - 1:L, :]
        b_last = b_col[L - 1:L, :]
        w_old = jnp.exp(b_last + m_prev - m_new)
        kw = ks * jnp.exp(a_col + b_last - m_new)
        C_ref[sq, h] = w_old * C_prev + lax.dot_general(
            kw.astype(BF16), vb, (((0,), (0,)), ((), ())), preferred_element_type=F32)
        n_ref[sq, h] = w_old * n_prev + jnp.sum(kw, axis=0, keepdims=True)
        m_ref[sq, h] = jnp.broadcast_to(m_new, (1, LANES))


def _mlstm_prompt(z, gates, bias, hg, *, B, S, d_pool, d_ml, L):
    nseq = math.gcd(B, MLSTM_SEQS_PER_STEP)
    z = z.reshape(B, S, z.shape[1])
    gates = gates.reshape(B, S, LANES)
    nc = S // L
    dk = d_ml // MH
    blk0 = d_pool // d_ml
    assert blk0 * d_ml == d_pool

    def zspec(part):
        return pl.BlockSpec((nseq, L, d_ml), lambda b, c: (b, c, blk0 + part))

    hn, C, n, m = pl.pallas_call(
        functools.partial(_mlstm_prompt_body, L=L, dk=dk),
        grid=(B // nseq, nc),
        in_specs=[zspec(0), zspec(1), zspec(2), zspec(3),
                  pl.BlockSpec((nseq, L, LANES), lambda b, c: (b, c, 0)),
                  pl.BlockSpec((1, LANES), lambda b, c: (0, 0)),
                  pl.BlockSpec((1, d_ml), lambda b, c: (0, 0))],
        out_specs=[pl.BlockSpec((nseq, L, d_ml), lambda b, c: (b, c, 0)),
                   pl.BlockSpec((nseq, MH, dk, dk), lambda b, c: (b, 0, 0, 0)),
                   pl.BlockSpec((nseq, MH, 1, dk), lambda b, c: (b, 0, 0, 0)),
                   pl.BlockSpec((nseq, MH, 1, LANES), lambda b, c: (b, 0, 0, 0))],
        out_shape=[jax.ShapeDtypeStruct((B, S, d_ml), BF16),
                   jax.ShapeDtypeStruct((B, MH, dk, dk), F32),
                   jax.ShapeDtypeStruct((B, MH, 1, dk), F32),
                   jax.ShapeDtypeStruct((B, MH, 1, LANES), F32)],
        compiler_params=_params("parallel", "arbitrary"), name="mlstm_prompt")(
            z, z, z, z, gates, bias, hg)
    return hn.reshape(B * S, d_ml), C, n, m


def _row_to_col(x_row, n):
    r = lax.broadcasted_iota(jnp.int32, (n, n), 0)
    c = lax.broadcasted_iota(jnp.int32, (n, n), 1)
    return jnp.sum(jnp.where(r == c, x_row, 0.0), axis=1, keepdims=True)


def _mlstm_step_body(z_ref, gt_ref, bias_ref, hg_ref, C0_ref, n0_ref, m0_ref,
                     hn_ref, C_ref, n_ref, m_ref, *, d_pool, d_ml, dk):
    gt = gt_ref[0] + bias_ref[...]
    k_scale = dk ** -0.5
    for h in range(MH):
        base = d_pool + h * dk
        q = z_ref[0, :, base:base + dk]
        ks = z_ref[0, :, base + d_ml:base + d_ml + dk] * k_scale
        v = z_ref[0, :, base + 2 * d_ml:base + 2 * d_ml + dk]
        o = z_ref[0, :, base + 3 * d_ml:base + 3 * d_ml + dk]
        ig = gt[:, h:h + 1]
        lf = _log_sigmoid(gt[:, MH + h:MH + h + 1])
        m_prev = m0_ref[0, h]
        m_new = jnp.maximum(m_prev + lf, ig)
        w_old = jnp.exp(lf + m_prev - m_new)
        w_s = jnp.exp(ig - m_new)
        C_new = w_old * C0_ref[0, h] + _row_to_col(ks * w_s, dk) * v
        n_new = w_old * n0_ref[0, h] + w_s * ks
        num = jnp.sum(_row_to_col(q, dk) * C_new, axis=0, keepdims=True)
        den = jnp.sum(q * n_new, axis=1, keepdims=True)
        hh = num / jnp.maximum(jnp.abs(den), jnp.exp(-m_new))
        c0 = h * dk
        hn_ref[0, :, c0:c0 + dk] = _head_norm_gate(hh, o, hg_ref[:, c0:c0 + dk]).astype(hn_ref.dtype)
        C_ref[0, h] = C_new
        n_ref[0, h] = n_new
        m_ref[0, h] = m_new


def _mlstm_step(z3, g3, bias, hg, C0, n0, m0, *, d_pool, d_ml):
    Bs = C0.shape[0]
    dk = d_ml // MH
    dz = z3.shape[-1]
    st = lambda *shape: pl.BlockSpec((1,) + shape, lambda b: (b,) + (0,) * len(shape))
    return pl.pallas_call(
        functools.partial(_mlstm_step_body, d_pool=d_pool, d_ml=d_ml, dk=dk),
        grid=(Bs,),
        in_specs=[st(1, dz), st(1, LANES),
                  pl.BlockSpec((1, LANES), lambda b: (0, 0)),
                  pl.BlockSpec((1, d_ml), lambda b: (0, 0)),
                  st(MH, dk, dk), st(MH, 1, dk), st(MH, 1, 1)],
        out_specs=[st(1, d_ml), st(MH, dk, dk), st(MH, 1, dk), st(MH, 1, 1)],
        out_shape=[jax.ShapeDtypeStruct((Bs, 1, d_ml), BF16),
                   jax.ShapeDtypeStruct((Bs, MH, dk, dk), F32),
                   jax.ShapeDtypeStruct((Bs, MH, 1, dk), F32),
                   jax.ShapeDtypeStruct((Bs, MH, 1, 1), F32)],
        compiler_params=_params("parallel"), name="mlstm_step")(z3, g3, bias, hg, C0, n0, m0)


KM_ROWS = 16


F32_BIG = 1.0e30


def _moba_seq_body(q_ref, k_ref, v_ref, o_ref, kb_ref, vT_ref, km_ref, qT_ref, pT_ref,
                   m_ref, l_ref, acc_ref, *, nblk, hd):
    blk = MOBA_BLOCK
    R = GROUP * blk
    nchunk = R // LANES
    c_exp = hd ** -0.5 * LOG2E
    neg_inf = jnp.float32(-jnp.inf)

    kb_ref[...] = k_ref[...].astype(BF16)
    vT_ref[...] = v_ref[...].T.astype(BF16)
    km = [jnp.sum(k_ref[n * blk:(n + 1) * blk, :], axis=0, keepdims=True) * (1.0 / blk) for n in range(nblk)]
    km.append(jnp.zeros((KM_ROWS - nblk, hd), F32))
    km_ref[...] = jnp.concatenate(km, axis=0).astype(BF16)
    sub = lax.broadcasted_iota(jnp.int32, (nblk, R), 0)

    def own_block(j, buf, p_rows_ref):
        ms, ls = [], []
        for c in range(nchunk):
            lanes = slice(c * LANES, (c + 1) * LANES)
            q0 = (c * LANES) % blk
            nk = q0 + LANES
            s = jnp.dot(kb_ref[j * blk:j * blk + nk, :], qT_ref[buf, :, lanes], preferred_element_type=F32)
            key = lax.broadcasted_iota(jnp.int32, (nk, LANES), 0)
            qry = lax.broadcasted_iota(jnp.int32, (nk, LANES), 1) + q0
            s = jnp.where(key <= qry, s, neg_inf)
            m = jnp.max(s, axis=0, keepdims=True)
            p = jnp.exp2((s - m) * c_exp)
            ms.append(m)
            ls.append(jnp.sum(p, axis=0, keepdims=True))
            p_rows_ref[0:nk, lanes] = p.astype(BF16)
            if nk < blk:
                p_rows_ref[nk:blk, lanes] = jnp.zeros((blk - nk, LANES), BF16)
        return ms, ls

    def write_out(j, acc, l_all):
        inv_l = 1.0 / l_all
        for g in range(GROUP):
            cols = slice(g * blk, (g + 1) * blk)
            o_ref[j * blk:(j + 1) * blk, g * hd:(g + 1) * hd] = (acc[:, cols] * inv_l[:, cols]).T.astype(o_ref.dtype)

    for j in range(nblk):
        buf = j % 2
        for g in range(GROUP):
            qT_ref[buf, :, g * blk:(g + 1) * blk] = (
                q_ref[j * blk:(j + 1) * blk, g * hd:(g + 1) * hd].astype(F32).T.astype(BF16))

        sel = None
        if j > MOBA_TOPK:
            gate = jnp.dot(km_ref[...], qT_ref[buf], preferred_element_type=F32)[0:nblk, :]
            rows = []
            for n in range(j):
                gn = gate[n:n + 1, :]
                beats = ((gate > gn) | ((gate == gn) & (sub < n))) & (sub < j)
                rows.append((jnp.sum(beats.astype(F32), axis=0, keepdims=True) < MOBA_TOPK).astype(F32))
            sel = rows

        ms, ls = own_block(j, buf, pT_ref.at[buf, j * blk:(j + 1) * blk])
        for n in range(j):
            kb = kb_ref[n * blk:(n + 1) * blk, :]
            for g in range(GROUP):
                s2 = jnp.dot(kb, qT_ref[buf, :, g * blk:(g + 1) * blk], preferred_element_type=F32)
                for half in range(blk // LANES):
                    c = g * (blk // LANES) + half
                    lanes = slice(c * LANES, (c + 1) * LANES)
                    s = s2[:, half * LANES:(half + 1) * LANES]
                    if sel is not None:
                        s = jnp.where(sel[n][:, lanes] > 0.5, s, neg_inf)
                    p = jnp.exp2((s - ms[c]) * c_exp)
                    ls[c] = ls[c] + jnp.sum(p, axis=0, keepdims=True)
                    pT_ref[buf, n * blk:(n + 1) * blk, lanes] = p.astype(BF16)
        nkeys = (j + 1) * blk
        acc = jnp.dot(vT_ref[:, 0:nkeys], pT_ref[buf, 0:nkeys, :], preferred_element_type=F32)
        l_all = jnp.concatenate(ls, axis=1)
        write_out(j, acc, l_all)

        if j > 0:
            @pl.when(jnp.logical_not(jnp.max(l_all) < F32_BIG))
            def _(j=j, buf=buf, sel=sel):
                ms, ls = own_block(j, buf, pT_ref.at[buf, 0:blk])
                m_ref[...] = jnp.concatenate(ms, axis=1)
                l_ref[...] = jnp.concatenate(ls, axis=1)
                acc_ref[...] = jnp.dot(vT_ref[:, j * blk:(j + 1) * blk], pT_ref[buf, 0:blk, :],
                                       preferred_element_type=F32)
                selmat = None if sel is None else jnp.concatenate(
                    sel + [jnp.zeros((nblk - j, R), F32)], axis=0)

                def body(n, carry):
                    start = pl.multiple_of(n * blk, blk)
                    kb = kb_ref[pl.ds(start, blk), :]
                    vtb = vT_ref[:, pl.ds(start, blk)]
                    if selmat is not None:
                        sel_n = jnp.sum(jnp.where(sub == n, selmat, 0.0), axis=0, keepdims=True)
                    for c in range(nchunk):
                        lanes = slice(c * LANES, (c + 1) * LANES)
                        s = jnp.dot(kb, qT_ref[buf, :, lanes], preferred_element_type=F32)
                        if selmat is not None:
                            s = jnp.where(sel_n[:, lanes] > 0.5, s, neg_inf)
                        m_old = m_ref[:, lanes]
                        m_new = jnp.maximum(m_old, jnp.max(s, axis=0, keepdims=True))
                        alpha = jnp.exp2((m_old - m_new) * c_exp)
                        p = jnp.exp2((s - m_new) * c_exp)
                        m_ref[:, lanes] = m_new
                        l_ref[:, lanes] = alpha * l_ref[:, lanes] + jnp.sum(p, axis=0, keepdims=True)
                        acc_ref[:, lanes] = acc_ref[:, lanes] * alpha + jnp.dot(
                            vtb, p.astype(BF16), preferred_element_type=F32)
                    return carry

                lax.fori_loop(0, j, body, 0)
                write_out(j, acc_ref[...], l_ref[...])


def _moba_seq(q, k, v, *, B, S, hd):
    assert S % MOBA_BLOCK == 0 and S // MOBA_BLOCK <= KM_ROWS
    nq = S // MOBA_BLOCK
    gw = GROUP * hd
    R = GROUP * MOBA_BLOCK
    return pl.pallas_call(
        functools.partial(_moba_seq_body, nblk=nq, hd=hd),
        grid=(B, N_KV_HEADS),
        in_specs=[pl.BlockSpec((S, gw), lambda b, h: (b, h)),
                  pl.BlockSpec((S, hd), lambda b, h: (b, h)),
                  pl.BlockSpec((S, hd), lambda b, h: (b, h))],
        out_specs=pl.BlockSpec((S, gw), lambda b, h: (b, h)),
        out_shape=jax.ShapeDtypeStruct((B * S, N_HEADS * hd), BF16),
        scratch_shapes=[pltpu.VMEM((S, hd), BF16),
                        pltpu.VMEM((hd, S), BF16),
                        pltpu.VMEM((KM_ROWS, hd), BF16),
                        pltpu.VMEM((2, hd, R), BF16),
                        pltpu.VMEM((2, S, R), BF16),
                        pltpu.VMEM((1, R), F32),
                        pltpu.VMEM((1, R), F32),
                        pltpu.VMEM((hd, R), F32)],
        compiler_params=_params("parallel", "parallel"), name="moba_seq")(q, k, v)


PAGES_PER_STEP = 16
PAGES_PER_BLOCK = MOBA_BLOCK // PAGE_SIZE
TOKENS_PER_TILE = SUBLANES // N_KV_HEADS
KV_SHIFT = N_KV_HEADS.bit_length() - 1
GROUP_SHIFT = GROUP.bit_length() - 1
assert 1 << KV_SHIFT == N_KV_HEADS and 1 << GROUP_SHIFT == GROUP and TOKENS_PER_TILE * N_KV_HEADS == SUBLANES


def _block_mean_keys(page_refs, o_ref):
    rows, hd = page_refs[0].shape[2:]
    for blk in range(PAGES_PER_STEP // PAGES_PER_BLOCK):
        acc = None
        for r in range(PAGES_PER_BLOCK):
            pg = page_refs[blk * PAGES_PER_BLOCK + r][0, 0]
            part = jnp.sum(pg.reshape(rows // SUBLANES, SUBLANES, hd), axis=0)
            acc = part if acc is None else acc + part
        per_head = acc[0:N_KV_HEADS]
        for t in range(1, TOKENS_PER_TILE):
            per_head = per_head + acc[t * N_KV_HEADS:(t + 1) * N_KV_HEADS]
        o_ref[0, 0, blk * N_KV_HEADS:(blk + 1) * N_KV_HEADS, :] = per_head * (1.0 / MOBA_BLOCK)


def _topk_body(q_ref, km_ref, idx_ref, *, ncol):
    qb = q_ref[0].astype(BF16)
    km = km_ref[0, 0].astype(BF16)
    gate = lax.dot_general(qb, km, (((1,), (1,)), ((), ())), preferred_element_type=F32)
    head = lax.broadcasted_iota(jnp.int32, (N_HEADS, ncol), 0)
    col = lax.broadcasted_iota(jnp.int32, (N_HEADS, ncol), 1)
    gate = jnp.where((col & (N_KV_HEADS - 1)) == (head >> GROUP_SHIFT), gate, -jnp.inf)
    out_lane = lax.broadcasted_iota(jnp.int32, (N_HEADS, LANES), 1)
    out = jnp.zeros((N_HEADS, LANES), jnp.int32)
    for r in range(MOBA_TOPK):
        mx = jnp.max(gate, axis=1, keepdims=True)
        idx = jnp.min(jnp.where(gate == mx, col, ncol), axis=1, keepdims=True)
        out = jnp.where(out_lane == r, idx >> KV_SHIFT, out)
        gate = jnp.where(col == idx, -jnp.inf, gate)
    idx_ref[0] = out


def _topk_blocks(q3, kmean, layer):
    Bs, _, hd = q3.shape
    ncol = kmean.shape[2]
    assert ncol >= MOBA_TOPK * N_KV_HEADS
    return pl.pallas_call(
        functools.partial(_topk_body, ncol=ncol),
        grid=(Bs,),
        in_specs=[pl.BlockSpec((1, N_HEADS, hd), lambda b: (b, 0, 0)),
                  pl.BlockSpec((1, 1, ncol, hd), lambda b: (layer, b, 0, 0))],
        out_specs=pl.BlockSpec((1, N_HEADS, LANES), lambda b: (b, 0, 0)),
        out_shape=jax.ShapeDtypeStruct((Bs, N_HEADS, LANES), jnp.int32),
        compiler_params=_params("parallel"), name="topk_blocks")(q3, kmean)


N_SEL_PAGES = MOBA_TOPK * PAGES_PER_BLOCK


def _moba_step_body(idx_ref, pt_ref, q_ref, kn_ref, vn_ref, *refs, hd):
    n_in = GROUP * N_SEL_PAGES
    k_refs, v_refs, o_ref = refs[:n_in], refs[n_in:2 * n_in], refs[2 * n_in]
    kv_head = pl.program_id(1)
    scale = hd ** -0.5
    rows = SAMPLE_ROWS
    ncol = k_refs[0].shape[2]
    mine = (lax.broadcasted_iota(jnp.int32, (rows, ncol), 1) & (N_KV_HEADS - 1)) == kv_head
    k_new = kn_ref[0, 0].astype(BF16).astype(F32)
    v_new = vn_ref[0, 0].astype(BF16).astype(F32)
    for g in range(GROUP):
        qb = jnp.broadcast_to(q_ref[0, g], (rows, hd)).astype(BF16)
        pages = slice(g * N_SEL_PAGES, (g + 1) * N_SEL_PAGES)
        s_pages = [jnp.where(mine,
                             lax.dot_general(qb, kr[0, 0].astype(BF16), (((1,), (1,)), ((), ())),
                                             preferred_element_type=F32) * scale,
                             -jnp.inf) for kr in k_refs[pages]]
        s_own = jnp.sum(qb.astype(F32) * k_new, axis=1, keepdims=True) * scale
        m = s_own
        for s in s_pages:
            m = jnp.maximum(m, jnp.max(s, axis=1, keepdims=True))
        p_own = jnp.exp(s_own - m)
        l = p_own
        acc = p_own * v_new
        for s, vr in zip(s_pages, v_refs[pages]):
            p = jnp.exp(s - m)
            l = l + jnp.sum(p, axis=1, keepdims=True)
            acc = acc + jnp.dot(p.astype(BF16), vr[0, 0].astype(BF16), preferred_element_type=F32)
        o_ref[0, g] = (acc / l)[0:1, :].astype(o_ref.dtype)


def _moba_step(q4, kn4, vn4, cache_k4, cache_v4, idx_flat, pt_flat, *, layer, n_pages):
    Bs, _, _, hd = q4.shape
    rows = cache_k4.shape[2]

    def page_spec(r):
        g, r = divmod(r, N_SEL_PAGES)
        sel, half = divmod(r, PAGES_PER_BLOCK)

        def index_map(b, h, idx, pt):
            blk = idx[(b * N_HEADS + h * GROUP + g) * MOBA_TOPK + sel]
            return (layer, pt[b * n_pages + blk * PAGES_PER_BLOCK + half], 0, 0)

        return pl.BlockSpec((1, 1, rows, hd), index_map)

    group_spec = pl.BlockSpec((1, GROUP, 1, hd), lambda b, h, idx, pt: (b, h, 0, 0))
    kv_spec = pl.BlockSpec((1, 1, 1, hd), lambda b, h, idx, pt: (b, h, 0, 0))
    n_in = GROUP * N_SEL_PAGES
    return pl.pallas_call(
        functools.partial(_moba_step_body, hd=hd),
        grid_spec=pltpu.PrefetchScalarGridSpec(
            num_scalar_prefetch=2, grid=(Bs, N_KV_HEADS),
            in_specs=[group_spec, kv_spec, kv_spec] + [page_spec(r) for r in range(n_in)] * 2,
            out_specs=group_spec),
        out_shape=jax.ShapeDtypeStruct((Bs, N_HEADS, 1, hd), BF16),
        compiler_params=_params("parallel", "arbitrary"), name="moba_step")(
            idx_flat, pt_flat, q4, kn4, vn4, *([cache_k4] * n_in), *([cache_v4] * n_in))


def _pad_rows(a, rows):
    return jnp.pad(a, ((0, rows - a.shape[0]),) + ((0, 0),) * (a.ndim - 1))


def kernel(x_prompt, x_sample, cache_k, cache_v, state_pool, state_C, state_n, state_m, page_table,
           w_in_mix, b_if, w_pool, pool_scale, mlstm_norm_g, w_out_mix, w_qkv, w_o,
           ln_mix_g, ln_mix_b, ln_ffn_g, ln_ffn_b, w_up, w_down):
    B, S, D = x_prompt.shape
    Bs = x_sample.shape[0]
    assert x_sample.shape[1] == 1 and Bs <= SAMPLE_ROWS
    depth = w_up.shape[0]
    d_pool = state_pool.shape[-1]
    dk = state_C.shape[-1]
    d_ml = MH * dk
    d_main = d_pool + 4 * d_ml
    assert w_in_mix.shape[2] == d_main + 2 * MH
    hd = cache_k.shape[-1]
    d_q, d_kv = N_HEADS * hd, N_KV_HEADS * hd
    n_pages = page_table.shape[1]
    past_len = n_pages * PAGE_SIZE
    assert cache_k.shape[2] == PAGE_SIZE and past_len % MOBA_BLOCK == 0
    alpha = (2 * depth) ** 0.25
    M = B * S
    L = math.gcd(S, MLSTM_CHUNK)

    w_in_t = jnp.swapaxes(w_in_mix, 1, 2)
    bias_g = jnp.pad(b_if, ((0, 0), (0, LANES - 2 * MH)))[:, None, :]
    w_pool_b = w_pool.astype(BF16)

    cache_k4 = cache_k.reshape(cache_k.shape[:2] + (PAGE_SIZE * N_KV_HEADS, hd))
    cache_v4 = cache_v.reshape(cache_v.shape[:2] + (PAGE_SIZE * N_KV_HEADS, hd))
    pt_flat = page_table.reshape(-1)
    kmean_past = None

    xs = _pad_rows(x_sample.reshape(Bs, D), SAMPLE_ROWS)
    xp = x_prompt.reshape(M, D)
    ks_, vs_, pool_s, C_s, n_s, m_s = [], [], [], [], [], []
    pool_p, C_p, n_p, m_p = [], [], [], []
    n_attn = cache_k.shape[0]
    kv_flat = None
    for l in range(depth):
        i = l // 2
        ln_g, ln_b = ln_mix_g[l][None, :], ln_mix_b[l][None, :]
        fg, fb = ln_ffn_g[l][None, :], ln_ffn_b[l][None, :]
        if l % 2 == 0:
            sc = pool_scale[i][None, :]
            hg = mlstm_norm_g[i][None, :]
            zs, gs, wb, wgb = _proj_gates_step(xs, w_in_t, i, n_main=d_main, n_gate=2 * MH, tn=1024)
            us = zs[:Bs, :d_pool]
            ys_pool = _pool_step(jnp.transpose(state_pool[i], (1, 0, 2)), us, w_pool_b[i], sc, pos0=past_len)
            hns, C, n, m = _mlstm_step(
                zs[:Bs, None, :], gs[:Bs, None, :], bias_g[i], hg, state_C[i],
                state_n[i][:, :, None, :], state_m[i][:, :, None, None], d_pool=d_pool, d_ml=d_ml)
            a = _pad_rows(jnp.concatenate([ys_pool, hns[:, 0, :]], axis=1), SAMPLE_ROWS)
            xs, wpb = _proj_ln_step(a, w_out_mix, i, xs, ln_g, ln_b, alpha=alpha, tk=512)
            pool_s.append(jnp.concatenate([state_pool[i][:, 1:, :], us[:, None, :]], axis=1))
            C_s.append(C)
            n_s.append(n[:, :, 0, :])
            m_s.append(m[:, :, 0, 0])
        else:
            qs, k_new, v_new, wb = _proj_qkv_step(xs, w_qkv, i, d_q=d_q, d_kv=d_kv)
            q_s = qs[:Bs].astype(F32).reshape(Bs, N_HEADS, hd)
            k_new = k_new[:Bs].reshape(Bs, N_KV_HEADS, 1, hd)
            v_new = v_new[:Bs].reshape(Bs, N_KV_HEADS, 1, hd)
            idx = _topk_blocks(q_s, kmean_past, i)[:, :, :MOBA_TOPK].reshape(-1)
            os_ = _moba_step(q_s[:, :, None, :], k_new, v_new, cache_k4, cache_v4, idx, pt_flat,
                             layer=i, n_pages=n_pages)
            xs, wpb = _proj_ln_step(_pad_rows(os_.reshape(Bs, d_q), SAMPLE_ROWS), w_o, i, xs, ln_g, ln_b,
                                    alpha=alpha, tk=512)
            ks_.append(k_new.reshape(Bs, 1, N_KV_HEADS, hd))
            vs_.append(v_new.reshape(Bs, 1, N_KV_HEADS, hd))
        if l == 0:
            xs, wub, wdb = _ffn_step(xs, w_up, w_down, l, fg, fb, alpha=alpha, tf=512)
        else:
            xs, = _ffn(xs, wub[None], wdb[None], 0, fg, fb, alpha=alpha, tm=SAMPLE_ROWS, tf=1024)

        if l % 2 == 0:
            z, gates = _proj_gates(xp, wb, wgb, tm=1024, tn=1024)
            y_pool = _pool_prompt(z, w_pool_b, i, sc, B=B, S=S, d_pool=d_pool, T=256)
            hn, C, n, m = _mlstm_prompt(z, gates, bias_g[i], hg, B=B, S=S, d_pool=d_pool, d_ml=d_ml, L=L)
            xp = _proj_ln([y_pool, hn], wpb[None], 0, xp, ln_g, ln_b, alpha=alpha, tm=512)
            pool_p.append(z.reshape(B, S, d_main)[:, S - POOL_BUF:, :d_pool])
            C_p.append(C)
            n_p.append(n.reshape(B, MH, dk))
            m_p.append(m[:, :, 0, 0])
        else:
            q, k, v, *kv_flat = _proj_qkv(xp, wb[None], 0, d_q=d_q, d_kv=d_kv, hd=hd, tm=1024,
                                          slab=i, n_slabs=n_attn, prev=kv_flat)
            o = _moba_seq(q, k, v, B=B, S=S, hd=hd)
            xp = _proj_ln([o], wpb[None], 0, xp, ln_g, ln_b, alpha=alpha, tm=512)
        res = _ffn(xp, wub[None], wdb[None], 0, fg, fb, alpha=alpha, tm=512, tf=1024,
                   paged=(cache_k4, pt_flat, Bs, n_pages) if kmean_past is None else None,
                   cast_next=(w_up, w_down, l + 1) if l + 1 < depth else None)
        xp = res[0]
        if kmean_past is None:
            kmean_past = res[1]
        if l + 1 < depth:
            wub, wdb = res[-2:]

    k_prompt, v_prompt = (t.reshape(n_attn, B, S, N_KV_HEADS, hd) for t in kv_flat)
    return (xp.reshape(B, S, D), xs[:Bs].reshape(Bs, 1, D),
            k_prompt, v_prompt, jnp.stack(ks_), jnp.stack(vs_),
            jnp.stack(pool_p), jnp.stack(C_p), jnp.stack(n_p), jnp.stack(m_p),
            jnp.stack(pool_s), jnp.stack(C_s), jnp.stack(n_s), jnp.stack(m_s))
```

```python
import functools
import math

import jax
import jax.numpy as jnp
from jax import lax
from jax.experimental import pallas as pl
from jax.experimental.pallas import tpu as pltpu

F32 = jnp.float32
BF16 = jnp.bfloat16

POOL_WINDOWS = (2, 4, 8, 16)
POOL_BUF = max(POOL_WINDOWS) - 1
POOL_HALO = 16
MH = 4
N_HEADS = 16
N_KV_HEADS = 4
GROUP = N_HEADS // N_KV_HEADS
MOBA_BLOCK = 256
MOBA_TOPK = 3
PAGE_SIZE = 128
LN_EPS = 1e-5
HEAD_NORM_EPS = 1e-6
LANES = 128
SUBLANES = 8
SAMPLE_ROWS = 16
LOG2E = 1.4426950408889634

VMEM_LIMIT = 56 * 1024 * 1024

PROJ_TM, PROJ_TN = 1024, 1024
LN_TM = 512
LN_SUB_ROWS = 256
FFN_TM, FFN_TF = 512, 1024
STEP_TN, STEP_TK, STEP_TF = 1024, 512, 512
POOL_T = 256
MLSTM_CHUNK = 256
MLSTM_SEQS_PER_STEP = 1


def _params(*sem):
    return pltpu.CompilerParams(dimension_semantics=sem, vmem_limit_bytes=VMEM_LIMIT)


def _layer_norm(y, g, b):
    mu = jnp.mean(y, axis=-1, keepdims=True)
    yc = y - mu
    var = jnp.mean(yc * yc, axis=-1, keepdims=True)
    return yc * lax.rsqrt(var + LN_EPS) * g + b


def _log_sigmoid(x):
    return jnp.minimum(x, 0.0) - jnp.log1p(jnp.exp(-jnp.abs(x)))


_NT = (((1,), (1,)), ((), ()))


def _proj_gates_body(x_ref, w_ref, wg_ref, o_ref, g_ref, xb_ref):
    @pl.when(pl.program_id(1) == 0)
    def _():
        xb = x_ref[...].astype(BF16)
        xb_ref[...] = xb
        g_ref[...] = lax.dot_general(xb, wg_ref[...], _NT, preferred_element_type=F32)

    o_ref[...] = lax.dot_general(xb_ref[...], w_ref[...], _NT, preferred_element_type=F32)


def _proj_gates(x, wt, wgt, *, tm, tn):
    M, K = x.shape
    n_main = wt.shape[0]
    return pl.pallas_call(
        _proj_gates_body, grid=(M // tm, n_main // tn),
        in_specs=[pl.BlockSpec((tm, K), lambda i, j: (i, 0)),
                  pl.BlockSpec((tn, K), lambda i, j: (j, 0)),
                  pl.BlockSpec((LANES, K), lambda i, j: (0, 0))],
        out_specs=[pl.BlockSpec((tm, tn), lambda i, j: (i, j)),
                   pl.BlockSpec((tm, LANES), lambda i, j: (i, 0))],
        out_shape=[jax.ShapeDtypeStruct((M, n_main), F32), jax.ShapeDtypeStruct((M, LANES), F32)],
        scratch_shapes=[pltpu.VMEM((tm, K), BF16)],
        compiler_params=_params("parallel", "arbitrary"), name="proj_gates")(x, wt, wgt)


def _proj_qkv_body(x_ref, w_ref, *rest, nq_tiles, n_alias):
    q_ref, k_ref, v_ref, kf_ref, vf_ref, xb_ref = rest[n_alias:]
    j = pl.program_id(1)

    @pl.when(j == 0)
    def _():
        xb_ref[...] = x_ref[...].astype(BF16)

    y = jnp.dot(xb_ref[...], w_ref[...], preferred_element_type=F32)

    @pl.when(j < nq_tiles)
    def _():
        q_ref[...] = y.astype(q_ref.dtype)

    @pl.when(j == nq_tiles)
    def _():
        tm, d_kv = k_ref.shape
        hd = kf_ref.shape[1]
        nh = d_kv // hd
        k_ref[...] = y[:, :d_kv]
        v_ref[...] = y[:, d_kv:]
        for h in range(nh):
            kf_ref[pl.ds(h, tm, stride=nh), :] = y[:, h * hd:(h + 1) * hd]
            vf_ref[pl.ds(h, tm, stride=nh), :] = y[:, d_kv + h * hd:d_kv + (h + 1) * hd]


def _proj_qkv(x, w_all, layer, *, d_q, d_kv, hd, tm, slab, n_slabs, prev=None):
    M, K = x.shape
    tn = 2 * d_kv
    nq_tiles = d_q // tn
    nh = d_kv // hd
    assert nq_tiles * tn == d_q
    flat_spec = pl.BlockSpec((None, tm * nh, hd), lambda i, j: (slab, i, 0))
    flat_shape = jax.ShapeDtypeStruct((n_slabs, M * nh, hd), F32)
    in_specs = [pl.BlockSpec((tm, K), lambda i, j: (i, 0)),
                pl.BlockSpec((None, K, tn), lambda i, j: (layer, 0, j))]
    args = [x, w_all]
    aliases = {}
    if prev is not None:
        in_specs += [pl.BlockSpec(memory_space=pl.ANY)] * 2
        args += list(prev)
        aliases = {2: 3, 3: 4}
    return pl.pallas_call(
        functools.partial(_proj_qkv_body, nq_tiles=nq_tiles, n_alias=len(aliases)),
        grid=(M // tm, nq_tiles + 1),
        in_specs=in_specs,
        out_specs=[pl.BlockSpec((tm, tn), lambda i, j: (i, jnp.minimum(j, nq_tiles - 1))),
                   pl.BlockSpec((tm, d_kv), lambda i, j: (i, 0)),
                   pl.BlockSpec((tm, d_kv), lambda i, j: (i, 0)),
                   flat_spec, flat_spec],
        out_shape=[jax.ShapeDtypeStruct((M, d_q), BF16),
                   jax.ShapeDtypeStruct((M, d_kv), F32),
                   jax.ShapeDtypeStruct((M, d_kv), F32),
                   flat_shape, flat_shape],
        scratch_shapes=[pltpu.VMEM((tm, K), BF16)],
        input_output_aliases=aliases,
        compiler_params=_params("parallel", "arbitrary"), name="proj_qkv")(*args)


def _proj_ln_body(*refs, n_a, alpha):
    a_refs = refs[:n_a]
    w_ref, res_ref, g_ref, b_ref, o_ref = refs[n_a:]
    tm = res_ref.shape[0]
    sub = min(tm, LN_SUB_ROWS)
    for r0 in range(0, tm, sub):
        rows = slice(r0, r0 + sub)
        y = alpha * res_ref[rows, :]
        k0 = 0
        for a_ref in a_refs:
            ka = a_ref.shape[1]
            y = y + jnp.dot(a_ref[rows, :], w_ref[k0:k0 + ka, :], preferred_element_type=F32)
            k0 += ka
        o_ref[rows, :] = _layer_norm(y, g_ref[...], b_ref[...])


def _proj_ln(a_list, w_all, layer, res, g, b, *, alpha, tm):
    M, D = res.shape
    K = w_all.shape[1]
    in_specs = [pl.BlockSpec((tm, a.shape[1]), lambda i: (i, 0)) for a in a_list]
    in_specs += [pl.BlockSpec((None, K, D), lambda i: (layer, 0, 0)),
                 pl.BlockSpec((tm, D), lambda i: (i, 0)),
                 pl.BlockSpec((1, D), lambda i: (0, 0)),
                 pl.BlockSpec((1, D), lambda i: (0, 0))]
    return pl.pallas_call(
        functools.partial(_proj_ln_body, n_a=len(a_list), alpha=alpha),
        grid=(M // tm,), in_specs=in_specs,
        out_specs=pl.BlockSpec((tm, D), lambda i: (i, 0)),
        out_shape=jax.ShapeDtypeStruct((M, D), F32),
        compiler_params=_params("parallel"), name="proj_ln")(*a_list, w_all, res, g, b)


def _ffn_body(*refs, alpha, n_page_refs, cast_next):
    if n_page_refs:
        refs = refs[1:]
    x_ref, wu_ref, wd_ref, g_ref, b_ref = refs[:5]
    refs = refs[5:]
    page_refs, refs = refs[:n_page_refs], refs[n_page_refs:]
    next_refs, refs = (refs[:2], refs[2:]) if cast_next else ((), refs)
    o_ref, refs = refs[0], refs[1:]
    km_ref, refs = (refs[0], refs[1:]) if n_page_refs else (None, refs)
    next_out_refs, refs = (refs[:2], refs[2:]) if cast_next else ((), refs)
    xb_ref, acc_ref = refs
    f = pl.program_id(1)
    last = pl.num_programs(1) - 1

    def hidden(rows):
        h = jnp.dot(xb_ref[rows, :], wu_ref[...], preferred_element_type=F32)
        h = jnp.square(jnp.maximum(h, 0.0)).astype(BF16)
        return jnp.dot(h, wd_ref[...], preferred_element_type=F32)

    def page_stream():
        if n_page_refs:
            _block_mean_keys(page_refs, km_ref)
        for src, dst in zip(next_refs, next_out_refs):
            dst[...] = src[...].astype(BF16)

    @pl.when(f == 0)
    def _():
        xb_ref[...] = x_ref[...].astype(BF16)
        acc_ref[...] = jnp.zeros_like(acc_ref)

    @pl.when(f < last)
    def _():
        acc_ref[...] += hidden(slice(None))
        page_stream()

    @pl.when(f == last)
    def _():
        page_stream()
        tm = x_ref.shape[0]
        sub = min(tm, LN_SUB_ROWS)
        for r0 in range(0, tm, sub):
            rows = slice(r0, r0 + sub)
            y = alpha * x_ref[rows, :] + (acc_ref[rows, :] + hidden(rows))
            o_ref[rows, :] = _layer_norm(y, g_ref[...], b_ref[...])


def _ffn(x, wu_all, wd_all, layer, g, b, *, alpha, tm, tf, paged=None, cast_next=None):
    M, D = x.shape
    Fd = wu_all.shape[2]
    grid = (M // tm, Fd // tf)
    n_steps = grid[0] * grid[1]
    in_specs = [pl.BlockSpec((tm, D), lambda i, f, *_: (i, 0)),
                pl.BlockSpec((None, D, tf), lambda i, f, *_: (layer, 0, f)),
                pl.BlockSpec((None, tf, D), lambda i, f, *_: (layer, f, 0)),
                pl.BlockSpec((1, D), lambda i, f, *_: (0, 0)),
                pl.BlockSpec((1, D), lambda i, f, *_: (0, 0))]
    args = [x, wu_all, wd_all, g, b]
    out_specs = [pl.BlockSpec((tm, D), lambda i, f, *_: (i, 0))]
    out_shape = [jax.ShapeDtypeStruct((M, D), F32)]
    scratch = [pltpu.VMEM((tm, D), BF16), pltpu.VMEM((tm, D), F32)]
    n_page_refs = 0
    if paged is not None:
        cache_k4, pt_flat, Bs, n_pages = paged
        n_attn, _, rows, hd = cache_k4.shape
        assert grid == (n_attn * Bs, n_pages // PAGES_PER_STEP)
        out_rows = PAGES_PER_STEP // PAGES_PER_BLOCK * N_KV_HEADS
        n_page_refs = PAGES_PER_STEP

        def page_spec(r):
            return pl.BlockSpec(
                (1, 1, rows, hd),
                lambda i, f, pt: (i // Bs, pt[(i % Bs) * n_pages + f * PAGES_PER_STEP + r], 0, 0))

        in_specs += [page_spec(r) for r in range(PAGES_PER_STEP)]
        args += [cache_k4] * PAGES_PER_STEP
        out_specs.append(pl.BlockSpec((1, 1, out_rows, hd), lambda i, f, pt: (i // Bs, i % Bs, f, 0)))
        out_shape.append(jax.ShapeDtypeStruct((n_attn, Bs, grid[1] * out_rows, hd), F32))
    if cast_next is not None:
        wu_next, wd_next, nxt = cast_next
        for w in (wu_next, wd_next):
            rows_w, cols_w = w.shape[1:]
            rs = rows_w // n_steps
            assert rs * n_steps == rows_w and rs % SAMPLE_ROWS == 0
            in_specs.append(pl.BlockSpec((None, rs, cols_w), lambda i, f, *_: (nxt, i * grid[1] + f, 0)))
            args.append(w)
            out_specs.append(pl.BlockSpec((rs, cols_w), lambda i, f, *_: (i * grid[1] + f, 0)))
            out_shape.append(jax.ShapeDtypeStruct((rows_w, cols_w), BF16))
    body = functools.partial(_ffn_body, alpha=alpha, n_page_refs=n_page_refs, cast_next=cast_next is not None)
    riders = paged is not None or cast_next is not None
    params = _params("arbitrary" if riders else "parallel", "arbitrary")
    if paged is None:
        return pl.pallas_call(body, grid=grid, in_specs=in_specs, out_specs=out_specs, out_shape=out_shape,
                              scratch_shapes=scratch, compiler_params=params, name="ffn")(*args)
    return pl.pallas_call(
        body,
        grid_spec=pltpu.PrefetchScalarGridSpec(num_scalar_prefetch=1, grid=grid, in_specs=in_specs,
                                               out_specs=out_specs, scratch_shapes=scratch),
        out_shape=out_shape, compiler_params=params, name="ffn_kmean")(pt_flat, *args)


def _cast_tile(w_ref, wb_ref):
    wb = w_ref[...].astype(BF16)
    wb_ref[...] = wb
    return wb


def _proj_gates_step_body(x_ref, w_ref, wg_ref, o_ref, g_ref, wb_ref, wgb_ref):
    xb = x_ref[...].astype(BF16)

    @pl.when(pl.program_id(0) == 0)
    def _():
        n_gate, K = wg_ref.shape
        wg = jnp.concatenate([wg_ref[...], jnp.zeros((LANES - n_gate, K), F32)], axis=0).astype(BF16)
        wgb_ref[...] = wg
        g_ref[...] = lax.dot_general(xb, wg, _NT, preferred_element_type=F32)

    o_ref[...] = lax.dot_general(xb, _cast_tile(w_ref, wb_ref), _NT, preferred_element_type=F32)


def _proj_gates_step(x, wt_all, layer, *, n_main, n_gate, tn):
    R, K = x.shape
    assert n_main % n_gate == 0 and n_gate % SUBLANES == 0
    return pl.pallas_call(
        _proj_gates_step_body, grid=(n_main // tn,),
        in_specs=[pl.BlockSpec((R, K), lambda j: (0, 0)),
                  pl.BlockSpec((None, tn, K), lambda j: (layer, j, 0)),
                  pl.BlockSpec((None, n_gate, K), lambda j: (layer, n_main // n_gate, 0))],
        out_specs=[pl.BlockSpec((R, tn), lambda j: (0, j)),
                   pl.BlockSpec((R, LANES), lambda j: (0, 0)),
                   pl.BlockSpec((tn, K), lambda j: (j, 0)),
                   pl.BlockSpec((LANES, K), lambda j: (0, 0))],
        out_shape=[jax.ShapeDtypeStruct((R, n_main), F32), jax.ShapeDtypeStruct((R, LANES), F32),
                   jax.ShapeDtypeStruct((n_main, K), BF16), jax.ShapeDtypeStruct((LANES, K), BF16)],
        compiler_params=_params("arbitrary"), name="proj_gates_step")(x, wt_all, wt_all)


def _proj_qkv_step_body(x_ref, w_ref, q_ref, k_ref, v_ref, wb_ref, *, nq_tiles):
    j = pl.program_id(0)
    y = jnp.dot(x_ref[...].astype(BF16), _cast_tile(w_ref, wb_ref), preferred_element_type=F32)

    @pl.when(j < nq_tiles)
    def _():
        q_ref[...] = y.astype(q_ref.dtype)

    @pl.when(j == nq_tiles)
    def _():
        d_kv = k_ref.shape[1]
        k_ref[...] = y[:, :d_kv]
        v_ref[...] = y[:, d_kv:]


def _proj_qkv_step(x, w_all, layer, *, d_q, d_kv):
    R, K = x.shape
    tn = 2 * d_kv
    nq_tiles = d_q // tn
    assert nq_tiles * tn == d_q
    return pl.pallas_call(
        functools.partial(_proj_qkv_step_body, nq_tiles=nq_tiles),
        grid=(nq_tiles + 1,),
        in_specs=[pl.BlockSpec((R, K), lambda j: (0, 0)),
                  pl.BlockSpec((None, K, tn), lambda j: (layer, 0, j))],
        out_specs=[pl.BlockSpec((R, tn), lambda j: (0, jnp.minimum(j, nq_tiles - 1))),
                   pl.BlockSpec((R, d_kv), lambda j: (0, 0)),
                   pl.BlockSpec((R, d_kv), lambda j: (0, 0)),
                   pl.BlockSpec((K, tn), lambda j: (0, j))],
        out_shape=[jax.ShapeDtypeStruct((R, d_q), BF16),
                   jax.ShapeDtypeStruct((R, d_kv), F32),
                   jax.ShapeDtypeStruct((R, d_kv), F32),
                   jax.ShapeDtypeStruct((K, d_q + 2 * d_kv), BF16)],
        compiler_params=_params("arbitrary"), name="proj_qkv_step")(x, w_all)


def _proj_ln_step_body(a_ref, w_ref, res_ref, g_ref, b_ref, o_ref, wb_ref, acc_ref, *, alpha):
    k = pl.program_id(0)

    @pl.when(k == 0)
    def _():
        acc_ref[...] = alpha * res_ref[...]

    acc_ref[...] += jnp.dot(a_ref[...], _cast_tile(w_ref, wb_ref), preferred_element_type=F32)

    @pl.when(k == pl.num_programs(0) - 1)
    def _():
        o_ref[...] = _layer_norm(acc_ref[...], g_ref[...], b_ref[...])


def _proj_ln_step(a, w_all, layer, res, g, b, *, alpha, tk):
    R, D = res.shape
    K = w_all.shape[1]
    return pl.pallas_call(
        functools.partial(_proj_ln_step_body, alpha=alpha),
        grid=(K // tk,),
        in_specs=[pl.BlockSpec((R, tk), lambda k: (0, k)),
                  pl.BlockSpec((None, tk, D), lambda k: (layer, k, 0)),
                  pl.BlockSpec((R, D), lambda k: (0, 0)),
                  pl.BlockSpec((1, D), lambda k: (0, 0)),
                  pl.BlockSpec((1, D), lambda k: (0, 0))],
        out_specs=[pl.BlockSpec((R, D), lambda k: (0, 0)),
                   pl.BlockSpec((tk, D), lambda k: (k, 0))],
        out_shape=[jax.ShapeDtypeStruct((R, D), F32), jax.ShapeDtypeStruct((K, D), BF16)],
        scratch_shapes=[pltpu.VMEM((R, D), F32)],
        compiler_params=_params("arbitrary"), name="proj_ln_step")(a, w_all, res, g, b)


def _ffn_step_body(x_ref, wu_ref, wd_ref, g_ref, b_ref, o_ref, wub_ref, wdb_ref, acc_ref, *, alpha):
    f = pl.program_id(0)

    @pl.when(f == 0)
    def _():
        acc_ref[...] = jnp.zeros_like(acc_ref)

    h = jnp.dot(x_ref[...].astype(BF16), _cast_tile(wu_ref, wub_ref), preferred_element_type=F32)
    h = jnp.square(jnp.maximum(h, 0.0)).astype(BF16)
    acc_ref[...] += jnp.dot(h, _cast_tile(wd_ref, wdb_ref), preferred_element_type=F32)

    @pl.when(f == pl.num_programs(0) - 1)
    def _():
        y = alpha * x_ref[...] + acc_ref[...]
        o_ref[...] = _layer_norm(y, g_ref[...], b_ref[...])


def _ffn_step(x, wu_all, wd_all, layer, g, b, *, alpha, tf):
    R, D = x.shape
    Fd = wu_all.shape[2]
    return pl.pallas_call(
        functools.partial(_ffn_step_body, alpha=alpha),
        grid=(Fd // tf,),
        in_specs=[pl.BlockSpec((R, D), lambda f: (0, 0)),
                  pl.BlockSpec((None, D, tf), lambda f: (layer, 0, f)),
                  pl.BlockSpec((None, tf, D), lambda f: (layer, f, 0)),
                  pl.BlockSpec((1, D), lambda f: (0, 0)),
                  pl.BlockSpec((1, D), lambda f: (0, 0))],
        out_specs=[pl.BlockSpec((R, D), lambda f: (0, 0)),
                   pl.BlockSpec((D, tf), lambda f: (0, f)),
                   pl.BlockSpec((tf, D), lambda f: (f, 0))],
        out_shape=[jax.ShapeDtypeStruct((R, D), F32),
                   jax.ShapeDtypeStruct((D, Fd), BF16),
                   jax.ShapeDtypeStruct((Fd, D), BF16)],
        scratch_shapes=[pltpu.VMEM((R, D), F32)],
        compiler_params=_params("arbitrary"), name="ffn_step")(x, wu_all, wd_all, g, b)


def _pool_prompt_body(u_ref, wp_ref, sc_ref, y_ref, ext_ref, *, T, gc):
    t = pl.program_id(1)

    @pl.when(t == 0)
    def _():
        ext_ref[0:POOL_HALO, :] = jnp.zeros((POOL_HALO, ext_ref.shape[1]), F32)

    ext_ref[POOL_HALO:POOL_HALO + T, :] = u_ref[...]
    pos = t * T + lax.broadcasted_iota(jnp.int32, (T, 1), 0)
    for gi, w in enumerate(POOL_WINDOWS):
        c0, c1 = gi * gc, (gi + 1) * gc
        tok = ext_ref[POOL_HALO:POOL_HALO + T, c0:c1]
        acc = tok
        for k in range(1, w):
            acc = acc + ext_ref[POOL_HALO - k:POOL_HALO - k + T, c0:c1]
        cnt = jnp.minimum(w, pos + 1).astype(F32)
        d = acc / cnt - tok
        y = jnp.dot(d.astype(BF16), wp_ref[gi], preferred_element_type=F32) * sc_ref[:, c0:c1]
        y_ref[:, c0:c1] = y.astype(y_ref.dtype)
    ext_ref[0:POOL_HALO, :] = ext_ref[T:T + POOL_HALO, :]


def _pool_prompt(z, wp_all, layer, sc, *, B, S, d_pool, T):
    nt = S // T
    gc = d_pool // len(POOL_WINDOWS)
    return pl.pallas_call(
        functools.partial(_pool_prompt_body, T=T, gc=gc),
        grid=(B, nt),
        in_specs=[pl.BlockSpec((T, d_pool), lambda b, t: (b * nt + t, 0)),
                  pl.BlockSpec((None,) + wp_all.shape[1:], lambda b, t: (layer, 0, 0, 0)),
                  pl.BlockSpec((1, d_pool), lambda b, t: (0, 0))],
        out_specs=pl.BlockSpec((T, d_pool), lambda b, t: (b * nt + t, 0)),
        out_shape=jax.ShapeDtypeStruct((B * S, d_pool), BF16),
        scratch_shapes=[pltpu.VMEM((POOL_HALO + T, d_pool), F32)],
        compiler_params=_params("parallel", "arbitrary"), name="pool_prompt")(z, wp_all, sc)


def _pool_step_body(buf_ref, u_ref, wp_ref, sc_ref, y_ref, *, gc, pos0):
    u = u_ref[...]
    for gi, w in enumerate(POOL_WINDOWS):
        c0, c1 = gi * gc, (gi + 1) * gc
        tok = u[:, c0:c1]
        acc = tok
        for k in range(1, w):
            acc = acc + buf_ref[POOL_BUF - k, :, c0:c1]
        d = acc / float(min(w, pos0 + 1)) - tok
        y = jnp.dot(d.astype(BF16), wp_ref[gi], preferred_element_type=F32) * sc_ref[:, c0:c1]
        y_ref[:, c0:c1] = y.astype(y_ref.dtype)


def _pool_step(buf_t, u, wp, sc, *, pos0):
    Bs, d_pool = u.shape
    gc = d_pool // len(POOL_WINDOWS)
    return pl.pallas_call(
        functools.partial(_pool_step_body, gc=gc, pos0=pos0),
        out_shape=jax.ShapeDtypeStruct((Bs, d_pool), BF16),
        compiler_params=pltpu.CompilerParams(vmem_limit_bytes=VMEM_LIMIT),
        name="pool_step")(buf_t, u, wp, sc)


def _head_norm_gate(h, o, g):
    mu = jnp.mean(h, axis=-1, keepdims=True)
    hc = h - mu
    var = jnp.mean(hc * hc, axis=-1, keepdims=True)
    return hc * lax.rsqrt(var + HEAD_NORM_EPS) * g * jax.nn.sigmoid(o)


def _mlstm_prompt_body(q_ref, k_ref, v_ref, o_ref, gt_ref, bias_ref, hg_ref,
                       hn_ref, C_ref, n_ref, m_ref, *, L, dk):
    c = pl.program_id(1)

    @pl.when(c == 0)
    def _():
        C_ref[...] = jnp.zeros_like(C_ref)
        n_ref[...] = jnp.zeros_like(n_ref)
        m_ref[...] = jnp.zeros_like(m_ref)

    row = lax.broadcasted_iota(jnp.int32, (L, L), 0)
    col = lax.broadcasted_iota(jnp.int32, (L, L), 1)
    causal = col <= row
    neg_inf = jnp.float32(-jnp.inf)
    k_scale = dk ** -0.5
    gts = [gt_ref[sq] + bias_ref[...] for sq in range(gt_ref.shape[0])]
    gTs = [gt.T for gt in gts]
    for sq, h in [(sq, h) for sq in range(len(gts)) for h in range(MH)]:
        gt, gT = gts[sq], gTs[sq]
        c0, c1 = h * dk, (h + 1) * dk
        ig_col = gt[:, h:h + 1]
        lf_col = _log_sigmoid(gt[:, MH + h:MH + h + 1])
        ig_row = gT[h:h + 1, :]
        lf_row = _log_sigmoid(gT[MH + h:MH + h + 1, :])
        b_col = jnp.sum(jnp.where(causal, lf_row, 0.0), axis=1, keepdims=True)
        b_row = jnp.sum(jnp.where(row <= col, lf_col, 0.0), axis=0, keepdims=True)
        a_row = ig_row - b_row
        a_col = ig_col - b_col
        cmax = jnp.max(jnp.where(causal, a_row, neg_inf), axis=1, keepdims=True)
        m_prev = m_ref[sq, h, :, 0:1]
        m_t = b_col + jnp.maximum(m_prev, cmax)
        w_inter = jnp.exp(b_col + m_prev - m_t)
        dmat = jnp.exp(jnp.where(causal, a_row + (b_col - m_t), neg_inf))
        q = q_ref[sq, :, c0:c1]
        ks = k_ref[sq, :, c0:c1] * k_scale
        qb = q.astype(BF16)
        kb = ks.astype(BF16)
        vb = v_ref[sq, :, c0:c1].astype(BF16)
        s = lax.dot_general(qb, kb, (((1,), (1,)), ((), ())), preferred_element_type=F32) * dmat
        C_prev = C_ref[sq, h]
        n_prev = n_ref[sq, h]
        num = (w_inter * jnp.dot(qb, C_prev.astype(BF16), preferred_element_type=F32)
               + jnp.dot(s.astype(BF16), vb, preferred_element_type=F32))
        den = (w_inter * jnp.sum(q * n_prev, axis=1, keepdims=True)
               + jnp.sum(s, axis=1, keepdims=True))
        hh = num * (1.0 / jnp.maximum(jnp.abs(den), jnp.exp(-m_t)))
        hn_ref[sq, :, c0:c1] = _head_norm_gate(hh, o_ref[sq, :, c0:c1], hg_ref[:, c0:c1]).astype(hn_ref.dtype)

        m_new = m_t[L - 1:L, :]
        b_last = b_col[L - 1:L, :]
        w_old = jnp.exp(b_last + m_prev - m_new)
        kw = ks * jnp.exp(a_col + b_last - m_new)
        C_ref[sq, h] = w_old * C_prev + lax.dot_general(
            kw.astype(BF16), vb, (((0,), (0,)), ((), ())), preferred_element_type=F32)
        n_ref[sq, h] = w_old * n_prev + jnp.sum(kw, axis=0, keepdims=True)
        m_ref[sq, h] = jnp.broadcast_to(m_new, (1, LANES))


def _mlstm_prompt(z, gates, bias, hg, *, B, S, d_pool, d_ml, L):
    nseq = math.gcd(B, MLSTM_SEQS_PER_STEP)
    z = z.reshape(B, S, z.shape[1])
    gates = gates.reshape(B, S, LANES)
    nc = S // L
    dk = d_ml // MH
    blk0 = d_pool // d_ml
    assert blk0 * d_ml == d_pool

    def zspec(part):
        return pl.BlockSpec((nseq, L, d_ml), lambda b, c: (b, c, blk0 + part))

    hn, C, n, m = pl.pallas_call(
        functools.partial(_mlstm_prompt_body, L=L, dk=dk),
        grid=(B // nseq, nc),
        in_specs=[zspec(0), zspec(1), zspec(2), zspec(3),
                  pl.BlockSpec((nseq, L, LANES), lambda b, c: (b, c, 0)),
                  pl.BlockSpec((1, LANES), lambda b, c: (0, 0)),
                  pl.BlockSpec((1, d_ml), lambda b, c: (0, 0))],
        out_specs=[pl.BlockSpec((nseq, L, d_ml), lambda b, c: (b, c, 0)),
                   pl.BlockSpec((nseq, MH, dk, dk), lambda b, c: (b, 0, 0, 0)),
                   pl.BlockSpec((nseq, MH, 1, dk), lambda b, c: (b, 0, 0, 0)),
                   pl.BlockSpec((nseq, MH, 1, LANES), lambda b, c: (b, 0, 0, 0))],
        out_shape=[jax.ShapeDtypeStruct((B, S, d_ml), BF16),
                   jax.ShapeDtypeStruct((B, MH, dk, dk), F32),
                   jax.ShapeDtypeStruct((B, MH, 1, dk), F32),
                   jax.ShapeDtypeStruct((B, MH, 1, LANES), F32)],
        compiler_params=_params("parallel", "arbitrary"), name="mlstm_prompt")(
            z, z, z, z, gates, bias, hg)
    return hn.reshape(B * S, d_ml), C, n, m


def _row_to_col(x_row, n):
    r = lax.broadcasted_iota(jnp.int32, (n, n), 0)
    c = lax.broadcasted_iota(jnp.int32, (n, n), 1)
    return jnp.sum(jnp.where(r == c, x_row, 0.0), axis=1, keepdims=True)


def _mlstm_step_body(z_ref, gt_ref, bias_ref, hg_ref, C0_ref, n0_ref, m0_ref,
                     hn_ref, C_ref, n_ref, m_ref, *, d_pool, d_ml, dk):
    gt = gt_ref[0] + bias_ref[...]
    k_scale = dk ** -0.5
    for h in range(MH):
        base = d_pool + h * dk
        q = z_ref[0, :, base:base + dk]
        ks = z_ref[0, :, base + d_ml:base + d_ml + dk] * k_scale
        v = z_ref[0, :, base + 2 * d_ml:base + 2 * d_ml + dk]
        o = z_ref[0, :, base + 3 * d_ml:base + 3 * d_ml + dk]
        ig = gt[:, h:h + 1]
        lf = _log_sigmoid(gt[:, MH + h:MH + h + 1])
        m_prev = m0_ref[0, h]
        m_new = jnp.maximum(m_prev + lf, ig)
        w_old = jnp.exp(lf + m_prev - m_new)
        w_s = jnp.exp(ig - m_new)
        C_new = w_old * C0_ref[0, h] + _row_to_col(ks * w_s, dk) * v
        n_new = w_old * n0_ref[0, h] + w_s * ks
        num = jnp.sum(_row_to_col(q, dk) * C_new, axis=0, keepdims=True)
        den = jnp.sum(q * n_new, axis=1, keepdims=True)
        hh = num / jnp.maximum(jnp.abs(den), jnp.exp(-m_new))
        c0 = h * dk
        hn_ref[0, :, c0:c0 + dk] = _head_norm_gate(hh, o, hg_ref[:, c0:c0 + dk]).astype(hn_ref.dtype)
        C_ref[0, h] = C_new
        n_ref[0, h] = n_new
        m_ref[0, h] = m_new


def _mlstm_step(z3, g3, bias, hg, C0, n0, m0, *, d_pool, d_ml):
    Bs = C0.shape[0]
    dk = d_ml // MH
    dz = z3.shape[-1]
    st = lambda *shape: pl.BlockSpec((1,) + shape, lambda b: (b,) + (0,) * len(shape))
    return pl.pallas_call(
        functools.partial(_mlstm_step_body, d_pool=d_pool, d_ml=d_ml, dk=dk),
        grid=(Bs,),
        in_specs=[st(1, dz), st(1, LANES),
                  pl.BlockSpec((1, LANES), lambda b: (0, 0)),
                  pl.BlockSpec((1, d_ml), lambda b: (0, 0)),
                  st(MH, dk, dk), st(MH, 1, dk), st(MH, 1, 1)],
        out_specs=[st(1, d_ml), st(MH, dk, dk), st(MH, 1, dk), st(MH, 1, 1)],
        out_shape=[jax.ShapeDtypeStruct((Bs, 1, d_ml), BF16),
                   jax.ShapeDtypeStruct((Bs, MH, dk, dk), F32),
                   jax.ShapeDtypeStruct((Bs, MH, 1, dk), F32),
                   jax.ShapeDtypeStruct((Bs, MH, 1, 1), F32)],
        compiler_params=_params("parallel"), name="mlstm_step")(z3, g3, bias, hg, C0, n0, m0)


KM_ROWS = 16


F32_BIG = 1.0e30


def _moba_seq_body(q_ref, k_ref, v_ref, o_ref, kb_ref, vT_ref, km_ref, qT_ref, pT_ref,
                   m_ref, l_ref, acc_ref, *, nblk, hd):
    blk = MOBA_BLOCK
    R = GROUP * blk
    nchunk = R // LANES
    c_exp = hd ** -0.5 * LOG2E
    neg_inf = jnp.float32(-jnp.inf)

    kb_ref[...] = k_ref[...].astype(BF16)
    vT_ref[...] = v_ref[...].T.astype(BF16)
    km = [jnp.sum(k_ref[n * blk:(n + 1) * blk, :], axis=0, keepdims=True) * (1.0 / blk) for n in range(nblk)]
    km.append(jnp.zeros((KM_ROWS - nblk, hd), F32))
    km_ref[...] = jnp.concatenate(km, axis=0).astype(BF16)
    sub = lax.broadcasted_iota(jnp.int32, (nblk, R), 0)

    def own_block(j, buf, p_rows_ref):
        ms, ls = [], []
        for c in range(nchunk):
            lanes = slice(c * LANES, (c + 1) * LANES)
            q0 = (c * LANES) % blk
            nk = q0 + LANES
            s = jnp.dot(kb_ref[j * blk:j * blk + nk, :], qT_ref[buf, :, lanes], preferred_element_type=F32)
            key = lax.broadcasted_iota(jnp.int32, (nk, LANES), 0)
            qry = lax.broadcasted_iota(jnp.int32, (nk, LANES), 1) + q0
            s = jnp.where(key <= qry, s, neg_inf)
            m = jnp.max(s, axis=0, keepdims=True)
            p = jnp.exp2((s - m) * c_exp)
            ms.append(m)
            ls.append(jnp.sum(p, axis=0, keepdims=True))
            p_rows_ref[0:nk, lanes] = p.astype(BF16)
            if nk < blk:
                p_rows_ref[nk:blk, lanes] = jnp.zeros((blk - nk, LANES), BF16)
        return ms, ls

    def write_out(j, acc, l_all):
        inv_l = 1.0 / l_all
        for g in range(GROUP):
            cols = slice(g * blk, (g + 1) * blk)
            o_ref[j * blk:(j + 1) * blk, g * hd:(g + 1) * hd] = (acc[:, cols] * inv_l[:, cols]).T.astype(o_ref.dtype)

    for j in range(nblk):
        buf = j % 2
        for g in range(GROUP):
            qT_ref[buf, :, g * blk:(g + 1) * blk] = (
                q_ref[j * blk:(j + 1) * blk, g * hd:(g + 1) * hd].astype(F32).T.astype(BF16))

        sel = None
        if j > MOBA_TOPK:
            gate = jnp.dot(km_ref[...], qT_ref[buf], preferred_element_type=F32)[0:nblk, :]
            rows = []
            for n in range(j):
                gn = gate[n:n + 1, :]
                beats = ((gate > gn) | ((gate == gn) & (sub < n))) & (sub < j)
                rows.append((jnp.sum(beats.astype(F32), axis=0, keepdims=True) < MOBA_TOPK).astype(F32))
            sel = rows

        ms, ls = own_block(j, buf, pT_ref.at[buf, j * blk:(j + 1) * blk])
        for n in range(j):
            kb = kb_ref[n * blk:(n + 1) * blk, :]
            for g in range(GROUP):
                s2 = jnp.dot(kb, qT_ref[buf, :, g * blk:(g + 1) * blk], preferred_element_type=F32)
                for half in range(blk // LANES):
                    c = g * (blk // LANES) + half
                    lanes = slice(c * LANES, (c + 1) * LANES)
                    s = s2[:, half * LANES:(half + 1) * LANES]
                    if sel is not None:
                        s = jnp.where(sel[n][:, lanes] > 0.5, s, neg_inf)
                    p = jnp.exp2((s - ms[c]) * c_exp)
                    ls[c] = ls[c] + jnp.sum(p, axis=0, keepdims=True)
                    pT_ref[buf, n * blk:(n + 1) * blk, lanes] = p.astype(BF16)
        nkeys = (j + 1) * blk
        acc = jnp.dot(vT_ref[:, 0:nkeys], pT_ref[buf, 0:nkeys, :], preferred_element_type=F32)
        l_all = jnp.concatenate(ls, axis=1)
        write_out(j, acc, l_all)

        if j > 0:
            @pl.when(jnp.logical_not(jnp.max(l_all) < F32_BIG))
            def _(j=j, buf=buf, sel=sel):
                ms, ls = own_block(j, buf, pT_ref.at[buf, 0:blk])
                m_ref[...] = jnp.concatenate(ms, axis=1)
                l_ref[...] = jnp.concatenate(ls, axis=1)
                acc_ref[...] = jnp.dot(vT_ref[:, j * blk:(j + 1) * blk], pT_ref[buf, 0:blk, :],
                                       preferred_element_type=F32)
                selmat = None if sel is None else jnp.concatenate(
                    sel + [jnp.zeros((nblk - j, R), F32)], axis=0)

                def body(n, carry):
                    start = pl.multiple_of(n * blk, blk)
                    kb = kb_ref[pl.ds(start, blk), :]
                    vtb = vT_ref[:, pl.ds(start, blk)]
                    if selmat is not None:
                        sel_n = jnp.sum(jnp.where(sub == n, selmat, 0.0), axis=0, keepdims=True)
                    for c in range(nchunk):
                        lanes = slice(c * LANES, (c + 1) * LANES)
                        s = jnp.dot(kb, qT_ref[buf, :, lanes], preferred_element_type=F32)
                        if selmat is not None:
                            s = jnp.where(sel_n[:, lanes] > 0.5, s, neg_inf)
                        m_old = m_ref[:, lanes]
                        m_new = jnp.maximum(m_old, jnp.max(s, axis=0, keepdims=True))
                        alpha = jnp.exp2((m_old - m_new) * c_exp)
                        p = jnp.exp2((s - m_new) * c_exp)
                        m_ref[:, lanes] = m_new
                        l_ref[:, lanes] = alpha * l_ref[:, lanes] + jnp.sum(p, axis=0, keepdims=True)
                        acc_ref[:, lanes] = acc_ref[:, lanes] * alpha + jnp.dot(
                            vtb, p.astype(BF16), preferred_element_type=F32)
                    return carry

                lax.fori_loop(0, j, body, 0)
                write_out(j, acc_ref[...], l_ref[...])


def _moba_seq(q, k, v, *, B, S, hd):
    assert S % MOBA_BLOCK == 0 and S // MOBA_BLOCK <= KM_ROWS
    nq = S // MOBA_BLOCK
    gw = GROUP * hd
    R = GROUP * MOBA_BLOCK
    return pl.pallas_call(
        functools.partial(_moba_seq_body, nblk=nq, hd=hd),
        grid=(B, N_KV_HEADS),
        in_specs=[pl.BlockSpec((S, gw), lambda b, h: (b, h)),
                  pl.BlockSpec((S, hd), lambda b, h: (b, h)),
                  pl.BlockSpec((S, hd), lambda b, h: (b, h))],
        out_specs=pl.BlockSpec((S, gw), lambda b, h: (b, h)),
        out_shape=jax.ShapeDtypeStruct((B * S, N_HEADS * hd), BF16),
        scratch_shapes=[pltpu.VMEM((S, hd), BF16),
                        pltpu.VMEM((hd, S), BF16),
                        pltpu.VMEM((KM_ROWS, hd), BF16),
                        pltpu.VMEM((2, hd, R), BF16),
                        pltpu.VMEM((2, S, R), BF16),
                        pltpu.VMEM((1, R), F32),
                        pltpu.VMEM((1, R), F32),
                        pltpu.VMEM((hd, R), F32)],
        compiler_params=_params("parallel", "parallel"), name="moba_seq")(q, k, v)


PAGES_PER_STEP = 16
PAGES_PER_BLOCK = MOBA_BLOCK // PAGE_SIZE
TOKENS_PER_TILE = SUBLANES // N_KV_HEADS
KV_SHIFT = N_KV_HEADS.bit_length() - 1
GROUP_SHIFT = GROUP.bit_length() - 1
assert 1 << KV_SHIFT == N_KV_HEADS and 1 << GROUP_SHIFT == GROUP and TOKENS_PER_TILE * N_KV_HEADS == SUBLANES


def _block_mean_keys(page_refs, o_ref):
    rows, hd = page_refs[0].shape[2:]
    for blk in range(PAGES_PER_STEP // PAGES_PER_BLOCK):
        acc = None
        for r in range(PAGES_PER_BLOCK):
            pg = page_refs[blk * PAGES_PER_BLOCK + r][0, 0]
            part = jnp.sum(pg.reshape(rows // SUBLANES, SUBLANES, hd), axis=0)
            acc = part if acc is None else acc + part
        per_head = acc[0:N_KV_HEADS]
        for t in range(1, TOKENS_PER_TILE):
            per_head = per_head + acc[t * N_KV_HEADS:(t + 1) * N_KV_HEADS]
        o_ref[0, 0, blk * N_KV_HEADS:(blk + 1) * N_KV_HEADS, :] = per_head * (1.0 / MOBA_BLOCK)


def _topk_body(q_ref, km_ref, idx_ref, *, ncol):
    qb = q_ref[0].astype(BF16)
    km = km_ref[0, 0].astype(BF16)
    gate = lax.dot_general(qb, km, (((1,), (1,)), ((), ())), preferred_element_type=F32)
    head = lax.broadcasted_iota(jnp.int32, (N_HEADS, ncol), 0)
    col = lax.broadcasted_iota(jnp.int32, (N_HEADS, ncol), 1)
    gate = jnp.where((col & (N_KV_HEADS - 1)) == (head >> GROUP_SHIFT), gate, -jnp.inf)
    out_lane = lax.broadcasted_iota(jnp.int32, (N_HEADS, LANES), 1)
    out = jnp.zeros((N_HEADS, LANES), jnp.int32)
    for r in range(MOBA_TOPK):
        mx = jnp.max(gate, axis=1, keepdims=True)
        idx = jnp.min(jnp.where(gate == mx, col, ncol), axis=1, keepdims=True)
        out = jnp.where(out_lane == r, idx >> KV_SHIFT, out)
        gate = jnp.where(col == idx, -jnp.inf, gate)
    idx_ref[0] = out


def _topk_blocks(q3, kmean, layer):
    Bs, _, hd = q3.shape
    ncol = kmean.shape[2]
    assert ncol >= MOBA_TOPK * N_KV_HEADS
    return pl.pallas_call(
        functools.partial(_topk_body, ncol=ncol),
        grid=(Bs,),
        in_specs=[pl.BlockSpec((1, N_HEADS, hd), lambda b: (b, 0, 0)),
                  pl.BlockSpec((1, 1, ncol, hd), lambda b: (layer, b, 0, 0))],
        out_specs=pl.BlockSpec((1, N_HEADS, LANES), lambda b: (b, 0, 0)),
        out_shape=jax.ShapeDtypeStruct((Bs, N_HEADS, LANES), jnp.int32),
        compiler_params=_params("parallel"), name="topk_blocks")(q3, kmean)


N_SEL_PAGES = MOBA_TOPK * PAGES_PER_BLOCK


def _moba_step_body(idx_ref, pt_ref, q_ref, kn_ref, vn_ref, *refs, hd):
    n_in = GROUP * N_SEL_PAGES
    k_refs, v_refs, o_ref = refs[:n_in], refs[n_in:2 * n_in], refs[2 * n_in]
    kv_head = pl.program_id(1)
    scale = hd ** -0.5
    rows = SAMPLE_ROWS
    ncol = k_refs[0].shape[2]
    mine = (lax.broadcasted_iota(jnp.int32, (rows, ncol), 1) & (N_KV_HEADS - 1)) == kv_head
    k_new = kn_ref[0, 0].astype(BF16).astype(F32)
    v_new = vn_ref[0, 0].astype(BF16).astype(F32)
    for g in range(GROUP):
        qb = jnp.broadcast_to(q_ref[0, g], (rows, hd)).astype(BF16)
        pages = slice(g * N_SEL_PAGES, (g + 1) * N_SEL_PAGES)
        s_pages = [jnp.where(mine,
                             lax.dot_general(qb, kr[0, 0].astype(BF16), (((1,), (1,)), ((), ())),
                                             preferred_element_type=F32) * scale,
                             -jnp.inf) for kr in k_refs[pages]]
        s_own = jnp.sum(qb.astype(F32) * k_new, axis=1, keepdims=True) * scale
        m = s_own
        for s in s_pages:
            m = jnp.maximum(m, jnp.max(s, axis=1, keepdims=True))
        p_own = jnp.exp(s_own - m)
        l = p_own
        acc = p_own * v_new
        for s, vr in zip(s_pages, v_refs[pages]):
            p = jnp.exp(s - m)
            l = l + jnp.sum(p, axis=1, keepdims=True)
            acc = acc + jnp.dot(p.astype(BF16), vr[0, 0].astype(BF16), preferred_element_type=F32)
        o_ref[0, g] = (acc / l)[0:1, :].astype(o_ref.dtype)


def _moba_step(q4, kn4, vn4, cache_k4, cache_v4, idx_flat, pt_flat, *, layer, n_pages):
    Bs, _, _, hd = q4.shape
    rows = cache_k4.shape[2]

    def page_spec(r):
        g, r = divmod(r, N_SEL_PAGES)
        sel, half = divmod(r, PAGES_PER_BLOCK)

        def index_map(b, h, idx, pt):
            blk = idx[(b * N_HEADS + h * GROUP + g) * MOBA_TOPK + sel]
            return (layer, pt[b * n_pages + blk * PAGES_PER_BLOCK + half], 0, 0)

        return pl.BlockSpec((1, 1, rows, hd), index_map)

    group_spec = pl.BlockSpec((1, GROUP, 1, hd), lambda b, h, idx, pt: (b, h, 0, 0))
    kv_spec = pl.BlockSpec((1, 1, 1, hd), lambda b, h, idx, pt: (b, h, 0, 0))
    n_in = GROUP * N_SEL_PAGES
    return pl.pallas_call(
        functools.partial(_moba_step_body, hd=hd),
        grid_spec=pltpu.PrefetchScalarGridSpec(
            num_scalar_prefetch=2, grid=(Bs, N_KV_HEADS),
            in_specs=[group_spec, kv_spec, kv_spec] + [page_spec(r) for r in range(n_in)] * 2,
            out_specs=group_spec),
        out_shape=jax.ShapeDtypeStruct((Bs, N_HEADS, 1, hd), BF16),
        compiler_params=_params("parallel", "arbitrary"), name="moba_step")(
            idx_flat, pt_flat, q4, kn4, vn4, *([cache_k4] * n_in), *([cache_v4] * n_in))


def _pad_rows(a, rows):
    return jnp.pad(a, ((0, rows - a.shape[0]),) + ((0, 0),) * (a.ndim - 1))


def kernel(x_prompt, x_sample, cache_k, cache_v, state_pool, state_C, state_n, state_m, page_table,
           w_in_mix, b_if, w_pool, pool_scale, mlstm_norm_g, w_out_mix, w_qkv, w_o,
           ln_mix_g, ln_mix_b, ln_ffn_g, ln_ffn_b, w_up, w_down):
    B, S, D = x_prompt.shape
    Bs = x_sample.shape[0]
    assert x_sample.shape[1] == 1 and Bs <= SAMPLE_ROWS
    depth = w_up.shape[0]
    d_pool = state_pool.shape[-1]
    dk = state_C.shape[-1]
    d_ml = MH * dk
    d_main = d_pool + 4 * d_ml
    assert w_in_mix.shape[2] == d_main + 2 * MH
    hd = cache_k.shape[-1]
    d_q, d_kv = N_HEADS * hd, N_KV_HEADS * hd
    n_pages = page_table.shape[1]
    past_len = n_pages * PAGE_SIZE
    assert cache_k.shape[2] == PAGE_SIZE and past_len % MOBA_BLOCK == 0
    alpha = (2 * depth) ** 0.25
    M = B * S
    L = math.gcd(S, MLSTM_CHUNK)

    w_in_t = jnp.swapaxes(w_in_mix, 1, 2)
    bias_g = jnp.pad(b_if, ((0, 0), (0, LANES - 2 * MH)))[:, None, :]
    w_pool_b = w_pool.astype(BF16)

    cache_k4 = cache_k.reshape(cache_k.shape[:2] + (PAGE_SIZE * N_KV_HEADS, hd))
    cache_v4 = cache_v.reshape(cache_v.shape[:2] + (PAGE_SIZE * N_KV_HEADS, hd))
    pt_flat = page_table.reshape(-1)
    kmean_past = None

    xs = _pad_rows(x_sample.reshape(Bs, D), SAMPLE_ROWS)
    xp = x_prompt.reshape(M, D)
    ks_, vs_, pool_s, C_s, n_s, m_s = [], [], [], [], [], []
    pool_p, C_p, n_p, m_p = [], [], [], []
    n_attn = cache_k.shape[0]
    kv_flat = None
    for l in range(depth):
        i = l // 2
        ln_g, ln_b = ln_mix_g[l][None, :], ln_mix_b[l][None, :]
        fg, fb = ln_ffn_g[l][None, :], ln_ffn_b[l][None, :]
        if l % 2 == 0:
            sc = pool_scale[i][None, :]
            hg = mlstm_norm_g[i][None, :]
            zs, gs, wb, wgb = _proj_gates_step(xs, w_in_t, i, n_main=d_main, n_gate=2 * MH, tn=STEP_TN)
            us = zs[:Bs, :d_pool]
            ys_pool = _pool_step(jnp.transpose(state_pool[i], (1, 0, 2)), us, w_pool_b[i], sc, pos0=past_len)
            hns, C, n, m = _mlstm_step(
                zs[:Bs, None, :], gs[:Bs, None, :], bias_g[i], hg, state_C[i],
                state_n[i][:, :, None, :], state_m[i][:, :, None, None], d_pool=d_pool, d_ml=d_ml)
            a = _pad_rows(jnp.concatenate([ys_pool, hns[:, 0, :]], axis=1), SAMPLE_ROWS)
            xs, wpb = _proj_ln_step(a, w_out_mix, i, xs, ln_g, ln_b, alpha=alpha, tk=STEP_TK)
            pool_s.append(jnp.concatenate([state_pool[i][:, 1:, :], us[:, None, :]], axis=1))
            C_s.append(C)
            n_s.append(n[:, :, 0, :])
            m_s.append(m[:, :, 0, 0])
        else:
            qs, k_new, v_new, wb = _proj_qkv_step(xs, w_qkv, i, d_q=d_q, d_kv=d_kv)
            q_s = qs[:Bs].astype(F32).reshape(Bs, N_HEADS, hd)
            k_new = k_new[:Bs].reshape(Bs, N_KV_HEADS, 1, hd)
            v_new = v_new[:Bs].reshape(Bs, N_KV_HEADS, 1, hd)
            idx = _topk_blocks(q_s, kmean_past, i)[:, :, :MOBA_TOPK].reshape(-1)
            os_ = _moba_step(q_s[:, :, None, :], k_new, v_new, cache_k4, cache_v4, idx, pt_flat,
                             layer=i, n_pages=n_pages)
            xs, wpb = _proj_ln_step(_pad_rows(os_.reshape(Bs, d_q), SAMPLE_ROWS), w_o, i, xs, ln_g, ln_b,
                                    alpha=alpha, tk=STEP_TK)
            ks_.append(k_new.reshape(Bs, 1, N_KV_HEADS, hd))
            vs_.append(v_new.reshape(Bs, 1, N_KV_HEADS, hd))
        if l == 0:
            xs, wub, wdb = _ffn_step(xs, w_up, w_down, l, fg, fb, alpha=alpha, tf=STEP_TF)
        else:
            xs, = _ffn(xs, wub[None], wdb[None], 0, fg, fb, alpha=alpha, tm=SAMPLE_ROWS, tf=FFN_TF)

        if l % 2 == 0:
            z, gates = _proj_gates(xp, wb, wgb, tm=PROJ_TM, tn=PROJ_TN)
            y_pool = _pool_prompt(z, w_pool_b, i, sc, B=B, S=S, d_pool=d_pool, T=POOL_T)
            hn, C, n, m = _mlstm_prompt(z, gates, bias_g[i], hg, B=B, S=S, d_pool=d_pool, d_ml=d_ml, L=L)
            xp = _proj_ln([y_pool, hn], wpb[None], 0, xp, ln_g, ln_b, alpha=alpha, tm=LN_TM)
            pool_p.append(z.reshape(B, S, d_main)[:, S - POOL_BUF:, :d_pool])
            C_p.append(C)
            n_p.append(n.reshape(B, MH, dk))
            m_p.append(m[:, :, 0, 0])
        else:
            q, k, v, *kv_flat = _proj_qkv(xp, wb[None], 0, d_q=d_q, d_kv=d_kv, hd=hd, tm=PROJ_TM,
                                          slab=i, n_slabs=n_attn, prev=kv_flat)
            o = _moba_seq(q, k, v, B=B, S=S, hd=hd)
            xp = _proj_ln([o], wpb[None], 0, xp, ln_g, ln_b, alpha=alpha, tm=LN_TM)
        res = _ffn(xp, wub[None], wdb[None], 0, fg, fb, alpha=alpha, tm=FFN_TM, tf=FFN_TF,
                   paged=(cache_k4, pt_flat, Bs, n_pages) if kmean_past is None else None,
                   cast_next=(w_up, w_down, l + 1) if l + 1 < depth else None)
        xp = res[0]
        if kmean_past is None:
            kmean_past = res[1]
        if l + 1 < depth:
            wub, wdb = res[-2:]

    k_prompt, v_prompt = (t.reshape(n_attn, B, S, N_KV_HEADS, hd) for t in kv_flat)
    return (xp.reshape(B, S, D), xs[:Bs].reshape(Bs, 1, D),
            k_prompt, v_prompt, jnp.stack(ks_), jnp.stack(vs_),
            jnp.stack(pool_p), jnp.stack(C_p), jnp.stack(n_p), jnp.stack(m_p),
            jnp.stack(pool_s), jnp.stack(C_s), jnp.stack(n_s), jnp.stack(m_s))
```

```python
import functools
import math

import jax
import jax.numpy as jnp
from jax import lax
from jax.experimental import pallas as pl
from jax.experimental.pallas import tpu as pltpu

F32 = jnp.float32
BF16 = jnp.bfloat16

POOL_WINDOWS = (2, 4, 8, 16)
POOL_BUF = max(POOL_WINDOWS) - 1
POOL_HALO = 16
MH = 4
N_HEADS = 16
N_KV_HEADS = 4
GROUP = N_HEADS // N_KV_HEADS
MOBA_BLOCK = 256
MOBA_TOPK = 3
PAGE_SIZE = 128
LN_EPS = 1e-5
HEAD_NORM_EPS = 1e-6
LANES = 128
SUBLANES = 8
SAMPLE_ROWS = 16
LOG2E = 1.4426950408889634

VMEM_LIMIT = 56 * 1024 * 1024

PROJ_TM, PROJ_TN = 1024, 1024
LN_TM = 512
LN_SUB_ROWS = 256
FFN_TM, FFN_TF = 512, 1024
STEP_TN, STEP_TK, STEP_TF = 1024, 512, 512
POOL_T = 256
MLSTM_CHUNK = 512
MLSTM_SEQS_PER_STEP = 1


def _params(*sem):
    return pltpu.CompilerParams(dimension_semantics=sem, vmem_limit_bytes=VMEM_LIMIT)


def _layer_norm(y, g, b):
    mu = jnp.mean(y, axis=-1, keepdims=True)
    yc = y - mu
    var = jnp.mean(yc * yc, axis=-1, keepdims=True)
    return yc * lax.rsqrt(var + LN_EPS) * g + b


def _log_sigmoid(x):
    return jnp.minimum(x, 0.0) - jnp.log1p(jnp.exp(-jnp.abs(x)))


_NT = (((1,), (1,)), ((), ()))


def _proj_gates_body(x_ref, w_ref, wg_ref, o_ref, g_ref, xb_ref):
    @pl.when(pl.program_id(1) == 0)
    def _():
        xb = x_ref[...].astype(BF16)
        xb_ref[...] = xb
        g_ref[...] = lax.dot_general(xb, wg_ref[...], _NT, preferred_element_type=F32)

    o_ref[...] = lax.dot_general(xb_ref[...], w_ref[...], _NT, preferred_element_type=F32)


def _proj_gates(x, wt, wgt, *, tm, tn):
    M, K = x.shape
    n_main = wt.shape[0]
    return pl.pallas_call(
        _proj_gates_body, grid=(M // tm, n_main // tn),
        in_specs=[pl.BlockSpec((tm, K), lambda i, j: (i, 0)),
                  pl.BlockSpec((tn, K), lambda i, j: (j, 0)),
                  pl.BlockSpec((LANES, K), lambda i, j: (0, 0))],
        out_specs=[pl.BlockSpec((tm, tn), lambda i, j: (i, j)),
                   pl.BlockSpec((tm, LANES), lambda i, j: (i, 0))],
        out_shape=[jax.ShapeDtypeStruct((M, n_main), F32), jax.ShapeDtypeStruct((M, LANES), F32)],
        scratch_shapes=[pltpu.VMEM((tm, K), BF16)],
        compiler_params=_params("parallel", "arbitrary"), name="proj_gates")(x, wt, wgt)


def _proj_qkv_body(x_ref, w_ref, *rest, nq_tiles, n_alias):
    q_ref, k_ref, v_ref, kf_ref, vf_ref, xb_ref = rest[n_alias:]
    j = pl.program_id(1)

    @pl.when(j == 0)
    def _():
        xb_ref[...] = x_ref[...].astype(BF16)

    y = jnp.dot(xb_ref[...], w_ref[...], preferred_element_type=F32)

    @pl.when(j < nq_tiles)
    def _():
        q_ref[...] = y.astype(q_ref.dtype)

    @pl.when(j == nq_tiles)
    def _():
        tm, d_kv = k_ref.shape
        hd = kf_ref.shape[1]
        nh = d_kv // hd
        k_ref[...] = y[:, :d_kv]
        v_ref[...] = y[:, d_kv:]
        for h in range(nh):
            kf_ref[pl.ds(h, tm, stride=nh), :] = y[:, h * hd:(h + 1) * hd]
            vf_ref[pl.ds(h, tm, stride=nh), :] = y[:, d_kv + h * hd:d_kv + (h + 1) * hd]


def _proj_qkv(x, w_all, layer, *, d_q, d_kv, hd, tm, slab, n_slabs, prev=None):
    M, K = x.shape
    tn = 2 * d_kv
    nq_tiles = d_q // tn
    nh = d_kv // hd
    assert nq_tiles * tn == d_q
    flat_spec = pl.BlockSpec((None, tm * nh, hd), lambda i, j: (slab, i, 0))
    flat_shape = jax.ShapeDtypeStruct((n_slabs, M * nh, hd), F32)
    in_specs = [pl.BlockSpec((tm, K), lambda i, j: (i, 0)),
                pl.BlockSpec((None, K, tn), lambda i, j: (layer, 0, j))]
    args = [x, w_all]
    aliases = {}
    if prev is not None:
        in_specs += [pl.BlockSpec(memory_space=pl.ANY)] * 2
        args += list(prev)
        aliases = {2: 3, 3: 4}
    return pl.pallas_call(
        functools.partial(_proj_qkv_body, nq_tiles=nq_tiles, n_alias=len(aliases)),
        grid=(M // tm, nq_tiles + 1),
        in_specs=in_specs,
        out_specs=[pl.BlockSpec((tm, tn), lambda i, j: (i, jnp.minimum(j, nq_tiles - 1))),
                   pl.BlockSpec((tm, d_kv), lambda i, j: (i, 0)),
                   pl.BlockSpec((tm, d_kv), lambda i, j: (i, 0)),
                   flat_spec, flat_spec],
        out_shape=[jax.ShapeDtypeStruct((M, d_q), BF16),
                   jax.ShapeDtypeStruct((M, d_kv), F32),
                   jax.ShapeDtypeStruct((M, d_kv), F32),
                   flat_shape, flat_shape],
        scratch_shapes=[pltpu.VMEM((tm, K), BF16)],
        input_output_aliases=aliases,
        compiler_params=_params("parallel", "arbitrary"), name="proj_qkv")(*args)


def _proj_ln_body(*refs, n_a, alpha):
    a_refs = refs[:n_a]
    w_ref, res_ref, g_ref, b_ref, o_ref = refs[n_a:]
    tm = res_ref.shape[0]
    sub = min(tm, LN_SUB_ROWS)
    for r0 in range(0, tm, sub):
        rows = slice(r0, r0 + sub)
        y = alpha * res_ref[rows, :]
        k0 = 0
        for a_ref in a_refs:
            ka = a_ref.shape[1]
            y = y + jnp.dot(a_ref[rows, :], w_ref[k0:k0 + ka, :], preferred_element_type=F32)
            k0 += ka
        o_ref[rows, :] = _layer_norm(y, g_ref[...], b_ref[...])


def _proj_ln(a_list, w_all, layer, res, g, b, *, alpha, tm):
    M, D = res.shape
    K = w_all.shape[1]
    in_specs = [pl.BlockSpec((tm, a.shape[1]), lambda i: (i, 0)) for a in a_list]
    in_specs += [pl.BlockSpec((None, K, D), lambda i: (layer, 0, 0)),
                 pl.BlockSpec((tm, D), lambda i: (i, 0)),
                 pl.BlockSpec((1, D), lambda i: (0, 0)),
                 pl.BlockSpec((1, D), lambda i: (0, 0))]
    return pl.pallas_call(
        functools.partial(_proj_ln_body, n_a=len(a_list), alpha=alpha),
        grid=(M // tm,), in_specs=in_specs,
        out_specs=pl.BlockSpec((tm, D), lambda i: (i, 0)),
        out_shape=jax.ShapeDtypeStruct((M, D), F32),
        compiler_params=_params("parallel"), name="proj_ln")(*a_list, w_all, res, g, b)


def _ffn_body(*refs, alpha, n_page_refs, cast_next):
    if n_page_refs:
        refs = refs[1:]
    x_ref, wu_ref, wd_ref, g_ref, b_ref = refs[:5]
    refs = refs[5:]
    page_refs, refs = refs[:n_page_refs], refs[n_page_refs:]
    next_refs, refs = (refs[:2], refs[2:]) if cast_next else ((), refs)
    o_ref, refs = refs[0], refs[1:]
    km_ref, refs = (refs[0], refs[1:]) if n_page_refs else (None, refs)
    next_out_refs, refs = (refs[:2], refs[2:]) if cast_next else ((), refs)
    xb_ref, acc_ref = refs
    f = pl.program_id(1)
    last = pl.num_programs(1) - 1

    def hidden(rows):
        h = jnp.dot(xb_ref[rows, :], wu_ref[...], preferred_element_type=F32)
        h = jnp.square(jnp.maximum(h, 0.0)).astype(BF16)
        return jnp.dot(h, wd_ref[...], preferred_element_type=F32)

    def page_stream():
        if n_page_refs:
            _block_mean_keys(page_refs, km_ref)
        for src, dst in zip(next_refs, next_out_refs):
            dst[...] = src[...].astype(BF16)

    @pl.when(f == 0)
    def _():
        xb_ref[...] = x_ref[...].astype(BF16)
        acc_ref[...] = jnp.zeros_like(acc_ref)

    @pl.when(f < last)
    def _():
        acc_ref[...] += hidden(slice(None))
        page_stream()

    @pl.when(f == last)
    def _():
        page_stream()
        tm = x_ref.shape[0]
        sub = min(tm, LN_SUB_ROWS)
        for r0 in range(0, tm, sub):
            rows = slice(r0, r0 + sub)
            y = alpha * x_ref[rows, :] + (acc_ref[rows, :] + hidden(rows))
            o_ref[rows, :] = _layer_norm(y, g_ref[...], b_ref[...])


def _ffn(x, wu_all, wd_all, layer, g, b, *, alpha, tm, tf, paged=None, cast_next=None):
    M, D = x.shape
    Fd = wu_all.shape[2]
    grid = (M // tm, Fd // tf)
    n_steps = grid[0] * grid[1]
    in_specs = [pl.BlockSpec((tm, D), lambda i, f, *_: (i, 0)),
                pl.BlockSpec((None, D, tf), lambda i, f, *_: (layer, 0, f)),
                pl.BlockSpec((None, tf, D), lambda i, f, *_: (layer, f, 0)),
                pl.BlockSpec((1, D), lambda i, f, *_: (0, 0)),
                pl.BlockSpec((1, D), lambda i, f, *_: (0, 0))]
    args = [x, wu_all, wd_all, g, b]
    out_specs = [pl.BlockSpec((tm, D), lambda i, f, *_: (i, 0))]
    out_shape = [jax.ShapeDtypeStruct((M, D), F32)]
    scratch = [pltpu.VMEM((tm, D), BF16), pltpu.VMEM((tm, D), F32)]
    n_page_refs = 0
    if paged is not None:
        cache_k4, pt_flat, Bs, n_pages = paged
        n_attn, _, rows, hd = cache_k4.shape
        assert grid == (n_attn * Bs, n_pages // PAGES_PER_STEP)
        out_rows = PAGES_PER_STEP // PAGES_PER_BLOCK * N_KV_HEADS
        n_page_refs = PAGES_PER_STEP

        def page_spec(r):
            return pl.BlockSpec(
                (1, 1, rows, hd),
                lambda i, f, pt: (i // Bs, pt[(i % Bs) * n_pages + f * PAGES_PER_STEP + r], 0, 0))

        in_specs += [page_spec(r) for r in range(PAGES_PER_STEP)]
        args += [cache_k4] * PAGES_PER_STEP
        out_specs.append(pl.BlockSpec((1, 1, out_rows, hd), lambda i, f, pt: (i // Bs, i % Bs, f, 0)))
        out_shape.append(jax.ShapeDtypeStruct((n_attn, Bs, grid[1] * out_rows, hd), F32))
    if cast_next is not None:
        wu_next, wd_next, nxt = cast_next
        for w in (wu_next, wd_next):
            rows_w, cols_w = w.shape[1:]
            rs = rows_w // n_steps
            assert rs * n_steps == rows_w and rs % SAMPLE_ROWS == 0
            in_specs.append(pl.BlockSpec((None, rs, cols_w), lambda i, f, *_: (nxt, i * grid[1] + f, 0)))
            args.append(w)
            out_specs.append(pl.BlockSpec((rs, cols_w), lambda i, f, *_: (i * grid[1] + f, 0)))
            out_shape.append(jax.ShapeDtypeStruct((rows_w, cols_w), BF16))
    body = functools.partial(_ffn_body, alpha=alpha, n_page_refs=n_page_refs, cast_next=cast_next is not None)
    riders = paged is not None or cast_next is not None
    params = _params("arbitrary" if riders else "parallel", "arbitrary")
    if paged is None:
        return pl.pallas_call(body, grid=grid, in_specs=in_specs, out_specs=out_specs, out_shape=out_shape,
                              scratch_shapes=scratch, compiler_params=params, name="ffn")(*args)
    return pl.pallas_call(
        body,
        grid_spec=pltpu.PrefetchScalarGridSpec(num_scalar_prefetch=1, grid=grid, in_specs=in_specs,
                                               out_specs=out_specs, scratch_shapes=scratch),
        out_shape=out_shape, compiler_params=params, name="ffn_kmean")(pt_flat, *args)


def _cast_tile(w_ref, wb_ref):
    wb = w_ref[...].astype(BF16)
    wb_ref[...] = wb
    return wb


def _proj_gates_step_body(x_ref, w_ref, wg_ref, o_ref, g_ref, wb_ref, wgb_ref):
    xb = x_ref[...].astype(BF16)

    @pl.when(pl.program_id(0) == 0)
    def _():
        n_gate, K = wg_ref.shape
        wg = jnp.concatenate([wg_ref[...], jnp.zeros((LANES - n_gate, K), F32)], axis=0).astype(BF16)
        wgb_ref[...] = wg
        g_ref[...] = lax.dot_general(xb, wg, _NT, preferred_element_type=F32)

    o_ref[...] = lax.dot_general(xb, _cast_tile(w_ref, wb_ref), _NT, preferred_element_type=F32)


def _proj_gates_step(x, wt_all, layer, *, n_main, n_gate, tn):
    R, K = x.shape
    assert n_main % n_gate == 0 and n_gate % SUBLANES == 0
    return pl.pallas_call(
        _proj_gates_step_body, grid=(n_main // tn,),
        in_specs=[pl.BlockSpec((R, K), lambda j: (0, 0)),
                  pl.BlockSpec((None, tn, K), lambda j: (layer, j, 0)),
                  pl.BlockSpec((None, n_gate, K), lambda j: (layer, n_main // n_gate, 0))],
        out_specs=[pl.BlockSpec((R, tn), lambda j: (0, j)),
                   pl.BlockSpec((R, LANES), lambda j: (0, 0)),
                   pl.BlockSpec((tn, K), lambda j: (j, 0)),
                   pl.BlockSpec((LANES, K), lambda j: (0, 0))],
        out_shape=[jax.ShapeDtypeStruct((R, n_main), F32), jax.ShapeDtypeStruct((R, LANES), F32),
                   jax.ShapeDtypeStruct((n_main, K), BF16), jax.ShapeDtypeStruct((LANES, K), BF16)],
        compiler_params=_params("arbitrary"), name="proj_gates_step")(x, wt_all, wt_all)


def _proj_qkv_step_body(x_ref, w_ref, q_ref, k_ref, v_ref, wb_ref, *, nq_tiles):
    j = pl.program_id(0)
    y = jnp.dot(x_ref[...].astype(BF16), _cast_tile(w_ref, wb_ref), preferred_element_type=F32)

    @pl.when(j < nq_tiles)
    def _():
        q_ref[...] = y.astype(q_ref.dtype)

    @pl.when(j == nq_tiles)
    def _():
        d_kv = k_ref.shape[1]
        k_ref[...] = y[:, :d_kv]
        v_ref[...] = y[:, d_kv:]


def _proj_qkv_step(x, w_all, layer, *, d_q, d_kv):
    R, K = x.shape
    tn = 2 * d_kv
    nq_tiles = d_q // tn
    assert nq_tiles * tn == d_q
    return pl.pallas_call(
        functools.partial(_proj_qkv_step_body, nq_tiles=nq_tiles),
        grid=(nq_tiles + 1,),
        in_specs=[pl.BlockSpec((R, K), lambda j: (0, 0)),
                  pl.BlockSpec((None, K, tn), lambda j: (layer, 0, j))],
        out_specs=[pl.BlockSpec((R, tn), lambda j: (0, jnp.minimum(j, nq_tiles - 1))),
                   pl.BlockSpec((R, d_kv), lambda j: (0, 0)),
                   pl.BlockSpec((R, d_kv), lambda j: (0, 0)),
                   pl.BlockSpec((K, tn), lambda j: (0, j))],
        out_shape=[jax.ShapeDtypeStruct((R, d_q), BF16),
                   jax.ShapeDtypeStruct((R, d_kv), F32),
                   jax.ShapeDtypeStruct((R, d_kv), F32),
                   jax.ShapeDtypeStruct((K, d_q + 2 * d_kv), BF16)],
        compiler_params=_params("arbitrary"), name="proj_qkv_step")(x, w_all)


def _proj_ln_step_body(a_ref, w_ref, res_ref, g_ref, b_ref, o_ref, wb_ref, acc_ref, *, alpha):
    k = pl.program_id(0)

    @pl.when(k == 0)
    def _():
        acc_ref[...] = alpha * res_ref[...]

    acc_ref[...] += jnp.dot(a_ref[...], _cast_tile(w_ref, wb_ref), preferred_element_type=F32)

    @pl.when(k == pl.num_programs(0) - 1)
    def _():
        o_ref[...] = _layer_norm(acc_ref[...], g_ref[...], b_ref[...])


def _proj_ln_step(a, w_all, layer, res, g, b, *, alpha, tk):
    R, D = res.shape
    K = w_all.shape[1]
    return pl.pallas_call(
        functools.partial(_proj_ln_step_body, alpha=alpha),
        grid=(K // tk,),
        in_specs=[pl.BlockSpec((R, tk), lambda k: (0, k)),
                  pl.BlockSpec((None, tk, D), lambda k: (layer, k, 0)),
                  pl.BlockSpec((R, D), lambda k: (0, 0)),
                  pl.BlockSpec((1, D), lambda k: (0, 0)),
                  pl.BlockSpec((1, D), lambda k: (0, 0))],
        out_specs=[pl.BlockSpec((R, D), lambda k: (0, 0)),
                   pl.BlockSpec((tk, D), lambda k: (k, 0))],
        out_shape=[jax.ShapeDtypeStruct((R, D), F32), jax.ShapeDtypeStruct((K, D), BF16)],
        scratch_shapes=[pltpu.VMEM((R, D), F32)],
        compiler_params=_params("arbitrary"), name="proj_ln_step")(a, w_all, res, g, b)


def _ffn_step_body(x_ref, wu_ref, wd_ref, g_ref, b_ref, o_ref, wub_ref, wdb_ref, acc_ref, *, alpha):
    f = pl.program_id(0)

    @pl.when(f == 0)
    def _():
        acc_ref[...] = jnp.zeros_like(acc_ref)

    h = jnp.dot(x_ref[...].astype(BF16), _cast_tile(wu_ref, wub_ref), preferred_element_type=F32)
    h = jnp.square(jnp.maximum(h, 0.0)).astype(BF16)
    acc_ref[...] += jnp.dot(h, _cast_tile(wd_ref, wdb_ref), preferred_element_type=F32)

    @pl.when(f == pl.num_programs(0) - 1)
    def _():
        y = alpha * x_ref[...] + acc_ref[...]
        o_ref[...] = _layer_norm(y, g_ref[...], b_ref[...])


def _ffn_step(x, wu_all, wd_all, layer, g, b, *, alpha, tf):
    R, D = x.shape
    Fd = wu_all.shape[2]
    return pl.pallas_call(
        functools.partial(_ffn_step_body, alpha=alpha),
        grid=(Fd // tf,),
        in_specs=[pl.BlockSpec((R, D), lambda f: (0, 0)),
                  pl.BlockSpec((None, D, tf), lambda f: (layer, 0, f)),
                  pl.BlockSpec((None, tf, D), lambda f: (layer, f, 0)),
                  pl.BlockSpec((1, D), lambda f: (0, 0)),
                  pl.BlockSpec((1, D), lambda f: (0, 0))],
        out_specs=[pl.BlockSpec((R, D), lambda f: (0, 0)),
                   pl.BlockSpec((D, tf), lambda f: (0, f)),
                   pl.BlockSpec((tf, D), lambda f: (f, 0))],
        out_shape=[jax.ShapeDtypeStruct((R, D), F32),
                   jax.ShapeDtypeStruct((D, Fd), BF16),
                   jax.ShapeDtypeStruct((Fd, D), BF16)],
        scratch_shapes=[pltpu.VMEM((R, D), F32)],
        compiler_params=_params("arbitrary"), name="ffn_step")(x, wu_all, wd_all, g, b)


def _pool_prompt_body(u_ref, wp_ref, sc_ref, y_ref, ext_ref, *, T, gc):
    t = pl.program_id(1)

    @pl.when(t == 0)
    def _():
        ext_ref[0:POOL_HALO, :] = jnp.zeros((POOL_HALO, ext_ref.shape[1]), F32)

    ext_ref[POOL_HALO:POOL_HALO + T, :] = u_ref[...]
    pos = t * T + lax.broadcasted_iota(jnp.int32, (T, 1), 0)
    for gi, w in enumerate(POOL_WINDOWS):
        c0, c1 = gi * gc, (gi + 1) * gc
        tok = ext_ref[POOL_HALO:POOL_HALO + T, c0:c1]
        acc = tok
        for k in range(1, w):
            acc = acc + ext_ref[POOL_HALO - k:POOL_HALO - k + T, c0:c1]
        cnt = jnp.minimum(w, pos + 1).astype(F32)
        d = acc / cnt - tok
        y = jnp.dot(d.astype(BF16), wp_ref[gi], preferred_element_type=F32) * sc_ref[:, c0:c1]
        y_ref[:, c0:c1] = y.astype(y_ref.dtype)
    ext_ref[0:POOL_HALO, :] = ext_ref[T:T + POOL_HALO, :]


def _pool_prompt(z, wp_all, layer, sc, *, B, S, d_pool, T):
    nt = S // T
    gc = d_pool // len(POOL_WINDOWS)
    return pl.pallas_call(
        functools.partial(_pool_prompt_body, T=T, gc=gc),
        grid=(B, nt),
        in_specs=[pl.BlockSpec((T, d_pool), lambda b, t: (b * nt + t, 0)),
                  pl.BlockSpec((None,) + wp_all.shape[1:], lambda b, t: (layer, 0, 0, 0)),
                  pl.BlockSpec((1, d_pool), lambda b, t: (0, 0))],
        out_specs=pl.BlockSpec((T, d_pool), lambda b, t: (b * nt + t, 0)),
        out_shape=jax.ShapeDtypeStruct((B * S, d_pool), BF16),
        scratch_shapes=[pltpu.VMEM((POOL_HALO + T, d_pool), F32)],
        compiler_params=_params("parallel", "arbitrary"), name="pool_prompt")(z, wp_all, sc)


def _pool_step_body(buf_ref, u_ref, wp_ref, sc_ref, y_ref, *, gc, pos0):
    u = u_ref[...]
    for gi, w in enumerate(POOL_WINDOWS):
        c0, c1 = gi * gc, (gi + 1) * gc
        tok = u[:, c0:c1]
        acc = tok
        for k in range(1, w):
            acc = acc + buf_ref[POOL_BUF - k, :, c0:c1]
        d = acc / float(min(w, pos0 + 1)) - tok
        y = jnp.dot(d.astype(BF16), wp_ref[gi], preferred_element_type=F32) * sc_ref[:, c0:c1]
        y_ref[:, c0:c1] = y.astype(y_ref.dtype)


def _pool_step(buf_t, u, wp, sc, *, pos0):
    Bs, d_pool = u.shape
    gc = d_pool // len(POOL_WINDOWS)
    return pl.pallas_call(
        functools.partial(_pool_step_body, gc=gc, pos0=pos0),
        out_shape=jax.ShapeDtypeStruct((Bs, d_pool), BF16),
        compiler_params=pltpu.CompilerParams(vmem_limit_bytes=VMEM_LIMIT),
        name="pool_step")(buf_t, u, wp, sc)


def _head_norm_gate(h, o, g):
    mu = jnp.mean(h, axis=-1, keepdims=True)
    hc = h - mu
    var = jnp.mean(hc * hc, axis=-1, keepdims=True)
    return hc * lax.rsqrt(var + HEAD_NORM_EPS) * g * jax.nn.sigmoid(o)


def _mlstm_prompt_body(q_ref, k_ref, v_ref, o_ref, gt_ref, bias_ref, hg_ref,
                       hn_ref, C_ref, n_ref, m_ref, *, L, dk):
    c = pl.program_id(1)

    @pl.when(c == 0)
    def _():
        C_ref[...] = jnp.zeros_like(C_ref)
        n_ref[...] = jnp.zeros_like(n_ref)
        m_ref[...] = jnp.zeros_like(m_ref)

    row = lax.broadcasted_iota(jnp.int32, (L, L), 0)
    col = lax.broadcasted_iota(jnp.int32, (L, L), 1)
    causal = col <= row
    neg_inf = jnp.float32(-jnp.inf)
    k_scale = dk ** -0.5
    gts = [gt_ref[sq] + bias_ref[...] for sq in range(gt_ref.shape[0])]
    gTs = [gt.T for gt in gts]
    for sq, h in [(sq, h) for sq in range(len(gts)) for h in range(MH)]:
        gt, gT = gts[sq], gTs[sq]
        c0, c1 = h * dk, (h + 1) * dk
        ig_col = gt[:, h:h + 1]
        lf_col = _log_sigmoid(gt[:, MH + h:MH + h + 1])
        ig_row = gT[h:h + 1, :]
        lf_row = _log_sigmoid(gT[MH + h:MH + h + 1, :])
        b_col = jnp.sum(jnp.where(causal, lf_row, 0.0), axis=1, keepdims=True)
        b_row = jnp.sum(jnp.where(row <= col, lf_col, 0.0), axis=0, keepdims=True)
        a_row = ig_row - b_row
        a_col = ig_col - b_col
        cmax = jnp.max(jnp.where(causal, a_row, neg_inf), axis=1, keepdims=True)
        m_prev = m_ref[sq, h, :, 0:1]
        m_t = b_col + jnp.maximum(m_prev, cmax)
        w_inter = jnp.exp(b_col + m_prev - m_t)
        dmat = jnp.exp(jnp.where(causal, a_row + (b_col - m_t), neg_inf))
        q = q_ref[sq, :, c0:c1]
        ks = k_ref[sq, :, c0:c1] * k_scale
        qb = q.astype(BF16)
        kb = ks.astype(BF16)
        vb = v_ref[sq, :, c0:c1].astype(BF16)
        s = lax.dot_general(qb, kb, (((1,), (1,)), ((), ())), preferred_element_type=F32) * dmat
        C_prev = C_ref[sq, h]
        n_prev = n_ref[sq, h]
        num = (w_inter * jnp.dot(qb, C_prev.astype(BF16), preferred_element_type=F32)
               + jnp.dot(s.astype(BF16), vb, preferred_element_type=F32))
        den = (w_inter * jnp.sum(q * n_prev, axis=1, keepdims=True)
               + jnp.sum(s, axis=1, keepdims=True))
        hh = num * (1.0 / jnp.maximum(jnp.abs(den), jnp.exp(-m_t)))
        hn_ref[sq, :, c0:c1] = _head_norm_gate(hh, o_ref[sq, :, c0:c1], hg_ref[:, c0:c1]).astype(hn_ref.dtype)

        m_new = m_t[L - 1:L, :]
        b_last = b_col[L - 1:L, :]
        w_old = jnp.exp(b_last + m_prev - m_new)
        kw = ks * jnp.exp(a_col + b_last - m_new)
        C_ref[sq, h] = w_old * C_prev + lax.dot_general(
            kw.astype(BF16), vb, (((0,), (0,)), ((), ())), preferred_element_type=F32)
        n_ref[sq, h] = w_old * n_prev + jnp.sum(kw, axis=0, keepdims=True)
        m_ref[sq, h] = jnp.broadcast_to(m_new, (1, LANES))


def _mlstm_prompt(z, gates, bias, hg, *, B, S, d_pool, d_ml, L):
    nseq = math.gcd(B, MLSTM_SEQS_PER_STEP)
    z = z.reshape(B, S, z.shape[1])
    gates = gates.reshape(B, S, LANES)
    nc = S // L
    dk = d_ml // MH
    blk0 = d_pool // d_ml
    assert blk0 * d_ml == d_pool

    def zspec(part):
        return pl.BlockSpec((nseq, L, d_ml), lambda b, c: (b, c, blk0 + part))

    hn, C, n, m = pl.pallas_call(
        functools.partial(_mlstm_prompt_body, L=L, dk=dk),
        grid=(B // nseq, nc),
        in_specs=[zspec(0), zspec(1), zspec(2), zspec(3),
                  pl.BlockSpec((nseq, L, LANES), lambda b, c: (b, c, 0)),
                  pl.BlockSpec((1, LANES), lambda b, c: (0, 0)),
                  pl.BlockSpec((1, d_ml), lambda b, c: (0, 0))],
        out_specs=[pl.BlockSpec((nseq, L, d_ml), lambda b, c: (b, c, 0)),
                   pl.BlockSpec((nseq, MH, dk, dk), lambda b, c: (b, 0, 0, 0)),
                   pl.BlockSpec((nseq, MH, 1, dk), lambda b, c: (b, 0, 0, 0)),
                   pl.BlockSpec((nseq, MH, 1, LANES), lambda b, c: (b, 0, 0, 0))],
        out_shape=[jax.ShapeDtypeStruct((B, S, d_ml), BF16),
                   jax.ShapeDtypeStruct((B, MH, dk, dk), F32),
                   jax.ShapeDtypeStruct((B, MH, 1, dk), F32),
                   jax.ShapeDtypeStruct((B, MH, 1, LANES), F32)],
        compiler_params=_params("parallel", "arbitrary"), name="mlstm_prompt")(
            z, z, z, z, gates, bias, hg)
    return hn.reshape(B * S, d_ml), C, n, m


def _row_to_col(x_row, n):
    r = lax.broadcasted_iota(jnp.int32, (n, n), 0)
    c = lax.broadcasted_iota(jnp.int32, (n, n), 1)
    return jnp.sum(jnp.where(r == c, x_row, 0.0), axis=1, keepdims=True)


def _mlstm_step_body(z_ref, gt_ref, bias_ref, hg_ref, C0_ref, n0_ref, m0_ref,
                     hn_ref, C_ref, n_ref, m_ref, *, d_pool, d_ml, dk):
    gt = gt_ref[0] + bias_ref[...]
    k_scale = dk ** -0.5
    for h in range(MH):
        base = d_pool + h * dk
        q = z_ref[0, :, base:base + dk]
        ks = z_ref[0, :, base + d_ml:base + d_ml + dk] * k_scale
        v = z_ref[0, :, base + 2 * d_ml:base + 2 * d_ml + dk]
        o = z_ref[0, :, base + 3 * d_ml:base + 3 * d_ml + dk]
        ig = gt[:, h:h + 1]
        lf = _log_sigmoid(gt[:, MH + h:MH + h + 1])
        m_prev = m0_ref[0, h]
        m_new = jnp.maximum(m_prev + lf, ig)
        w_old = jnp.exp(lf + m_prev - m_new)
        w_s = jnp.exp(ig - m_new)
        C_new = w_old * C0_ref[0, h] + _row_to_col(ks * w_s, dk) * v
        n_new = w_old * n0_ref[0, h] + w_s * ks
        num = jnp.sum(_row_to_col(q, dk) * C_new, axis=0, keepdims=True)
        den = jnp.sum(q * n_new, axis=1, keepdims=True)
        hh = num / jnp.maximum(jnp.abs(den), jnp.exp(-m_new))
        c0 = h * dk
        hn_ref[0, :, c0:c0 + dk] = _head_norm_gate(hh, o, hg_ref[:, c0:c0 + dk]).astype(hn_ref.dtype)
        C_ref[0, h] = C_new
        n_ref[0, h] = n_new
        m_ref[0, h] = m_new


def _mlstm_step(z3, g3, bias, hg, C0, n0, m0, *, d_pool, d_ml):
    Bs = C0.shape[0]
    dk = d_ml // MH
    dz = z3.shape[-1]
    st = lambda *shape: pl.BlockSpec((1,) + shape, lambda b: (b,) + (0,) * len(shape))
    return pl.pallas_call(
        functools.partial(_mlstm_step_body, d_pool=d_pool, d_ml=d_ml, dk=dk),
        grid=(Bs,),
        in_specs=[st(1, dz), st(1, LANES),
                  pl.BlockSpec((1, LANES), lambda b: (0, 0)),
                  pl.BlockSpec((1, d_ml), lambda b: (0, 0)),
                  st(MH, dk, dk), st(MH, 1, dk), st(MH, 1, 1)],
        out_specs=[st(1, d_ml), st(MH, dk, dk), st(MH, 1, dk), st(MH, 1, 1)],
        out_shape=[jax.ShapeDtypeStruct((Bs, 1, d_ml), BF16),
                   jax.ShapeDtypeStruct((Bs, MH, dk, dk), F32),
                   jax.ShapeDtypeStruct((Bs, MH, 1, dk), F32),
                   jax.ShapeDtypeStruct((Bs, MH, 1, 1), F32)],
        compiler_params=_params("parallel"), name="mlstm_step")(z3, g3, bias, hg, C0, n0, m0)


KM_ROWS = 16


F32_BIG = 1.0e30


def _moba_seq_body(q_ref, k_ref, v_ref, o_ref, kb_ref, vT_ref, km_ref, qT_ref, pT_ref,
                   m_ref, l_ref, acc_ref, *, nblk, hd):
    blk = MOBA_BLOCK
    R = GROUP * blk
    nchunk = R // LANES
    c_exp = hd ** -0.5 * LOG2E
    neg_inf = jnp.float32(-jnp.inf)

    kb_ref[...] = k_ref[...].astype(BF16)
    vT_ref[...] = v_ref[...].T.astype(BF16)
    km = [jnp.sum(k_ref[n * blk:(n + 1) * blk, :], axis=0, keepdims=True) * (1.0 / blk) for n in range(nblk)]
    km.append(jnp.zeros((KM_ROWS - nblk, hd), F32))
    km_ref[...] = jnp.concatenate(km, axis=0).astype(BF16)
    sub = lax.broadcasted_iota(jnp.int32, (nblk, R), 0)

    def own_block(j, buf, p_rows_ref):
        ms, ls = [], []
        for c in range(nchunk):
            lanes = slice(c * LANES, (c + 1) * LANES)
            q0 = (c * LANES) % blk
            nk = q0 + LANES
            s = jnp.dot(kb_ref[j * blk:j * blk + nk, :], qT_ref[buf, :, lanes], preferred_element_type=F32)
            key = lax.broadcasted_iota(jnp.int32, (nk, LANES), 0)
            qry = lax.broadcasted_iota(jnp.int32, (nk, LANES), 1) + q0
            s = jnp.where(key <= qry, s, neg_inf)
            m = jnp.max(s, axis=0, keepdims=True)
            p = jnp.exp2((s - m) * c_exp)
            ms.append(m)
            ls.append(jnp.sum(p, axis=0, keepdims=True))
            p_rows_ref[0:nk, lanes] = p.astype(BF16)
            if nk < blk:
                p_rows_ref[nk:blk, lanes] = jnp.zeros((blk - nk, LANES), BF16)
        return ms, ls

    def write_out(j, acc, l_all):
        inv_l = 1.0 / l_all
        for g in range(GROUP):
            cols = slice(g * blk, (g + 1) * blk)
            o_ref[j * blk:(j + 1) * blk, g * hd:(g + 1) * hd] = (acc[:, cols] * inv_l[:, cols]).T.astype(o_ref.dtype)

    for j in range(nblk):
        buf = j % 2
        for g in range(GROUP):
            qT_ref[buf, :, g * blk:(g + 1) * blk] = (
                q_ref[j * blk:(j + 1) * blk, g * hd:(g + 1) * hd].astype(F32).T.astype(BF16))

        sel = None
        if j > MOBA_TOPK:
            gate = jnp.dot(km_ref[...], qT_ref[buf], preferred_element_type=F32)[0:nblk, :]
            rows = []
            for n in range(j):
                gn = gate[n:n + 1, :]
                beats = ((gate > gn) | ((gate == gn) & (sub < n))) & (sub < j)
                rows.append((jnp.sum(beats.astype(F32), axis=0, keepdims=True) < MOBA_TOPK).astype(F32))
            sel = rows

        ms, ls = own_block(j, buf, pT_ref.at[buf, j * blk:(j + 1) * blk])
        for n in range(j):
            kb = kb_ref[n * blk:(n + 1) * blk, :]
            for g in range(GROUP):
                s2 = jnp.dot(kb, qT_ref[buf, :, g * blk:(g + 1) * blk], preferred_element_type=F32)
                for half in range(blk // LANES):
                    c = g * (blk // LANES) + half
                    lanes = slice(c * LANES, (c + 1) * LANES)
                    s = s2[:, half * LANES:(half + 1) * LANES]
                    if sel is not None:
                        s = jnp.where(sel[n][:, lanes] > 0.5, s, neg_inf)
                    p = jnp.exp2((s - ms[c]) * c_exp)
                    ls[c] = ls[c] + jnp.sum(p, axis=0, keepdims=True)
                    pT_ref[buf, n * blk:(n + 1) * blk, lanes] = p.astype(BF16)
        nkeys = (j + 1) * blk
        acc = jnp.dot(vT_ref[:, 0:nkeys], pT_ref[buf, 0:nkeys, :], preferred_element_type=F32)
        l_all = jnp.concatenate(ls, axis=1)
        write_out(j, acc, l_all)

        if j > 0:
            @pl.when(jnp.logical_not(jnp.max(l_all) < F32_BIG))
            def _(j=j, buf=buf, sel=sel):
                ms, ls = own_block(j, buf, pT_ref.at[buf, 0:blk])
                m_ref[...] = jnp.concatenate(ms, axis=1)
                l_ref[...] = jnp.concatenate(ls, axis=1)
                acc_ref[...] = jnp.dot(vT_ref[:, j * blk:(j + 1) * blk], pT_ref[buf, 0:blk, :],
                                       preferred_element_type=F32)
                selmat = None if sel is None else jnp.concatenate(
                    sel + [jnp.zeros((nblk - j, R), F32)], axis=0)

                def body(n, carry):
                    start = pl.multiple_of(n * blk, blk)
                    kb = kb_ref[pl.ds(start, blk), :]
                    vtb = vT_ref[:, pl.ds(start, blk)]
                    if selmat is not None:
                        sel_n = jnp.sum(jnp.where(sub == n, selmat, 0.0), axis=0, keepdims=True)
                    for c in range(nchunk):
                        lanes = slice(c * LANES, (c + 1) * LANES)
                        s = jnp.dot(kb, qT_ref[buf, :, lanes], preferred_element_type=F32)
                        if selmat is not None:
                            s = jnp.where(sel_n[:, lanes] > 0.5, s, neg_inf)
                        m_old = m_ref[:, lanes]
                        m_new = jnp.maximum(m_old, jnp.max(s, axis=0, keepdims=True))
                        alpha = jnp.exp2((m_old - m_new) * c_exp)
                        p = jnp.exp2((s - m_new) * c_exp)
                        m_ref[:, lanes] = m_new
                        l_ref[:, lanes] = alpha * l_ref[:, lanes] + jnp.sum(p, axis=0, keepdims=True)
                        acc_ref[:, lanes] = acc_ref[:, lanes] * alpha + jnp.dot(
                            vtb, p.astype(BF16), preferred_element_type=F32)
                    return carry

                lax.fori_loop(0, j, body, 0)
                write_out(j, acc_ref[...], l_ref[...])


def _moba_seq(q, k, v, *, B, S, hd):
    assert S % MOBA_BLOCK == 0 and S // MOBA_BLOCK <= KM_ROWS
    nq = S // MOBA_BLOCK
    gw = GROUP * hd
    R = GROUP * MOBA_BLOCK
    return pl.pallas_call(
        functools.partial(_moba_seq_body, nblk=nq, hd=hd),
        grid=(B, N_KV_HEADS),
        in_specs=[pl.BlockSpec((S, gw), lambda b, h: (b, h)),
                  pl.BlockSpec((S, hd), lambda b, h: (b, h)),
                  pl.BlockSpec((S, hd), lambda b, h: (b, h))],
        out_specs=pl.BlockSpec((S, gw), lambda b, h: (b, h)),
        out_shape=jax.ShapeDtypeStruct((B * S, N_HEADS * hd), BF16),
        scratch_shapes=[pltpu.VMEM((S, hd), BF16),
                        pltpu.VMEM((hd, S), BF16),
                        pltpu.VMEM((KM_ROWS, hd), BF16),
                        pltpu.VMEM((2, hd, R), BF16),
                        pltpu.VMEM((2, S, R), BF16),
                        pltpu.VMEM((1, R), F32),
                        pltpu.VMEM((1, R), F32),
                        pltpu.VMEM((hd, R), F32)],
        compiler_params=_params("parallel", "parallel"), name="moba_seq")(q, k, v)


PAGES_PER_STEP = 16
PAGES_PER_BLOCK = MOBA_BLOCK // PAGE_SIZE
TOKENS_PER_TILE = SUBLANES // N_KV_HEADS
KV_SHIFT = N_KV_HEADS.bit_length() - 1
GROUP_SHIFT = GROUP.bit_length() - 1
assert 1 << KV_SHIFT == N_KV_HEADS and 1 << GROUP_SHIFT == GROUP and TOKENS_PER_TILE * N_KV_HEADS == SUBLANES


def _block_mean_keys(page_refs, o_ref):
    rows, hd = page_refs[0].shape[2:]
    for blk in range(PAGES_PER_STEP // PAGES_PER_BLOCK):
        acc = None
        for r in range(PAGES_PER_BLOCK):
            pg = page_refs[blk * PAGES_PER_BLOCK + r][0, 0]
            part = jnp.sum(pg.reshape(rows // SUBLANES, SUBLANES, hd), axis=0)
            acc = part if acc is None else acc + part
        per_head = acc[0:N_KV_HEADS]
        for t in range(1, TOKENS_PER_TILE):
            per_head = per_head + acc[t * N_KV_HEADS:(t + 1) * N_KV_HEADS]
        o_ref[0, 0, blk * N_KV_HEADS:(blk + 1) * N_KV_HEADS, :] = per_head * (1.0 / MOBA_BLOCK)


def _topk_body(q_ref, km_ref, idx_ref, *, ncol):
    qb = q_ref[0].astype(BF16)
    km = km_ref[0, 0].astype(BF16)
    gate = lax.dot_general(qb, km, (((1,), (1,)), ((), ())), preferred_element_type=F32)
    head = lax.broadcasted_iota(jnp.int32, (N_HEADS, ncol), 0)
    col = lax.broadcasted_iota(jnp.int32, (N_HEADS, ncol), 1)
    gate = jnp.where((col & (N_KV_HEADS - 1)) == (head >> GROUP_SHIFT), gate, -jnp.inf)
    out_lane = lax.broadcasted_iota(jnp.int32, (N_HEADS, LANES), 1)
    out = jnp.zeros((N_HEADS, LANES), jnp.int32)
    for r in range(MOBA_TOPK):
        mx = jnp.max(gate, axis=1, keepdims=True)
        idx = jnp.min(jnp.where(gate == mx, col, ncol), axis=1, keepdims=True)
        out = jnp.where(out_lane == r, idx >> KV_SHIFT, out)
        gate = jnp.where(col == idx, -jnp.inf, gate)
    idx_ref[0] = out


def _topk_blocks(q3, kmean, layer):
    Bs, _, hd = q3.shape
    ncol = kmean.shape[2]
    assert ncol >= MOBA_TOPK * N_KV_HEADS
    return pl.pallas_call(
        functools.partial(_topk_body, ncol=ncol),
        grid=(Bs,),
        in_specs=[pl.BlockSpec((1, N_HEADS, hd), lambda b: (b, 0, 0)),
                  pl.BlockSpec((1, 1, ncol, hd), lambda b: (layer, b, 0, 0))],
        out_specs=pl.BlockSpec((1, N_HEADS, LANES), lambda b: (b, 0, 0)),
        out_shape=jax.ShapeDtypeStruct((Bs, N_HEADS, LANES), jnp.int32),
        compiler_params=_params("parallel"), name="topk_blocks")(q3, kmean)


N_SEL_PAGES = MOBA_TOPK * PAGES_PER_BLOCK


def _moba_step_body(idx_ref, pt_ref, q_ref, kn_ref, vn_ref, *refs, hd):
    n_in = GROUP * N_SEL_PAGES
    k_refs, v_refs, o_ref = refs[:n_in], refs[n_in:2 * n_in], refs[2 * n_in]
    kv_head = pl.program_id(1)
    scale = hd ** -0.5
    rows = SAMPLE_ROWS
    ncol = k_refs[0].shape[2]
    mine = (lax.broadcasted_iota(jnp.int32, (rows, ncol), 1) & (N_KV_HEADS - 1)) == kv_head
    k_new = kn_ref[0, 0].astype(BF16).astype(F32)
    v_new = vn_ref[0, 0].astype(BF16).astype(F32)
    for g in range(GROUP):
        qb = jnp.broadcast_to(q_ref[0, g], (rows, hd)).astype(BF16)
        pages = slice(g * N_SEL_PAGES, (g + 1) * N_SEL_PAGES)
        s_pages = [jnp.where(mine,
                             lax.dot_general(qb, kr[0, 0].astype(BF16), (((1,), (1,)), ((), ())),
                                             preferred_element_type=F32) * scale,
                             -jnp.inf) for kr in k_refs[pages]]
        s_own = jnp.sum(qb.astype(F32) * k_new, axis=1, keepdims=True) * scale
        m = s_own
        for s in s_pages:
            m = jnp.maximum(m, jnp.max(s, axis=1, keepdims=True))
        p_own = jnp.exp(s_own - m)
        l = p_own
        acc = p_own * v_new
        for s, vr in zip(s_pages, v_refs[pages]):
            p = jnp.exp(s - m)
            l = l + jnp.sum(p, axis=1, keepdims=True)
            acc = acc + jnp.dot(p.astype(BF16), vr[0, 0].astype(BF16), preferred_element_type=F32)
        o_ref[0, g] = (acc / l)[0:1, :].astype(o_ref.dtype)


def _moba_step(q4, kn4, vn4, cache_k4, cache_v4, idx_flat, pt_flat, *, layer, n_pages):
    Bs, _, _, hd = q4.shape
    rows = cache_k4.shape[2]

    def page_spec(r):
        g, r = divmod(r, N_SEL_PAGES)
        sel, half = divmod(r, PAGES_PER_BLOCK)

        def index_map(b, h, idx, pt):
            blk = idx[(b * N_HEADS + h * GROUP + g) * MOBA_TOPK + sel]
            return (layer, pt[b * n_pages + blk * PAGES_PER_BLOCK + half], 0, 0)

        return pl.BlockSpec((1, 1, rows, hd), index_map)

    group_spec = pl.BlockSpec((1, GROUP, 1, hd), lambda b, h, idx, pt: (b, h, 0, 0))
    kv_spec = pl.BlockSpec((1, 1, 1, hd), lambda b, h, idx, pt: (b, h, 0, 0))
    n_in = GROUP * N_SEL_PAGES
    return pl.pallas_call(
        functools.partial(_moba_step_body, hd=hd),
        grid_spec=pltpu.PrefetchScalarGridSpec(
            num_scalar_prefetch=2, grid=(Bs, N_KV_HEADS),
            in_specs=[group_spec, kv_spec, kv_spec] + [page_spec(r) for r in range(n_in)] * 2,
            out_specs=group_spec),
        out_shape=jax.ShapeDtypeStruct((Bs, N_HEADS, 1, hd), BF16),
        compiler_params=_params("parallel", "arbitrary"), name="moba_step")(
            idx_flat, pt_flat, q4, kn4, vn4, *([cache_k4] * n_in), *([cache_v4] * n_in))


def _pad_rows(a, rows):
    return jnp.pad(a, ((0, rows - a.shape[0]),) + ((0, 0),) * (a.ndim - 1))


def kernel(x_prompt, x_sample, cache_k, cache_v, state_pool, state_C, state_n, state_m, page_table,
           w_in_mix, b_if, w_pool, pool_scale, mlstm_norm_g, w_out_mix, w_qkv, w_o,
           ln_mix_g, ln_mix_b, ln_ffn_g, ln_ffn_b, w_up, w_down):
    B, S, D = x_prompt.shape
    Bs = x_sample.shape[0]
    assert x_sample.shape[1] == 1 and Bs <= SAMPLE_ROWS
    depth = w_up.shape[0]
    d_pool = state_pool.shape[-1]
    dk = state_C.shape[-1]
    d_ml = MH * dk
    d_main = d_pool + 4 * d_ml
    assert w_in_mix.shape[2] == d_main + 2 * MH
    hd = cache_k.shape[-1]
    d_q, d_kv = N_HEADS * hd, N_KV_HEADS * hd
    n_pages = page_table.shape[1]
    past_len = n_pages * PAGE_SIZE
    assert cache_k.shape[2] == PAGE_SIZE and past_len % MOBA_BLOCK == 0
    alpha = (2 * depth) ** 0.25
    M = B * S
    L = math.gcd(S, MLSTM_CHUNK)

    w_in_t = jnp.swapaxes(w_in_mix, 1, 2)
    bias_g = jnp.pad(b_if, ((0, 0), (0, LANES - 2 * MH)))[:, None, :]
    w_pool_b = w_pool.astype(BF16)

    cache_k4 = cache_k.reshape(cache_k.shape[:2] + (PAGE_SIZE * N_KV_HEADS, hd))
    cache_v4 = cache_v.reshape(cache_v.shape[:2] + (PAGE_SIZE * N_KV_HEADS, hd))
    pt_flat = page_table.reshape(-1)
    kmean_past = None

    xs = _pad_rows(x_sample.reshape(Bs, D), SAMPLE_ROWS)
    xp = x_prompt.reshape(M, D)
    ks_, vs_, pool_s, C_s, n_s, m_s = [], [], [], [], [], []
    pool_p, C_p, n_p, m_p = [], [], [], []
    n_attn = cache_k.shape[0]
    kv_flat = None
    for l in range(depth):
        i = l // 2
        ln_g, ln_b = ln_mix_g[l][None, :], ln_mix_b[l][None, :]
        fg, fb = ln_ffn_g[l][None, :], ln_ffn_b[l][None, :]
        if l % 2 == 0:
            sc = pool_scale[i][None, :]
            hg = mlstm_norm_g[i][None, :]
            zs, gs, wb, wgb = _proj_gates_step(xs, w_in_t, i, n_main=d_main, n_gate=2 * MH, tn=STEP_TN)
            us = zs[:Bs, :d_pool]
            ys_pool = _pool_step(jnp.transpose(state_pool[i], (1, 0, 2)), us, w_pool_b[i], sc, pos0=past_len)
            hns, C, n, m = _mlstm_step(
                zs[:Bs, None, :], gs[:Bs, None, :], bias_g[i], hg, state_C[i],
                state_n[i][:, :, None, :], state_m[i][:, :, None, None], d_pool=d_pool, d_ml=d_ml)
            a = _pad_rows(jnp.concatenate([ys_pool, hns[:, 0, :]], axis=1), SAMPLE_ROWS)
            xs, wpb = _proj_ln_step(a, w_out_mix, i, xs, ln_g, ln_b, alpha=alpha, tk=STEP_TK)
            pool_s.append(jnp.concatenate([state_pool[i][:, 1:, :], us[:, None, :]], axis=1))
            C_s.append(C)
            n_s.append(n[:, :, 0, :])
            m_s.append(m[:, :, 0, 0])
        else:
            qs, k_new, v_new, wb = _proj_qkv_step(xs, w_qkv, i, d_q=d_q, d_kv=d_kv)
            q_s = qs[:Bs].astype(F32).reshape(Bs, N_HEADS, hd)
            k_new = k_new[:Bs].reshape(Bs, N_KV_HEADS, 1, hd)
            v_new = v_new[:Bs].reshape(Bs, N_KV_HEADS, 1, hd)
            idx = _topk_blocks(q_s, kmean_past, i)[:, :, :MOBA_TOPK].reshape(-1)
            os_ = _moba_step(q_s[:, :, None, :], k_new, v_new, cache_k4, cache_v4, idx, pt_flat,
                             layer=i, n_pages=n_pages)
            xs, wpb = _proj_ln_step(_pad_rows(os_.reshape(Bs, d_q), SAMPLE_ROWS), w_o, i, xs, ln_g, ln_b,
                                    alpha=alpha, tk=STEP_TK)
            ks_.append(k_new.reshape(Bs, 1, N_KV_HEADS, hd))
            vs_.append(v_new.reshape(Bs, 1, N_KV_HEADS, hd))
        if l == 0:
            xs, wub, wdb = _ffn_step(xs, w_up, w_down, l, fg, fb, alpha=alpha, tf=STEP_TF)
        else:
            xs, = _ffn(xs, wub[None], wdb[None], 0, fg, fb, alpha=alpha, tm=SAMPLE_ROWS, tf=FFN_TF)

        if l % 2 == 0:
            z, gates = _proj_gates(xp, wb, wgb, tm=PROJ_TM, tn=PROJ_TN)
            y_pool = _pool_prompt(z, w_pool_b, i, sc, B=B, S=S, d_pool=d_pool, T=POOL_T)
            hn, C, n, m = _mlstm_prompt(z, gates, bias_g[i], hg, B=B, S=S, d_pool=d_pool, d_ml=d_ml, L=L)
            xp = _proj_ln([y_pool, hn], wpb[None], 0, xp, ln_g, ln_b, alpha=alpha, tm=LN_TM)
            pool_p.append(z.reshape(B, S, d_main)[:, S - POOL_BUF:, :d_pool])
            C_p.append(C)
            n_p.append(n.reshape(B, MH, dk))
            m_p.append(m[:, :, 0, 0])
        else:
            q, k, v, *kv_flat = _proj_qkv(xp, wb[None], 0, d_q=d_q, d_kv=d_kv, hd=hd, tm=PROJ_TM,
                                          slab=i, n_slabs=n_attn, prev=kv_flat)
            o = _moba_seq(q, k, v, B=B, S=S, hd=hd)
            xp = _proj_ln([o], wpb[None], 0, xp, ln_g, ln_b, alpha=alpha, tm=LN_TM)
        res = _ffn(xp, wub[None], wdb[None], 0, fg, fb, alpha=alpha, tm=FFN_TM, tf=FFN_TF,
                   paged=(cache_k4, pt_flat, Bs, n_pages) if kmean_past is None else None,
                   cast_next=(w_up, w_down, l + 1) if l + 1 < depth else None)
        xp = res[0]
        if kmean_past is None:
            kmean_past = res[1]
        if l + 1 < depth:
            wub, wdb = res[-2:]

    k_prompt, v_prompt = (t.reshape(n_attn, B, S, N_KV_HEADS, hd) for t in kv_flat)
    return (xp.reshape(B, S, D), xs[:Bs].reshape(Bs, 1, D),
            k_prompt, v_prompt, jnp.stack(ks_), jnp.stack(vs_),
            jnp.stack(pool_p), jnp.stack(C_p), jnp.stack(n_p), jnp.stack(m_p),
            jnp.stack(pool_s), jnp.stack(C_s), jnp.stack(n_s), jnp.stack(m_s))
```

```python
import functools
import math

import jax
import jax.numpy as jnp
from jax import lax
from jax.experimental import pallas as pl
from jax.experimental.pallas import tpu as pltpu

F32 = jnp.float32
BF16 = jnp.bfloat16

POOL_WINDOWS = (2, 4, 8, 16)
POOL_BUF = max(POOL_WINDOWS) - 1
POOL_HALO = 32
MH = 4
N_HEADS = 16
N_KV_HEADS = 4
GROUP = N_HEADS // N_KV_HEADS
MOBA_BLOCK = 256
MOBA_TOPK = 3
PAGE_SIZE = 128
LN_EPS = 1e-5
HEAD_NORM_EPS = 1e-6
LANES = 128
SUBLANES = 8
SAMPLE_ROWS = 16
LOG2E = 1.4426950408889634

VMEM_LIMIT = 56 * 1024 * 1024

PROJ_TM, PROJ_TN = 1024, 1024
LN_TM = 512
LN_SUB_ROWS = 256
FFN_TM, FFN_TF = 512, 1024
STEP_TN, STEP_TK, STEP_TF = 1024, 512, 512
POOL_T = 256
MLSTM_CHUNK = 512
MLSTM_SEQS_PER_STEP = 1


def _params(*sem):
    return pltpu.CompilerParams(dimension_semantics=sem, vmem_limit_bytes=VMEM_LIMIT)


def _layer_norm(y, g, b):
    mu = jnp.mean(y, axis=-1, keepdims=True)
    yc = y - mu
    var = jnp.mean(yc * yc, axis=-1, keepdims=True)
    return yc * lax.rsqrt(var + LN_EPS) * g + b


def _log_sigmoid(x):
    return jnp.minimum(x, 0.0) - jnp.log1p(jnp.exp(-jnp.abs(x)))


_NT = (((1,), (1,)), ((), ()))


def _proj_gates_body(x_ref, w_ref, wg_ref, o_ref, g_ref, xb_ref):
    @pl.when(pl.program_id(1) == 0)
    def _():
        xb = x_ref[...].astype(BF16)
        xb_ref[...] = xb
        g_ref[...] = lax.dot_general(xb, wg_ref[...], _NT, preferred_element_type=F32)

    o_ref[...] = lax.dot_general(xb_ref[...], w_ref[...], _NT, preferred_element_type=F32)


def _proj_gates(x, wt, wgt, *, tm, tn):
    M, K = x.shape
    n_main = wt.shape[0]
    return pl.pallas_call(
        _proj_gates_body, grid=(M // tm, n_main // tn),
        in_specs=[pl.BlockSpec((tm, K), lambda i, j: (i, 0)),
                  pl.BlockSpec((tn, K), lambda i, j: (j, 0)),
                  pl.BlockSpec((LANES, K), lambda i, j: (0, 0))],
        out_specs=[pl.BlockSpec((tm, tn), lambda i, j: (i, j)),
                   pl.BlockSpec((tm, LANES), lambda i, j: (i, 0))],
        out_shape=[jax.ShapeDtypeStruct((M, n_main), F32), jax.ShapeDtypeStruct((M, LANES), F32)],
        scratch_shapes=[pltpu.VMEM((tm, K), BF16)],
        compiler_params=_params("parallel", "arbitrary"), name="proj_gates")(x, wt, wgt)


def _proj_qkv_body(x_ref, w_ref, *rest, nq_tiles, n_alias):
    q_ref, k_ref, v_ref, kf_ref, vf_ref, xb_ref = rest[n_alias:]
    j = pl.program_id(1)

    @pl.when(j == 0)
    def _():
        xb_ref[...] = x_ref[...].astype(BF16)

    y = jnp.dot(xb_ref[...], w_ref[...], preferred_element_type=F32)

    @pl.when(j < nq_tiles)
    def _():
        q_ref[...] = y.astype(q_ref.dtype)

    @pl.when(j == nq_tiles)
    def _():
        tm, d_kv = k_ref.shape
        hd = kf_ref.shape[1]
        nh = d_kv // hd
        k_ref[...] = y[:, :d_kv]
        v_ref[...] = y[:, d_kv:]
        for h in range(nh):
            kf_ref[pl.ds(h, tm, stride=nh), :] = y[:, h * hd:(h + 1) * hd]
            vf_ref[pl.ds(h, tm, stride=nh), :] = y[:, d_kv + h * hd:d_kv + (h + 1) * hd]


def _proj_qkv(x, w_all, layer, *, d_q, d_kv, hd, tm, slab, n_slabs, prev=None):
    M, K = x.shape
    tn = 2 * d_kv
    nq_tiles = d_q // tn
    nh = d_kv // hd
    assert nq_tiles * tn == d_q
    flat_spec = pl.BlockSpec((None, tm * nh, hd), lambda i, j: (slab, i, 0))
    flat_shape = jax.ShapeDtypeStruct((n_slabs, M * nh, hd), F32)
    in_specs = [pl.BlockSpec((tm, K), lambda i, j: (i, 0)),
                pl.BlockSpec((None, K, tn), lambda i, j: (layer, 0, j))]
    args = [x, w_all]
    aliases = {}
    if prev is not None:
        in_specs += [pl.BlockSpec(memory_space=pl.ANY)] * 2
        args += list(prev)
        aliases = {2: 3, 3: 4}
    return pl.pallas_call(
        functools.partial(_proj_qkv_body, nq_tiles=nq_tiles, n_alias=len(aliases)),
        grid=(M // tm, nq_tiles + 1),
        in_specs=in_specs,
        out_specs=[pl.BlockSpec((tm, tn), lambda i, j: (i, jnp.minimum(j, nq_tiles - 1))),
                   pl.BlockSpec((tm, d_kv), lambda i, j: (i, 0)),
                   pl.BlockSpec((tm, d_kv), lambda i, j: (i, 0)),
                   flat_spec, flat_spec],
        out_shape=[jax.ShapeDtypeStruct((M, d_q), BF16),
                   jax.ShapeDtypeStruct((M, d_kv), F32),
                   jax.ShapeDtypeStruct((M, d_kv), F32),
                   flat_shape, flat_shape],
        scratch_shapes=[pltpu.VMEM((tm, K), BF16)],
        input_output_aliases=aliases,
        compiler_params=_params("parallel", "arbitrary"), name="proj_qkv")(*args)


def _proj_ln_body(*refs, n_a, alpha):
    a_refs = refs[:n_a]
    w_ref, res_ref, g_ref, b_ref, o_ref = refs[n_a:]
    tm = res_ref.shape[0]
    sub = min(tm, LN_SUB_ROWS)
    for r0 in range(0, tm, sub):
        rows = slice(r0, r0 + sub)
        y = alpha * res_ref[rows, :]
        k0 = 0
        for a_ref in a_refs:
            ka = a_ref.shape[1]
            y = y + jnp.dot(a_ref[rows, :], w_ref[k0:k0 + ka, :], preferred_element_type=F32)
            k0 += ka
        o_ref[rows, :] = _layer_norm(y, g_ref[...], b_ref[...])


def _proj_ln(a_list, w_all, layer, res, g, b, *, alpha, tm):
    M, D = res.shape
    K = w_all.shape[1]
    in_specs = [pl.BlockSpec((tm, a.shape[1]), lambda i: (i, 0)) for a in a_list]
    in_specs += [pl.BlockSpec((None, K, D), lambda i: (layer, 0, 0)),
                 pl.BlockSpec((tm, D), lambda i: (i, 0)),
                 pl.BlockSpec((1, D), lambda i: (0, 0)),
                 pl.BlockSpec((1, D), lambda i: (0, 0))]
    return pl.pallas_call(
        functools.partial(_proj_ln_body, n_a=len(a_list), alpha=alpha),
        grid=(M // tm,), in_specs=in_specs,
        out_specs=pl.BlockSpec((tm, D), lambda i: (i, 0)),
        out_shape=jax.ShapeDtypeStruct((M, D), F32),
        compiler_params=_params("parallel"), name="proj_ln")(*a_list, w_all, res, g, b)


def _ffn_body(*refs, alpha, n_page_refs, cast_next):
    if n_page_refs:
        refs = refs[1:]
    x_ref, wu_ref, wd_ref, g_ref, b_ref = refs[:5]
    refs = refs[5:]
    page_refs, refs = refs[:n_page_refs], refs[n_page_refs:]
    next_refs, refs = (refs[:2], refs[2:]) if cast_next else ((), refs)
    o_ref, refs = refs[0], refs[1:]
    km_ref, refs = (refs[0], refs[1:]) if n_page_refs else (None, refs)
    next_out_refs, refs = (refs[:2], refs[2:]) if cast_next else ((), refs)
    xb_ref, acc_ref = refs
    f = pl.program_id(1)
    last = pl.num_programs(1) - 1

    def hidden(rows):
        h = jnp.dot(xb_ref[rows, :], wu_ref[...], preferred_element_type=F32)
        h = jnp.square(jnp.maximum(h, 0.0)).astype(BF16)
        return jnp.dot(h, wd_ref[...], preferred_element_type=F32)

    def page_stream():
        if n_page_refs:
            _block_mean_keys(page_refs, km_ref)
        for src, dst in zip(next_refs, next_out_refs):
            dst[...] = src[...].astype(BF16)

    @pl.when(f == 0)
    def _():
        xb_ref[...] = x_ref[...].astype(BF16)
        acc_ref[...] = jnp.zeros_like(acc_ref)

    @pl.when(f < last)
    def _():
        acc_ref[...] += hidden(slice(None))
        page_stream()

    @pl.when(f == last)
    def _():
        page_stream()
        tm = x_ref.shape[0]
        sub = min(tm, LN_SUB_ROWS)
        for r0 in range(0, tm, sub):
            rows = slice(r0, r0 + sub)
            y = alpha * x_ref[rows, :] + (acc_ref[rows, :] + hidden(rows))
            o_ref[rows, :] = _layer_norm(y, g_ref[...], b_ref[...])


def _ffn(x, wu_all, wd_all, layer, g, b, *, alpha, tm, tf, paged=None, cast_next=None):
    M, D = x.shape
    Fd = wu_all.shape[2]
    grid = (M // tm, Fd // tf)
    n_steps = grid[0] * grid[1]
    in_specs = [pl.BlockSpec((tm, D), lambda i, f, *_: (i, 0)),
                pl.BlockSpec((None, D, tf), lambda i, f, *_: (layer, 0, f)),
                pl.BlockSpec((None, tf, D), lambda i, f, *_: (layer, f, 0)),
                pl.BlockSpec((1, D), lambda i, f, *_: (0, 0)),
                pl.BlockSpec((1, D), lambda i, f, *_: (0, 0))]
    args = [x, wu_all, wd_all, g, b]
    out_specs = [pl.BlockSpec((tm, D), lambda i, f, *_: (i, 0))]
    out_shape = [jax.ShapeDtypeStruct((M, D), F32)]
    scratch = [pltpu.VMEM((tm, D), BF16), pltpu.VMEM((tm, D), F32)]
    n_page_refs = 0
    if paged is not None:
        cache_k4, pt_flat, Bs, n_pages = paged
        n_attn, _, rows, hd = cache_k4.shape
        assert grid == (n_attn * Bs, n_pages // PAGES_PER_STEP)
        out_rows = PAGES_PER_STEP // PAGES_PER_BLOCK * N_KV_HEADS
        n_page_refs = PAGES_PER_STEP

        def page_spec(r):
            return pl.BlockSpec(
                (1, 1, rows, hd),
                lambda i, f, pt: (i // Bs, pt[(i % Bs) * n_pages + f * PAGES_PER_STEP + r], 0, 0))

        in_specs += [page_spec(r) for r in range(PAGES_PER_STEP)]
        args += [cache_k4] * PAGES_PER_STEP
        out_specs.append(pl.BlockSpec((1, 1, out_rows, hd), lambda i, f, pt: (i // Bs, i % Bs, f, 0)))
        out_shape.append(jax.ShapeDtypeStruct((n_attn, Bs, grid[1] * out_rows, hd), F32))
    if cast_next is not None:
        wu_next, wd_next, nxt = cast_next
        for w in (wu_next, wd_next):
            rows_w, cols_w = w.shape[1:]
            rs = rows_w // n_steps
            assert rs * n_steps == rows_w and rs % SAMPLE_ROWS == 0
            in_specs.append(pl.BlockSpec((None, rs, cols_w), lambda i, f, *_: (nxt, i * grid[1] + f, 0)))
            args.append(w)
            out_specs.append(pl.BlockSpec((rs, cols_w), lambda i, f, *_: (i * grid[1] + f, 0)))
            out_shape.append(jax.ShapeDtypeStruct((rows_w, cols_w), BF16))
    body = functools.partial(_ffn_body, alpha=alpha, n_page_refs=n_page_refs, cast_next=cast_next is not None)
    riders = paged is not None or cast_next is not None
    params = _params("arbitrary" if riders else "parallel", "arbitrary")
    if paged is None:
        return pl.pallas_call(body, grid=grid, in_specs=in_specs, out_specs=out_specs, out_shape=out_shape,
                              scratch_shapes=scratch, compiler_params=params, name="ffn")(*args)
    return pl.pallas_call(
        body,
        grid_spec=pltpu.PrefetchScalarGridSpec(num_scalar_prefetch=1, grid=grid, in_specs=in_specs,
                                               out_specs=out_specs, scratch_shapes=scratch),
        out_shape=out_shape, compiler_params=params, name="ffn_kmean")(pt_flat, *args)


def _cast_tile(w_ref, wb_ref):
    wb = w_ref[...].astype(BF16)
    wb_ref[...] = wb
    return wb


def _proj_gates_step_body(x_ref, w_ref, wg_ref, o_ref, g_ref, wb_ref, wgb_ref):
    xb = x_ref[...].astype(BF16)

    @pl.when(pl.program_id(0) == 0)
    def _():
        n_gate, K = wg_ref.shape
        wg = jnp.concatenate([wg_ref[...], jnp.zeros((LANES - n_gate, K), F32)], axis=0).astype(BF16)
        wgb_ref[...] = wg
        g_ref[...] = lax.dot_general(xb, wg, _NT, preferred_element_type=F32)

    o_ref[...] = lax.dot_general(xb, _cast_tile(w_ref, wb_ref), _NT, preferred_element_type=F32)


def _proj_gates_step(x, wt_all, layer, *, n_main, n_gate, tn):
    R, K = x.shape
    assert n_main % n_gate == 0 and n_gate % SUBLANES == 0
    return pl.pallas_call(
        _proj_gates_step_body, grid=(n_main // tn,),
        in_specs=[pl.BlockSpec((R, K), lambda j: (0, 0)),
                  pl.BlockSpec((None, tn, K), lambda j: (layer, j, 0)),
                  pl.BlockSpec((None, n_gate, K), lambda j: (layer, n_main // n_gate, 0))],
        out_specs=[pl.BlockSpec((R, tn), lambda j: (0, j)),
                   pl.BlockSpec((R, LANES), lambda j: (0, 0)),
                   pl.BlockSpec((tn, K), lambda j: (j, 0)),
                   pl.BlockSpec((LANES, K), lambda j: (0, 0))],
        out_shape=[jax.ShapeDtypeStruct((R, n_main), F32), jax.ShapeDtypeStruct((R, LANES), F32),
                   jax.ShapeDtypeStruct((n_main, K), BF16), jax.ShapeDtypeStruct((LANES, K), BF16)],
        compiler_params=_params("arbitrary"), name="proj_gates_step")(x, wt_all, wt_all)


def _proj_qkv_step_body(x_ref, w_ref, q_ref, k_ref, v_ref, wb_ref, *, nq_tiles):
    j = pl.program_id(0)
    y = jnp.dot(x_ref[...].astype(BF16), _cast_tile(w_ref, wb_ref), preferred_element_type=F32)

    @pl.when(j < nq_tiles)
    def _():
        q_ref[...] = y.astype(q_ref.dtype)

    @pl.when(j == nq_tiles)
    def _():
        d_kv = k_ref.shape[1]
        k_ref[...] = y[:, :d_kv]
        v_ref[...] = y[:, d_kv:]


def _proj_qkv_step(x, w_all, layer, *, d_q, d_kv):
    R, K = x.shape
    tn = 2 * d_kv
    nq_tiles = d_q // tn
    assert nq_tiles * tn == d_q
    return pl.pallas_call(
        functools.partial(_proj_qkv_step_body, nq_tiles=nq_tiles),
        grid=(nq_tiles + 1,),
        in_specs=[pl.BlockSpec((R, K), lambda j: (0, 0)),
                  pl.BlockSpec((None, K, tn), lambda j: (layer, 0, j))],
        out_specs=[pl.BlockSpec((R, tn), lambda j: (0, jnp.minimum(j, nq_tiles - 1))),
                   pl.BlockSpec((R, d_kv), lambda j: (0, 0)),
                   pl.BlockSpec((R, d_kv), lambda j: (0, 0)),
                   pl.BlockSpec((K, tn), lambda j: (0, j))],
        out_shape=[jax.ShapeDtypeStruct((R, d_q), BF16),
                   jax.ShapeDtypeStruct((R, d_kv), F32),
                   jax.ShapeDtypeStruct((R, d_kv), F32),
                   jax.ShapeDtypeStruct((K, d_q + 2 * d_kv), BF16)],
        compiler_params=_params("arbitrary"), name="proj_qkv_step")(x, w_all)


def _proj_ln_step_body(a_ref, w_ref, res_ref, g_ref, b_ref, o_ref, wb_ref, acc_ref, *, alpha):
    k = pl.program_id(0)

    @pl.when(k == 0)
    def _():
        acc_ref[...] = alpha * res_ref[...]

    acc_ref[...] += jnp.dot(a_ref[...], _cast_tile(w_ref, wb_ref), preferred_element_type=F32)

    @pl.when(k == pl.num_programs(0) - 1)
    def _():
        o_ref[...] = _layer_norm(acc_ref[...], g_ref[...], b_ref[...])


def _proj_ln_step(a, w_all, layer, res, g, b, *, alpha, tk):
    R, D = res.shape
    K = w_all.shape[1]
    return pl.pallas_call(
        functools.partial(_proj_ln_step_body, alpha=alpha),
        grid=(K // tk,),
        in_specs=[pl.BlockSpec((R, tk), lambda k: (0, k)),
                  pl.BlockSpec((None, tk, D), lambda k: (layer, k, 0)),
                  pl.BlockSpec((R, D), lambda k: (0, 0)),
                  pl.BlockSpec((1, D), lambda k: (0, 0)),
                  pl.BlockSpec((1, D), lambda k: (0, 0))],
        out_specs=[pl.BlockSpec((R, D), lambda k: (0, 0)),
                   pl.BlockSpec((tk, D), lambda k: (k, 0))],
        out_shape=[jax.ShapeDtypeStruct((R, D), F32), jax.ShapeDtypeStruct((K, D), BF16)],
        scratch_shapes=[pltpu.VMEM((R, D), F32)],
        compiler_params=_params("arbitrary"), name="proj_ln_step")(a, w_all, res, g, b)


def _ffn_step_body(x_ref, wu_ref, wd_ref, g_ref, b_ref, o_ref, wub_ref, wdb_ref, acc_ref, *, alpha):
    f = pl.program_id(0)

    @pl.when(f == 0)
    def _():
        acc_ref[...] = jnp.zeros_like(acc_ref)

    h = jnp.dot(x_ref[...].astype(BF16), _cast_tile(wu_ref, wub_ref), preferred_element_type=F32)
    h = jnp.square(jnp.maximum(h, 0.0)).astype(BF16)
    acc_ref[...] += jnp.dot(h, _cast_tile(wd_ref, wdb_ref), preferred_element_type=F32)

    @pl.when(f == pl.num_programs(0) - 1)
    def _():
        y = alpha * x_ref[...] + acc_ref[...]
        o_ref[...] = _layer_norm(y, g_ref[...], b_ref[...])


def _ffn_step(x, wu_all, wd_all, layer, g, b, *, alpha, tf):
    R, D = x.shape
    Fd = wu_all.shape[2]
    return pl.pallas_call(
        functools.partial(_ffn_step_body, alpha=alpha),
        grid=(Fd // tf,),
        in_specs=[pl.BlockSpec((R, D), lambda f: (0, 0)),
                  pl.BlockSpec((None, D, tf), lambda f: (layer, 0, f)),
                  pl.BlockSpec((None, tf, D), lambda f: (layer, f, 0)),
                  pl.BlockSpec((1, D), lambda f: (0, 0)),
                  pl.BlockSpec((1, D), lambda f: (0, 0))],
        out_specs=[pl.BlockSpec((R, D), lambda f: (0, 0)),
                   pl.BlockSpec((D, tf), lambda f: (0, f)),
                   pl.BlockSpec((tf, D), lambda f: (f, 0))],
        out_shape=[jax.ShapeDtypeStruct((R, D), F32),
                   jax.ShapeDtypeStruct((D, Fd), BF16),
                   jax.ShapeDtypeStruct((Fd, D), BF16)],
        scratch_shapes=[pltpu.VMEM((R, D), F32)],
        compiler_params=_params("arbitrary"), name="ffn_step")(x, wu_all, wd_all, g, b)


def _pool_prompt_body(u_ref, wp_ref, sc_ref, y_ref, ext_ref, tmp_ref, *, T, gc):
    t = pl.program_id(1)
    lo = POOL_HALO - max(POOL_WINDOWS)
    n = POOL_HALO + T - lo

    @pl.when(t == 0)
    def _():
        ext_ref[0:POOL_HALO, :] = jnp.zeros((POOL_HALO, ext_ref.shape[1]), F32)
        tmp_ref[:, 0:lo, :] = jnp.zeros((2, lo, gc), F32)

    ext_ref[POOL_HALO:POOL_HALO + T, :] = u_ref[...]
    pos = t * T + lax.broadcasted_iota(jnp.int32, (T, 1), 0)
    for gi, w in enumerate(POOL_WINDOWS):
        c0, c1 = gi * gc, (gi + 1) * gc
        tok = ext_ref[POOL_HALO:POOL_HALO + T, c0:c1]
        read = lambda s, c0=c0, c1=c1: ext_ref[s:s + n, c0:c1]
        h, k = 1, 0
        while h < w:
            stage = read(lo) + read(lo - h)
            h *= 2
            if h < w:
                tmp_ref[k, lo:lo + n, :] = stage
                read = lambda s, k=k: tmp_ref[k, s:s + n, :]
                k = 1 - k
        acc = stage[POOL_HALO - lo:, :]
        cnt = jnp.minimum(w, pos + 1).astype(F32)
        d = acc / cnt - tok
        y = jnp.dot(d.astype(BF16), wp_ref[gi], preferred_element_type=F32) * sc_ref[:, c0:c1]
        y_ref[:, c0:c1] = y.astype(y_ref.dtype)
    ext_ref[0:POOL_HALO, :] = ext_ref[T:T + POOL_HALO, :]


def _pool_prompt(z, wp_all, layer, sc, *, B, S, d_pool, T):
    nt = S // T
    gc = d_pool // len(POOL_WINDOWS)
    return pl.pallas_call(
        functools.partial(_pool_prompt_body, T=T, gc=gc),
        grid=(B, nt),
        in_specs=[pl.BlockSpec((T, d_pool), lambda b, t: (b * nt + t, 0)),
                  pl.BlockSpec((None,) + wp_all.shape[1:], lambda b, t: (layer, 0, 0, 0)),
                  pl.BlockSpec((1, d_pool), lambda b, t: (0, 0))],
        out_specs=pl.BlockSpec((T, d_pool), lambda b, t: (b * nt + t, 0)),
        out_shape=jax.ShapeDtypeStruct((B * S, d_pool), BF16),
        scratch_shapes=[pltpu.VMEM((POOL_HALO + T, d_pool), F32),
                        pltpu.VMEM((2, POOL_HALO + T, gc), F32)],
        compiler_params=_params("parallel", "arbitrary"), name="pool_prompt")(z, wp_all, sc)


def _pool_step_body(buf_ref, u_ref, wp_ref, sc_ref, y_ref, *, gc, pos0):
    u = u_ref[...]
    for gi, w in enumerate(POOL_WINDOWS):
        c0, c1 = gi * gc, (gi + 1) * gc
        tok = u[:, c0:c1]
        acc = tok
        for k in range(1, w):
            acc = acc + buf_ref[POOL_BUF - k, :, c0:c1]
        d = acc / float(min(w, pos0 + 1)) - tok
        y = jnp.dot(d.astype(BF16), wp_ref[gi], preferred_element_type=F32) * sc_ref[:, c0:c1]
        y_ref[:, c0:c1] = y.astype(y_ref.dtype)


def _pool_step(buf_t, u, wp, sc, *, pos0):
    Bs, d_pool = u.shape
    gc = d_pool // len(POOL_WINDOWS)
    return pl.pallas_call(
        functools.partial(_pool_step_body, gc=gc, pos0=pos0),
        out_shape=jax.ShapeDtypeStruct((Bs, d_pool), BF16),
        compiler_params=pltpu.CompilerParams(vmem_limit_bytes=VMEM_LIMIT),
        name="pool_step")(buf_t, u, wp, sc)


def _head_norm_gate(h, o, g):
    mu = jnp.mean(h, axis=-1, keepdims=True)
    hc = h - mu
    var = jnp.mean(hc * hc, axis=-1, keepdims=True)
    return hc * lax.rsqrt(var + HEAD_NORM_EPS) * g * jax.nn.sigmoid(o)


def _mlstm_prompt_body(q_ref, k_ref, v_ref, o_ref, gt_ref, bias_ref, hg_ref,
                       hn_ref, C_ref, n_ref, m_ref, *, L, dk):
    c = pl.program_id(1)

    @pl.when(c == 0)
    def _():
        C_ref[...] = jnp.zeros_like(C_ref)
        n_ref[...] = jnp.zeros_like(n_ref)
        m_ref[...] = jnp.zeros_like(m_ref)

    row = lax.broadcasted_iota(jnp.int32, (L, L), 0)
    col = lax.broadcasted_iota(jnp.int32, (L, L), 1)
    causal = col <= row
    neg_inf = jnp.float32(-jnp.inf)
    k_scale = dk ** -0.5
    gts = [gt_ref[sq] + bias_ref[...] for sq in range(gt_ref.shape[0])]
    gTs = [gt.T for gt in gts]
    for sq, h in [(sq, h) for sq in range(len(gts)) for h in range(MH)]:
        gt, gT = gts[sq], gTs[sq]
        c0, c1 = h * dk, (h + 1) * dk
        ig_col = gt[:, h:h + 1]
        lf_col = _log_sigmoid(gt[:, MH + h:MH + h + 1])
        ig_row = gT[h:h + 1, :]
        lf_row = _log_sigmoid(gT[MH + h:MH + h + 1, :])
        b_col = jnp.sum(jnp.where(causal, lf_row, 0.0), axis=1, keepdims=True)
        b_row = jnp.sum(jnp.where(row <= col, lf_col, 0.0), axis=0, keepdims=True)
        a_row = ig_row - b_row
        a_col = ig_col - b_col
        cmax = jnp.max(jnp.where(causal, a_row, neg_inf), axis=1, keepdims=True)
        m_prev = m_ref[sq, h, :, 0:1]
        m_t = b_col + jnp.maximum(m_prev, cmax)
        w_inter = jnp.exp(b_col + m_prev - m_t)
        dmat = jnp.exp(jnp.where(causal, a_row + (b_col - m_t), neg_inf))
        q = q_ref[sq, :, c0:c1]
        ks = k_ref[sq, :, c0:c1] * k_scale
        qb = q.astype(BF16)
        kb = ks.astype(BF16)
        vb = v_ref[sq, :, c0:c1].astype(BF16)
        s = lax.dot_general(qb, kb, (((1,), (1,)), ((), ())), preferred_element_type=F32) * dmat
        C_prev = C_ref[sq, h]
        n_prev = n_ref[sq, h]
        num = (w_inter * jnp.dot(qb, C_prev.astype(BF16), preferred_element_type=F32)
               + jnp.dot(s.astype(BF16), vb, preferred_element_type=F32))
        den = (w_inter * jnp.sum(q * n_prev, axis=1, keepdims=True)
               + jnp.sum(s, axis=1, keepdims=True))
        hh = num * (1.0 / jnp.maximum(jnp.abs(den), jnp.exp(-m_t)))
        hn_ref[sq, :, c0:c1] = _head_norm_gate(hh, o_ref[sq, :, c0:c1], hg_ref[:, c0:c1]).astype(hn_ref.dtype)

        m_new = m_t[L - 1:L, :]
        b_last = b_col[L - 1:L, :]
        w_old = jnp.exp(b_last + m_prev - m_new)
        kw = ks * jnp.exp(a_col + b_last - m_new)
        C_ref[sq, h] = w_old * C_prev + lax.dot_general(
            kw.astype(BF16), vb, (((0,), (0,)), ((), ())), preferred_element_type=F32)
        n_ref[sq, h] = w_old * n_prev + jnp.sum(kw, axis=0, keepdims=True)
        m_ref[sq, h] = jnp.broadcast_to(m_new, (1, LANES))


def _mlstm_prompt(z, gates, bias, hg, *, B, S, d_pool, d_ml, L):
    nseq = math.gcd(B, MLSTM_SEQS_PER_STEP)
    z = z.reshape(B, S, z.shape[1])
    gates = gates.reshape(B, S, LANES)
    nc = S // L
    dk = d_ml // MH
    blk0 = d_pool // d_ml
    assert blk0 * d_ml == d_pool

    def zspec(part):
        return pl.BlockSpec((nseq, L, d_ml), lambda b, c: (b, c, blk0 + part))

    hn, C, n, m = pl.pallas_call(
        functools.partial(_mlstm_prompt_body, L=L, dk=dk),
        grid=(B // nseq, nc),
        in_specs=[zspec(0), zspec(1), zspec(2), zspec(3),
                  pl.BlockSpec((nseq, L, LANES), lambda b, c: (b, c, 0)),
                  pl.BlockSpec((1, LANES), lambda b, c: (0, 0)),
                  pl.BlockSpec((1, d_ml), lambda b, c: (0, 0))],
        out_specs=[pl.BlockSpec((nseq, L, d_ml), lambda b, c: (b, c, 0)),
                   pl.BlockSpec((nseq, MH, dk, dk), lambda b, c: (b, 0, 0, 0)),
                   pl.BlockSpec((nseq, MH, 1, dk), lambda b, c: (b, 0, 0, 0)),
                   pl.BlockSpec((nseq, MH, 1, LANES), lambda b, c: (b, 0, 0, 0))],
        out_shape=[jax.ShapeDtypeStruct((B, S, d_ml), BF16),
                   jax.ShapeDtypeStruct((B, MH, dk, dk), F32),
                   jax.ShapeDtypeStruct((B, MH, 1, dk), F32),
                   jax.ShapeDtypeStruct((B, MH, 1, LANES), F32)],
        compiler_params=_params("parallel", "arbitrary"), name="mlstm_prompt")(
            z, z, z, z, gates, bias, hg)
    return hn.reshape(B * S, d_ml), C, n, m


def _row_to_col(x_row, n):
    r = lax.broadcasted_iota(jnp.int32, (n, n), 0)
    c = lax.broadcasted_iota(jnp.int32, (n, n), 1)
    return jnp.sum(jnp.where(r == c, x_row, 0.0), axis=1, keepdims=True)


def _mlstm_step_body(z_ref, gt_ref, bias_ref, hg_ref, C0_ref, n0_ref, m0_ref,
                     hn_ref, C_ref, n_ref, m_ref, *, d_pool, d_ml, dk):
    gt = gt_ref[0] + bias_ref[...]
    k_scale = dk ** -0.5
    for h in range(MH):
        base = d_pool + h * dk
        q = z_ref[0, :, base:base + dk]
        ks = z_ref[0, :, base + d_ml:base + d_ml + dk] * k_scale
        v = z_ref[0, :, base + 2 * d_ml:base + 2 * d_ml + dk]
        o = z_ref[0, :, base + 3 * d_ml:base + 3 * d_ml + dk]
        ig = gt[:, h:h + 1]
        lf = _log_sigmoid(gt[:, MH + h:MH + h + 1])
        m_prev = m0_ref[0, h]
        m_new = jnp.maximum(m_prev + lf, ig)
        w_old = jnp.exp(lf + m_prev - m_new)
        w_s = jnp.exp(ig - m_new)
        C_new = w_old * C0_ref[0, h] + _row_to_col(ks * w_s, dk) * v
        n_new = w_old * n0_ref[0, h] + w_s * ks
        num = jnp.sum(_row_to_col(q, dk) * C_new, axis=0, keepdims=True)
        den = jnp.sum(q * n_new, axis=1, keepdims=True)
        hh = num / jnp.maximum(jnp.abs(den), jnp.exp(-m_new))
        c0 = h * dk
        hn_ref[0, :, c0:c0 + dk] = _head_norm_gate(hh, o, hg_ref[:, c0:c0 + dk]).astype(hn_ref.dtype)
        C_ref[0, h] = C_new
        n_ref[0, h] = n_new
        m_ref[0, h] = m_new


def _mlstm_step(z3, g3, bias, hg, C0, n0, m0, *, d_pool, d_ml):
    Bs = C0.shape[0]
    dk = d_ml // MH
    dz = z3.shape[-1]
    st = lambda *shape: pl.BlockSpec((1,) + shape, lambda b: (b,) + (0,) * len(shape))
    return pl.pallas_call(
        functools.partial(_mlstm_step_body, d_pool=d_pool, d_ml=d_ml, dk=dk),
        grid=(Bs,),
        in_specs=[st(1, dz), st(1, LANES),
                  pl.BlockSpec((1, LANES), lambda b: (0, 0)),
                  pl.BlockSpec((1, d_ml), lambda b: (0, 0)),
                  st(MH, dk, dk), st(MH, 1, dk), st(MH, 1, 1)],
        out_specs=[st(1, d_ml), st(MH, dk, dk), st(MH, 1, dk), st(MH, 1, 1)],
        out_shape=[jax.ShapeDtypeStruct((Bs, 1, d_ml), BF16),
                   jax.ShapeDtypeStruct((Bs, MH, dk, dk), F32),
                   jax.ShapeDtypeStruct((Bs, MH, 1, dk), F32),
                   jax.ShapeDtypeStruct((Bs, MH, 1, 1), F32)],
        compiler_params=_params("parallel"), name="mlstm_step")(z3, g3, bias, hg, C0, n0, m0)


KM_ROWS = 16


F32_BIG = 1.0e30


def _moba_seq_body(q_ref, k_ref, v_ref, o_ref, kb_ref, vT_ref, km_ref, qT_ref, pT_ref,
                   m_ref, l_ref, acc_ref, *, nblk, hd):
    blk = MOBA_BLOCK
    R = GROUP * blk
    nchunk = R // LANES
    c_exp = hd ** -0.5 * LOG2E
    neg_inf = jnp.float32(-jnp.inf)

    kb_ref[...] = k_ref[...].astype(BF16)
    vT_ref[...] = v_ref[...].T.astype(BF16)
    km = [jnp.sum(k_ref[n * blk:(n + 1) * blk, :], axis=0, keepdims=True) * (1.0 / blk) for n in range(nblk)]
    km.append(jnp.zeros((KM_ROWS - nblk, hd), F32))
    km_ref[...] = jnp.concatenate(km, axis=0).astype(BF16)
    sub = lax.broadcasted_iota(jnp.int32, (nblk, R), 0)

    def own_block(j, buf, p_rows_ref):
        ms, ls = [], []
        for c in range(nchunk):
            lanes = slice(c * LANES, (c + 1) * LANES)
            q0 = (c * LANES) % blk
            nk = q0 + LANES
            s = jnp.dot(kb_ref[j * blk:j * blk + nk, :], qT_ref[buf, :, lanes], preferred_element_type=F32)
            key = lax.broadcasted_iota(jnp.int32, (nk, LANES), 0)
            qry = lax.broadcasted_iota(jnp.int32, (nk, LANES), 1) + q0
            s = jnp.where(key <= qry, s, neg_inf)
            m = jnp.max(s, axis=0, keepdims=True)
            p = jnp.exp2((s - m) * c_exp)
            ms.append(m)
            ls.append(jnp.sum(p, axis=0, keepdims=True))
            p_rows_ref[0:nk, lanes] = p.astype(BF16)
            if nk < blk:
                p_rows_ref[nk:blk, lanes] = jnp.zeros((blk - nk, LANES), BF16)
        return ms, ls

    def write_out(j, acc, l_all):
        inv_l = 1.0 / l_all
        for g in range(GROUP):
            cols = slice(g * blk, (g + 1) * blk)
            o_ref[j * blk:(j + 1) * blk, g * hd:(g + 1) * hd] = (acc[:, cols] * inv_l[:, cols]).T.astype(o_ref.dtype)

    for j in range(nblk):
        buf = j % 2
        for g in range(GROUP):
            qT_ref[buf, :, g * blk:(g + 1) * blk] = (
                q_ref[j * blk:(j + 1) * blk, g * hd:(g + 1) * hd].astype(F32).T.astype(BF16))

        sel = None
        if j > MOBA_TOPK:
            gate = jnp.dot(km_ref[...], qT_ref[buf], preferred_element_type=F32)[0:nblk, :]
            rows = []
            for n in range(j):
                gn = gate[n:n + 1, :]
                beats = ((gate > gn) | ((gate == gn) & (sub < n))) & (sub < j)
                rows.append((jnp.sum(beats.astype(F32), axis=0, keepdims=True) < MOBA_TOPK).astype(F32))
            sel = rows

        ms, ls = own_block(j, buf, pT_ref.at[buf, j * blk:(j + 1) * blk])
        for n in range(j):
            kb = kb_ref[n * blk:(n + 1) * blk, :]
            for g in range(GROUP):
                s2 = jnp.dot(kb, qT_ref[buf, :, g * blk:(g + 1) * blk], preferred_element_type=F32)
                for half in range(blk // LANES):
                    c = g * (blk // LANES) + half
                    lanes = slice(c * LANES, (c + 1) * LANES)
                    s = s2[:, half * LANES:(half + 1) * LANES]
                    if sel is not None:
                        s = jnp.where(sel[n][:, lanes] > 0.5, s, neg_inf)
                    p = jnp.exp2((s - ms[c]) * c_exp)
                    ls[c] = ls[c] + jnp.sum(p, axis=0, keepdims=True)
                    pT_ref[buf, n * blk:(n + 1) * blk, lanes] = p.astype(BF16)
        nkeys = (j + 1) * blk
        acc = jnp.dot(vT_ref[:, 0:nkeys], pT_ref[buf, 0:nkeys, :], preferred_element_type=F32)
        l_all = jnp.concatenate(ls, axis=1)
        write_out(j, acc, l_all)

        if j > 0:
            @pl.when(jnp.logical_not(jnp.max(l_all) < F32_BIG))
            def _(j=j, buf=buf, sel=sel):
                ms, ls = own_block(j, buf, pT_ref.at[buf, 0:blk])
                m_ref[...] = jnp.concatenate(ms, axis=1)
                l_ref[...] = jnp.concatenate(ls, axis=1)
                acc_ref[...] = jnp.dot(vT_ref[:, j * blk:(j + 1) * blk], pT_ref[buf, 0:blk, :],
                                       preferred_element_type=F32)
                selmat = None if sel is None else jnp.concatenate(
                    sel + [jnp.zeros((nblk - j, R), F32)], axis=0)

                def body(n, carry):
                    start = pl.multiple_of(n * blk, blk)
                    kb = kb_ref[pl.ds(start, blk), :]
                    vtb = vT_ref[:, pl.ds(start, blk)]
                    if selmat is not None:
                        sel_n = jnp.sum(jnp.where(sub == n, selmat, 0.0), axis=0, keepdims=True)
                    for c in range(nchunk):
                        lanes = slice(c * LANES, (c + 1) * LANES)
                        s = jnp.dot(kb, qT_ref[buf, :, lanes], preferred_element_type=F32)
                        if selmat is not None:
                            s = jnp.where(sel_n[:, lanes] > 0.5, s, neg_inf)
                        m_old = m_ref[:, lanes]
                        m_new = jnp.maximum(m_old, jnp.max(s, axis=0, keepdims=True))
                        alpha = jnp.exp2((m_old - m_new) * c_exp)
                        p = jnp.exp2((s - m_new) * c_exp)
                        m_ref[:, lanes] = m_new
                        l_ref[:, lanes] = alpha * l_ref[:, lanes] + jnp.sum(p, axis=0, keepdims=True)
                        acc_ref[:, lanes] = acc_ref[:, lanes] * alpha + jnp.dot(
                            vtb, p.astype(BF16), preferred_element_type=F32)
                    return carry

                lax.fori_loop(0, j, body, 0)
                write_out(j, acc_ref[...], l_ref[...])


def _moba_seq(q, k, v, *, B, S, hd):
    assert S % MOBA_BLOCK == 0 and S // MOBA_BLOCK <= KM_ROWS
    nq = S // MOBA_BLOCK
    gw = GROUP * hd
    R = GROUP * MOBA_BLOCK
    return pl.pallas_call(
        functools.partial(_moba_seq_body, nblk=nq, hd=hd),
        grid=(B, N_KV_HEADS),
        in_specs=[pl.BlockSpec((S, gw), lambda b, h: (b, h)),
                  pl.BlockSpec((S, hd), lambda b, h: (b, h)),
                  pl.BlockSpec((S, hd), lambda b, h: (b, h))],
        out_specs=pl.BlockSpec((S, gw), lambda b, h: (b, h)),
        out_shape=jax.ShapeDtypeStruct((B * S, N_HEADS * hd), BF16),
        scratch_shapes=[pltpu.VMEM((S, hd), BF16),
                        pltpu.VMEM((hd, S), BF16),
                        pltpu.VMEM((KM_ROWS, hd), BF16),
                        pltpu.VMEM((2, hd, R), BF16),
                        pltpu.VMEM((2, S, R), BF16),
                        pltpu.VMEM((1, R), F32),
                        pltpu.VMEM((1, R), F32),
                        pltpu.VMEM((hd, R), F32)],
        compiler_params=_params("parallel", "parallel"), name="moba_seq")(q, k, v)


PAGES_PER_STEP = 16
PAGES_PER_BLOCK = MOBA_BLOCK // PAGE_SIZE
TOKENS_PER_TILE = SUBLANES // N_KV_HEADS
KV_SHIFT = N_KV_HEADS.bit_length() - 1
GROUP_SHIFT = GROUP.bit_length() - 1
assert 1 << KV_SHIFT == N_KV_HEADS and 1 << GROUP_SHIFT == GROUP and TOKENS_PER_TILE * N_KV_HEADS == SUBLANES


def _block_mean_keys(page_refs, o_ref):
    rows, hd = page_refs[0].shape[2:]
    for blk in range(PAGES_PER_STEP // PAGES_PER_BLOCK):
        acc = None
        for r in range(PAGES_PER_BLOCK):
            pg = page_refs[blk * PAGES_PER_BLOCK + r][0, 0]
            part = jnp.sum(pg.reshape(rows // SUBLANES, SUBLANES, hd), axis=0)
            acc = part if acc is None else acc + part
        per_head = acc[0:N_KV_HEADS]
        for t in range(1, TOKENS_PER_TILE):
            per_head = per_head + acc[t * N_KV_HEADS:(t + 1) * N_KV_HEADS]
        o_ref[0, 0, blk * N_KV_HEADS:(blk + 1) * N_KV_HEADS, :] = per_head * (1.0 / MOBA_BLOCK)


def _topk_body(q_ref, km_ref, idx_ref, *, ncol):
    qb = q_ref[0].astype(BF16)
    km = km_ref[0, 0].astype(BF16)
    gate = lax.dot_general(qb, km, (((1,), (1,)), ((), ())), preferred_element_type=F32)
    head = lax.broadcasted_iota(jnp.int32, (N_HEADS, ncol), 0)
    col = lax.broadcasted_iota(jnp.int32, (N_HEADS, ncol), 1)
    gate = jnp.where((col & (N_KV_HEADS - 1)) == (head >> GROUP_SHIFT), gate, -jnp.inf)
    out_lane = lax.broadcasted_iota(jnp.int32, (N_HEADS, LANES), 1)
    out = jnp.zeros((N_HEADS, LANES), jnp.int32)
    for r in range(MOBA_TOPK):
        mx = jnp.max(gate, axis=1, keepdims=True)
        idx = jnp.min(jnp.where(gate == mx, col, ncol), axis=1, keepdims=True)
        out = jnp.where(out_lane == r, idx >> KV_SHIFT, out)
        gate = jnp.where(col == idx, -jnp.inf, gate)
    idx_ref[0] = out


def _topk_blocks(q3, kmean, layer):
    Bs, _, hd = q3.shape
    ncol = kmean.shape[2]
    assert ncol >= MOBA_TOPK * N_KV_HEADS
    return pl.pallas_call(
        functools.partial(_topk_body, ncol=ncol),
        grid=(Bs,),
        in_specs=[pl.BlockSpec((1, N_HEADS, hd), lambda b: (b, 0, 0)),
                  pl.BlockSpec((1, 1, ncol, hd), lambda b: (layer, b, 0, 0))],
        out_specs=pl.BlockSpec((1, N_HEADS, LANES), lambda b: (b, 0, 0)),
        out_shape=jax.ShapeDtypeStruct((Bs, N_HEADS, LANES), jnp.int32),
        compiler_params=_params("parallel"), name="topk_blocks")(q3, kmean)


N_SEL_PAGES = MOBA_TOPK * PAGES_PER_BLOCK


def _moba_step_body(idx_ref, pt_ref, q_ref, kn_ref, vn_ref, *refs, hd):
    n_in = GROUP * N_SEL_PAGES
    k_refs, v_refs, o_ref = refs[:n_in], refs[n_in:2 * n_in], refs[2 * n_in]
    kv_head = pl.program_id(1)
    scale = hd ** -0.5
    rows = SAMPLE_ROWS
    ncol = k_refs[0].shape[2]
    mine = (lax.broadcasted_iota(jnp.int32, (rows, ncol), 1) & (N_KV_HEADS - 1)) == kv_head
    k_new = kn_ref[0, 0].astype(BF16).astype(F32)
    v_new = vn_ref[0, 0].astype(BF16).astype(F32)
    for g in range(GROUP):
        qb = jnp.broadcast_to(q_ref[0, g], (rows, hd)).astype(BF16)
        pages = slice(g * N_SEL_PAGES, (g + 1) * N_SEL_PAGES)
        s_pages = [jnp.where(mine,
                             lax.dot_general(qb, kr[0, 0].astype(BF16), (((1,), (1,)), ((), ())),
                                             preferred_element_type=F32) * scale,
                             -jnp.inf) for kr in k_refs[pages]]
        s_own = jnp.sum(qb.astype(F32) * k_new, axis=1, keepdims=True) * scale
        m = s_own
        for s in s_pages:
            m = jnp.maximum(m, jnp.max(s, axis=1, keepdims=True))
        p_own = jnp.exp(s_own - m)
        l = p_own
        acc = p_own * v_new
        for s, vr in zip(s_pages, v_refs[pages]):
            p = jnp.exp(s - m)
            l = l + jnp.sum(p, axis=1, keepdims=True)
            acc = acc + jnp.dot(p.astype(BF16), vr[0, 0].astype(BF16), preferred_element_type=F32)
        o_ref[0, g] = (acc / l)[0:1, :].astype(o_ref.dtype)


def _moba_step(q4, kn4, vn4, cache_k4, cache_v4, idx_flat, pt_flat, *, layer, n_pages):
    Bs, _, _, hd = q4.shape
    rows = cache_k4.shape[2]

    def page_spec(r):
        g, r = divmod(r, N_SEL_PAGES)
        sel, half = divmod(r, PAGES_PER_BLOCK)

        def index_map(b, h, idx, pt):
            blk = idx[(b * N_HEADS + h * GROUP + g) * MOBA_TOPK + sel]
            return (layer, pt[b * n_pages + blk * PAGES_PER_BLOCK + half], 0, 0)

        return pl.BlockSpec((1, 1, rows, hd), index_map)

    group_spec = pl.BlockSpec((1, GROUP, 1, hd), lambda b, h, idx, pt: (b, h, 0, 0))
    kv_spec = pl.BlockSpec((1, 1, 1, hd), lambda b, h, idx, pt: (b, h, 0, 0))
    n_in = GROUP * N_SEL_PAGES
    return pl.pallas_call(
        functools.partial(_moba_step_body, hd=hd),
        grid_spec=pltpu.PrefetchScalarGridSpec(
            num_scalar_prefetch=2, grid=(Bs, N_KV_HEADS),
            in_specs=[group_spec, kv_spec, kv_spec] + [page_spec(r) for r in range(n_in)] * 2,
            out_specs=group_spec),
        out_shape=jax.ShapeDtypeStruct((Bs, N_HEADS, 1, hd), BF16),
        compiler_params=_params("parallel", "arbitrary"), name="moba_step")(
            idx_flat, pt_flat, q4, kn4, vn4, *([cache_k4] * n_in), *([cache_v4] * n_in))


def _pad_rows(a, rows):
    return jnp.pad(a, ((0, rows - a.shape[0]),) + ((0, 0),) * (a.ndim - 1))


def kernel(x_prompt, x_sample, cache_k, cache_v, state_pool, state_C, state_n, state_m, page_table,
           w_in_mix, b_if, w_pool, pool_scale, mlstm_norm_g, w_out_mix, w_qkv, w_o,
           ln_mix_g, ln_mix_b, ln_ffn_g, ln_ffn_b, w_up, w_down):
    B, S, D = x_prompt.shape
    Bs = x_sample.shape[0]
    assert x_sample.shape[1] == 1 and Bs <= SAMPLE_ROWS
    depth = w_up.shape[0]
    d_pool = state_pool.shape[-1]
    dk = state_C.shape[-1]
    d_ml = MH * dk
    d_main = d_pool + 4 * d_ml
    assert w_in_mix.shape[2] == d_main + 2 * MH
    hd = cache_k.shape[-1]
    d_q, d_kv = N_HEADS * hd, N_KV_HEADS * hd
    n_pages = page_table.shape[1]
    past_len = n_pages * PAGE_SIZE
    assert cache_k.shape[2] == PAGE_SIZE and past_len % MOBA_BLOCK == 0
    alpha = (2 * depth) ** 0.25
    M = B * S
    L = math.gcd(S, MLSTM_CHUNK)

    w_in_t = jnp.swapaxes(w_in_mix, 1, 2)
    bias_g = jnp.pad(b_if, ((0, 0), (0, LANES - 2 * MH)))[:, None, :]
    w_pool_b = w_pool.astype(BF16)

    cache_k4 = cache_k.reshape(cache_k.shape[:2] + (PAGE_SIZE * N_KV_HEADS, hd))
    cache_v4 = cache_v.reshape(cache_v.shape[:2] + (PAGE_SIZE * N_KV_HEADS, hd))
    pt_flat = page_table.reshape(-1)
    kmean_past = None

    xs = _pad_rows(x_sample.reshape(Bs, D), SAMPLE_ROWS)
    xp = x_prompt.reshape(M, D)
    ks_, vs_, pool_s, C_s, n_s, m_s = [], [], [], [], [], []
    pool_p, C_p, n_p, m_p = [], [], [], []
    n_attn = cache_k.shape[0]
    kv_flat = None
    for l in range(depth):
        i = l // 2
        ln_g, ln_b = ln_mix_g[l][None, :], ln_mix_b[l][None, :]
        fg, fb = ln_ffn_g[l][None, :], ln_ffn_b[l][None, :]
        if l % 2 == 0:
            sc = pool_scale[i][None, :]
            hg = mlstm_norm_g[i][None, :]
            zs, gs, wb, wgb = _proj_gates_step(xs, w_in_t, i, n_main=d_main, n_gate=2 * MH, tn=STEP_TN)
            us = zs[:Bs, :d_pool]
            ys_pool = _pool_step(jnp.transpose(state_pool[i], (1, 0, 2)), us, w_pool_b[i], sc, pos0=past_len)
            hns, C, n, m = _mlstm_step(
                zs[:Bs, None, :], gs[:Bs, None, :], bias_g[i], hg, state_C[i],
                state_n[i][:, :, None, :], state_m[i][:, :, None, None], d_pool=d_pool, d_ml=d_ml)
            a = _pad_rows(jnp.concatenate([ys_pool, hns[:, 0, :]], axis=1), SAMPLE_ROWS)
            xs, wpb = _proj_ln_step(a, w_out_mix, i, xs, ln_g, ln_b, alpha=alpha, tk=STEP_TK)
            pool_s.append(jnp.concatenate([state_pool[i][:, 1:, :], us[:, None, :]], axis=1))
            C_s.append(C)
            n_s.append(n[:, :, 0, :])
            m_s.append(m[:, :, 0, 0])
        else:
            qs, k_new, v_new, wb = _proj_qkv_step(xs, w_qkv, i, d_q=d_q, d_kv=d_kv)
            q_s = qs[:Bs].astype(F32).reshape(Bs, N_HEADS, hd)
            k_new = k_new[:Bs].reshape(Bs, N_KV_HEADS, 1, hd)
            v_new = v_new[:Bs].reshape(Bs, N_KV_HEADS, 1, hd)
            idx = _topk_blocks(q_s, kmean_past, i)[:, :, :MOBA_TOPK].reshape(-1)
            os_ = _moba_step(q_s[:, :, None, :], k_new, v_new, cache_k4, cache_v4, idx, pt_flat,
                             layer=i, n_pages=n_pages)
            xs, wpb = _proj_ln_step(_pad_rows(os_.reshape(Bs, d_q), SAMPLE_ROWS), w_o, i, xs, ln_g, ln_b,
                                    alpha=alpha, tk=STEP_TK)
            ks_.append(k_new.reshape(Bs, 1, N_KV_HEADS, hd))
            vs_.append(v_new.reshape(Bs, 1, N_KV_HEADS, hd))
        if l == 0:
            xs, wub, wdb = _ffn_step(xs, w_up, w_down, l, fg, fb, alpha=alpha, tf=STEP_TF)
        else:
            xs, = _ffn(xs, wub[None], wdb[None], 0, fg, fb, alpha=alpha, tm=SAMPLE_ROWS, tf=FFN_TF)

        if l % 2 == 0:
            z, gates = _proj_gates(xp, wb, wgb, tm=PROJ_TM, tn=PROJ_TN)
            y_pool = _pool_prompt(z, w_pool_b, i, sc, B=B, S=S, d_pool=d_pool, T=POOL_T)
            hn, C, n, m = _mlstm_prompt(z, gates, bias_g[i], hg, B=B, S=S, d_pool=d_pool, d_ml=d_ml, L=L)
            xp = _proj_ln([y_pool, hn], wpb[None], 0, xp, ln_g, ln_b, alpha=alpha, tm=LN_TM)
            pool_p.append(z.reshape(B, S, d_main)[:, S - POOL_BUF:, :d_pool])
            C_p.append(C)
            n_p.append(n.reshape(B, MH, dk))
            m_p.append(m[:, :, 0, 0])
        else:
            q, k, v, *kv_flat = _proj_qkv(xp, wb[None], 0, d_q=d_q, d_kv=d_kv, hd=hd, tm=PROJ_TM,
                                          slab=i, n_slabs=n_attn, prev=kv_flat)
            o = _moba_seq(q, k, v, B=B, S=S, hd=hd)
            xp = _proj_ln([o], wpb[None], 0, xp, ln_g, ln_b, alpha=alpha, tm=LN_TM)
        res = _ffn(xp, wub[None], wdb[None], 0, fg, fb, alpha=alpha, tm=FFN_TM, tf=FFN_TF,
                   paged=(cache_k4, pt_flat, Bs, n_pages) if kmean_past is None else None,
                   cast_next=(w_up, w_down, l + 1) if l + 1 < depth else None)
        xp = res[0]
        if kmean_past is None:
            kmean_past = res[1]
        if l + 1 < depth:
            wub, wdb = res[-2:]

    k_prompt, v_prompt = (t.reshape(n_attn, B, S, N_KV_HEADS, hd) for t in kv_flat)
    return (xp.reshape(B, S, D), xs[:Bs].reshape(Bs, 1, D),
            k_prompt, v_prompt, jnp.stack(ks_), jnp.stack(vs_),
            jnp.stack(pool_p), jnp.stack(C_p), jnp.stack(n_p), jnp.stack(m_p),
            jnp.stack(pool_s), jnp.stack(C_s), jnp.stack(n_s), jnp.stack(m_s))
```
